```python
import math
import jax, jax.numpy as jnp
from jax import lax
import numpy as np

D_MODEL = 1024
BATCH = 8
SEQ = 2048
DEPTH = 1
DEC_BATCH = 128
DEC_SEQ = 1
PAST_LEN = 16384
PAGE_SIZE = 128

D_MIX = 2 * D_MODEL
D_S5 = D_MIX // 2
S5_CH = 16
S5_GROUPS = D_S5 // S5_CH
S5_STATE = 64
D_SSD = D_MIX - D_S5
SSD_HEAD_DIM = 64
SSD_HEADS = D_SSD // SSD_HEAD_DIM
SSD_GROUPS = 2
SSD_HPG = SSD_HEADS // SSD_GROUPS
SSD_STATE = 128
SSD_CONV = 4
SSD_CHUNK = 128
D_CONV = D_SSD + 2 * SSD_GROUPS * SSD_STATE
D_IN_PROJ = D_S5 + D_SSD + D_CONV + SSD_HEADS
D_FF = 4 * D_MODEL
EPS = 1e-5

kernel_name = "hymba_s5_ssd_decoder_step"


def rmsnorm(x, w):
    xf = x.astype(jnp.float32)
    xf = xf * lax.rsqrt(jnp.mean(xf * xf, axis=-1, keepdims=True) + EPS)
    return (xf * w.astype(jnp.float32)).astype(x.dtype)


def _complex_affine_combine(e1, e2):
    a1r, a1i, b1r, b1i = e1
    a2r, a2i, b2r, b2i = e2
    ar = a2r * a1r - a2i * a1i
    ai = a2r * a1i + a2i * a1r
    br = a2r * b1r - a2i * b1i + b2r
    bi = a2r * b1i + a2i * b1r + b2i
    return (ar, ai, br, bi)


def s5_mixer(u, h0_re, h0_im, lam_re, lam_im, log_dt, b_re, b_im, c_re, c_im, d_s5, w_glu, b_glu):
    bsz, L, _ = u.shape
    uf = u.astype(jnp.float32).reshape(bsz, L, S5_GROUPS, S5_CH)
    dt = jnp.exp(log_dt.astype(jnp.float32))[:, None]
    lr = lam_re.astype(jnp.float32)
    li = lam_im.astype(jnp.float32)
    mag = jnp.exp(lr * dt)
    ab_re = mag * jnp.cos(li * dt)
    ab_im = mag * jnp.sin(li * dt)
    den = lr * lr + li * li
    nr, ni = ab_re - 1.0, ab_im
    f_re = (nr * lr + ni * li) / den
    f_im = (ni * lr - nr * li) / den
    br = b_re.astype(jnp.float32)
    bi = b_im.astype(jnp.float32)
    bb_re = f_re[..., None] * br - f_im[..., None] * bi
    bb_im = f_re[..., None] * bi + f_im[..., None] * br
    bu_re = jnp.einsum('gpc,blgc->blgp', bb_re, uf)
    bu_im = jnp.einsum('gpc,blgc->blgp', bb_im, uf)
    a_re = jnp.broadcast_to(ab_re, bu_re.shape)
    a_im = jnp.broadcast_to(ab_im, bu_im.shape)
    A_re, A_im, s_re, s_im = lax.associative_scan(
        _complex_affine_combine, (a_re, a_im, bu_re, bu_im), axis=1)
    h0r = h0_re.astype(jnp.float32)[:, None]
    h0i = h0_im.astype(jnp.float32)[:, None]
    h_re = A_re * h0r - A_im * h0i + s_re
    h_im = A_re * h0i + A_im * h0r + s_im
    y = (jnp.einsum('gcp,blgp->blgc', c_re.astype(jnp.float32), h_re)
         - jnp.einsum('gcp,blgp->blgc', c_im.astype(jnp.float32), h_im)
         + d_s5.astype(jnp.float32).reshape(S5_GROUPS, S5_CH) * uf)
    y = y.reshape(bsz, L, D_S5)
    g = jax.nn.gelu(y)
    out = g * jax.nn.sigmoid(g @ w_glu.astype(jnp.float32) + b_glu.astype(jnp.float32))
    return out.astype(u.dtype), h_re[:, -1], h_im[:, -1]


def ssd_chunked(x, dt, A, B, C, h0):
    bsz, L = x.shape[:2]
    Q = min(SSD_CHUNK, L)
    pad = (-L) % Q
    if pad:
        padw = lambda t: jnp.pad(t, [(0, 0), (0, pad)] + [(0, 0)] * (t.ndim - 2))
        x, dt, B, C = padw(x), padw(dt), padw(B), padw(C)
    nc = (L + pad) // Q
    x = x.reshape(bsz, nc, Q, SSD_GROUPS, SSD_HPG, SSD_HEAD_DIM)
    dt = dt.reshape(bsz, nc, Q, SSD_GROUPS, SSD_HPG)
    B = B.reshape(bsz, nc, Q, SSD_GROUPS, SSD_STATE)
    C = C.reshape(bsz, nc, Q, SSD_GROUPS, SSD_STATE)
    a = jnp.moveaxis(dt * A, 2, -1)
    a_cum = jnp.cumsum(a, axis=-1)
    mask = jnp.tril(jnp.ones((Q, Q), dtype=bool))
    seg = a_cum[..., :, None] - a_cum[..., None, :]
    Lmat = jnp.exp(jnp.where(mask, seg, -jnp.inf))
    xdt = x * dt[..., None]
    cb = jnp.einsum('bclgn,bcsgn->bcgls', C, B)
    y_diag = jnp.einsum('bcgls,bcgrls,bcsgrp->bclgrp', cb, Lmat, xdt)
    decay_states = jnp.exp(a_cum[..., -1:] - a_cum)
    states = jnp.einsum('bclgn,bcgrl,bclgrp->bcgrpn', B, decay_states, xdt)
    chunk_decay = jnp.exp(a_cum[..., -1])

    def step(h, inp):
        dec, st = inp
        return dec[..., None, None] * h + st, h

    h_last, h_prev = lax.scan(step, h0, (jnp.moveaxis(chunk_decay, 1, 0), jnp.moveaxis(states, 1, 0)))
    h_prev = jnp.moveaxis(h_prev, 0, 1)
    y_off = jnp.einsum('bclgn,bcgrpn,bcgrl->bclgrp', C, h_prev, jnp.exp(a_cum))
    y = (y_diag + y_off).reshape(bsz, nc * Q, SSD_GROUPS, SSD_HPG, SSD_HEAD_DIM)[:, :L]
    return y, h_last


def ssd_mixer(z, xbc, dt_raw, h0, conv_buf, conv_w, conv_b, a_log, dt_bias, d_ssd, norm_w):
    bsz, L, _ = xbc.shape
    xpad = jnp.concatenate([conv_buf.astype(xbc.dtype), xbc], axis=1)
    new_conv = xpad[:, -(SSD_CONV - 1):]
    conv = conv_b + sum(xpad[:, k:k + L] * conv_w[k] for k in range(SSD_CONV))
    conv = jax.nn.silu(conv.astype(jnp.float32))
    xs = conv[..., :D_SSD].reshape(bsz, L, SSD_GROUPS, SSD_HPG, SSD_HEAD_DIM)
    Bm = conv[..., D_SSD:D_SSD + SSD_GROUPS * SSD_STATE].reshape(bsz, L, SSD_GROUPS, SSD_STATE)
    Cm = conv[..., D_SSD + SSD_GROUPS * SSD_STATE:].reshape(bsz, L, SSD_GROUPS, SSD_STATE)
    dt = jax.nn.softplus(dt_raw.astype(jnp.float32) + dt_bias.astype(jnp.float32))
    dt = dt.reshape(bsz, L, SSD_GROUPS, SSD_HPG)
    A = -jnp.exp(a_log.astype(jnp.float32)).reshape(SSD_GROUPS, SSD_HPG)
    h0g = h0.astype(jnp.float32).reshape(bsz, SSD_GROUPS, SSD_HPG, SSD_HEAD_DIM, SSD_STATE)
    y, h_last = ssd_chunked(xs, dt, A, Bm, Cm, h0g)
    y = y + d_ssd.astype(jnp.float32).reshape(SSD_GROUPS, SSD_HPG, 1) * xs
    y = y.reshape(bsz, L, D_SSD) * jax.nn.silu(z.astype(jnp.float32))
    yg = y.reshape(bsz, L, SSD_GROUPS, D_SSD // SSD_GROUPS)
    yg = yg * lax.rsqrt(jnp.mean(yg * yg, axis=-1, keepdims=True) + EPS)
    y = yg.reshape(bsz, L, D_SSD) * norm_w.astype(jnp.float32)
    h_last = h_last.reshape(bsz, SSD_HEADS, SSD_HEAD_DIM, SSD_STATE)
    return y.astype(z.dtype), h_last, new_conv


def hybrid_layer(x, h5_re, h5_im, h_ssd, conv_buf, p):
    h = rmsnorm(x, p['norm_mix_w'])
    proj = h @ p['w_in']
    o1, o2, o3 = D_S5, D_S5 + D_SSD, D_S5 + D_SSD + D_CONV
    u, z, xbc, dt_raw = proj[..., :o1], proj[..., o1:o2], proj[..., o2:o3], proj[..., o3:]
    y5, n5_re, n5_im = s5_mixer(u, h5_re, h5_im, p['s5_lam_re'], p['s5_lam_im'], p['s5_log_dt'],
                                p['s5_b_re'], p['s5_b_im'], p['s5_c_re'], p['s5_c_im'], p['s5_d'],
                                p['s5_w_glu'], p['s5_b_glu'])
    y5 = rmsnorm(y5, p['s5_norm_w'])
    yssd, n_ssd, n_conv = ssd_mixer(z, xbc, dt_raw, h_ssd, conv_buf, p['ssd_conv_w'], p['ssd_conv_b'],
                                    p['ssd_a_log'], p['ssd_dt_bias'], p['ssd_d'], p['ssd_norm_w'])
    x = x + jnp.concatenate([y5, yssd], axis=-1) @ p['w_out']
    hf = rmsnorm(x, p['norm_ffn_w'])
    x = x + jnp.square(jax.nn.relu(hf @ p['w_ff1'])) @ p['w_ff2']
    return x, n5_re, n5_im, n_ssd, n_conv


def setup_inputs(seed: int = 0) -> dict:
    key = jax.random.key(seed)
    ks = jax.random.split(key, 32)
    nrm = lambda k, shape, s: jax.random.normal(k, shape, jnp.float32) * s
    Ld = DEPTH
    dt_ssd = jnp.exp(jax.random.uniform(ks[20], (Ld, SSD_HEADS), jnp.float32, math.log(1e-3), math.log(1e-1)))
    return {
        "x_prompt": nrm(ks[0], (BATCH, SEQ, D_MODEL), 1.0),
        "x_sample": nrm(ks[1], (DEC_BATCH, DEC_SEQ, D_MODEL), 1.0),
        "state_s5_re": nrm(ks[2], (Ld, DEC_BATCH, S5_GROUPS, S5_STATE), 0.5),
        "state_s5_im": nrm(ks[3], (Ld, DEC_BATCH, S5_GROUPS, S5_STATE), 0.5),
        "state_ssd": nrm(ks[4], (Ld, DEC_BATCH, SSD_HEADS, SSD_HEAD_DIM, SSD_STATE), 0.1),
        "state_conv": nrm(ks[5], (Ld, DEC_BATCH, SSD_CONV - 1, D_CONV), 1.0),
        "norm_mix_w": 1.0 + nrm(ks[6], (Ld, D_MODEL), 0.02),
        "w_in": nrm(ks[7], (Ld, D_MODEL, D_IN_PROJ), D_MODEL ** -0.5),
        "s5_lam_re": -0.5 + nrm(ks[8], (Ld, S5_GROUPS, S5_STATE), 0.01),
        "s5_lam_im": jnp.broadcast_to(math.pi * jnp.arange(S5_STATE, dtype=jnp.float32), (Ld, S5_GROUPS, S5_STATE))
                      + nrm(ks[9], (Ld, S5_GROUPS, S5_STATE), 0.01),
        "s5_log_dt": jax.random.uniform(ks[10], (Ld, S5_GROUPS), jnp.float32, math.log(1e-3), math.log(1e-1)),
        "s5_b_re": nrm(ks[11], (Ld, S5_GROUPS, S5_STATE, S5_CH), (2 * S5_CH) ** -0.5),
        "s5_b_im": nrm(ks[12], (Ld, S5_GROUPS, S5_STATE, S5_CH), (2 * S5_CH) ** -0.5),
        "s5_c_re": nrm(ks[13], (Ld, S5_GROUPS, S5_CH, S5_STATE), (2 * S5_STATE) ** -0.5),
        "s5_c_im": nrm(ks[14], (Ld, S5_GROUPS, S5_CH, S5_STATE), (2 * S5_STATE) ** -0.5),
        "s5_d": nrm(ks[15], (Ld, D_S5), 1.0),
        "s5_w_glu": nrm(ks[16], (Ld, D_S5, D_S5), D_S5 ** -0.5),
        "s5_b_glu": nrm(ks[17], (Ld, D_S5), 0.01),
        "s5_norm_w": 1.0 + nrm(ks[18], (Ld, D_S5), 0.02),
        "ssd_conv_w": nrm(ks[19], (Ld, SSD_CONV, D_CONV), SSD_CONV ** -0.5),
        "ssd_conv_b": nrm(ks[21], (Ld, D_CONV), 0.01),
        "ssd_a_log": jnp.log(jax.random.uniform(ks[22], (Ld, SSD_HEADS), jnp.float32, 1.0, 16.0)),
        "ssd_dt_bias": dt_ssd + jnp.log(-jnp.expm1(-dt_ssd)),
        "ssd_d": 1.0 + nrm(ks[23], (Ld, SSD_HEADS), 0.02),
        "ssd_norm_w": 1.0 + nrm(ks[24], (Ld, D_SSD), 0.02),
        "w_out": nrm(ks[25], (Ld, D_MIX, D_MODEL), D_MIX ** -0.5),
        "norm_ffn_w": 1.0 + nrm(ks[26], (Ld, D_MODEL), 0.02),
        "w_ff1": nrm(ks[27], (Ld, D_MODEL, D_FF), D_MODEL ** -0.5),
        "w_ff2": nrm(ks[28], (Ld, D_FF, D_MODEL), D_FF ** -0.5),
        "norm_final_w": 1.0 + nrm(ks[29], (D_MODEL,), 0.02),
    }


def reference(x_prompt, x_sample, state_s5_re, state_s5_im, state_ssd, state_conv,
              norm_mix_w, w_in, s5_lam_re, s5_lam_im, s5_log_dt, s5_b_re, s5_b_im, s5_c_re, s5_c_im,
              s5_d, s5_w_glu, s5_b_glu, s5_norm_w, ssd_conv_w, ssd_conv_b, ssd_a_log, ssd_dt_bias,
              ssd_d, ssd_norm_w, w_out, norm_ffn_w, w_ff1, w_ff2, norm_final_w):
    bp = x_prompt.shape[0]
    xp, xs = x_prompt, x_sample
    np_re, np_im, np_ssd, np_conv = [], [], [], []
    ns_re, ns_im, ns_ssd, ns_conv = [], [], [], []
    for l in range(DEPTH):
        p = dict(norm_mix_w=norm_mix_w[l], w_in=w_in[l], s5_lam_re=s5_lam_re[l], s5_lam_im=s5_lam_im[l],
                 s5_log_dt=s5_log_dt[l], s5_b_re=s5_b_re[l], s5_b_im=s5_b_im[l], s5_c_re=s5_c_re[l],
                 s5_c_im=s5_c_im[l], s5_d=s5_d[l], s5_w_glu=s5_w_glu[l], s5_b_glu=s5_b_glu[l],
                 s5_norm_w=s5_norm_w[l], ssd_conv_w=ssd_conv_w[l], ssd_conv_b=ssd_conv_b[l],
                 ssd_a_log=ssd_a_log[l], ssd_dt_bias=ssd_dt_bias[l], ssd_d=ssd_d[l], ssd_norm_w=ssd_norm_w[l],
                 w_out=w_out[l], norm_ffn_w=norm_ffn_w[l], w_ff1=w_ff1[l], w_ff2=w_ff2[l])
        z5 = jnp.zeros((bp, S5_GROUPS, S5_STATE), jnp.float32)
        zssd = jnp.zeros((bp, SSD_HEADS, SSD_HEAD_DIM, SSD_STATE), jnp.float32)
        zconv = jnp.zeros((bp, SSD_CONV - 1, D_CONV), x_prompt.dtype)
        xp, a, b, c, d = hybrid_layer(xp, z5, z5, zssd, zconv, p)
        np_re.append(a); np_im.append(b); np_ssd.append(c); np_conv.append(d)
        xs, a, b, c, d = hybrid_layer(xs, state_s5_re[l], state_s5_im[l], state_ssd[l], state_conv[l], p)
        ns_re.append(a); ns_im.append(b); ns_ssd.append(c); ns_conv.append(d)
    y_prompt = rmsnorm(xp, norm_final_w)
    y_sample = rmsnorm(xs, norm_final_w)
    return (y_prompt, y_sample,
            jnp.stack(np_re), jnp.stack(np_im), jnp.stack(np_ssd), jnp.stack(np_conv),
            jnp.stack(ns_re), jnp.stack(ns_im), jnp.stack(ns_ssd), jnp.stack(ns_conv))
```

```python
import functools

import jax
import jax.numpy as jnp
from jax import lax
from jax.experimental import pallas as pl
from jax.experimental.pallas import tpu as pltpu

F32 = jnp.float32
BF16 = jnp.bfloat16
HI = lax.Precision.HIGHEST
EPS = 1e-5

D_MODEL = 1024
BATCH = 8
SEQ = 2048
DEC_BATCH = 128
D_S5 = 1024
S5_CH = 16
S5_GROUPS = 64
S5_STATE = 64
D_SSD = 1024
SSD_HEADS = 16
SSD_HEAD_DIM = 64
SSD_STATE = 128
SSD_CONV = 4
D_CONV = 1536
D_FF = 4096

LANES = 128
SUBLANES = 8
Q5 = 32
NC5 = SEQ // Q5
M5 = NC5 * BATCH
R5 = Q5 * S5_CH
QS = 128
TL = 512
TM_FF = 512
SB = 8
VMEM_LIMIT = 56 * 1024 * 1024


def _cp(*sem):
    return pltpu.CompilerParams(dimension_semantics=sem, vmem_limit_bytes=VMEM_LIMIT)


def _rms(x, w):
    return x * lax.rsqrt(jnp.mean(x * x, axis=-1, keepdims=True) + EPS) * w


def _dot(a, b, precision=None):
    return jnp.dot(a, b, preferred_element_type=F32, precision=precision)


def _dot_nt(a, b, precision=None):
    return lax.dot_general(a, b, (((1,), (1,)), ((), ())), preferred_element_type=F32,
                           precision=precision)


def _dot_tn(a, b, precision=None):
    return lax.dot_general(a, b, (((0,), (0,)), ((), ())), preferred_element_type=F32,
                           precision=precision)


def _iota(shape, dim):
    return lax.broadcasted_iota(jnp.int32, shape, dim)


def _silu(x):
    return x * jax.nn.sigmoid(x)


def _s5_prep_kernel(lrc_ref, lic_ref, lr2_ref, li2_ref, ldt_ref, bre_ref, bim_ref, ccat_ref,
                    c2re_ref, c2im_ref, dcol_ref, bt2re_ref, bt2im_ref, ctre_ref, ctim_ref,
                    ws_ref, t_ref, wo_ref, aq_ref, bbt_ref, cst_ref):
    dt = jnp.exp(ldt_ref[...])

    def zoh(lr, li):
        mag = jnp.exp(lr * dt)
        ab_re = mag * jnp.cos(li * dt)
        ab_im = mag * jnp.sin(li * dt)
        den = lr * lr + li * li
        nr, ni = ab_re - 1.0, ab_im
        return (nr * lr + ni * li) / den, (ni * lr - nr * li) / den

    def powers(lr, li, e):
        mag = jnp.exp(e * (lr * dt))
        ang = e * (li * dt)
        return mag * jnp.cos(ang), mag * jnp.sin(ang)

    lrc, lic = lrc_ref[...], lic_ref[...]
    f_re, f_im = zoh(lrc, lic)
    bre, bim = bre_ref[...], bim_ref[...]
    bb_re = f_re * bre - f_im * bim
    bb_im = f_re * bim + f_im * bre
    e_l = _iota((S5_STATE, LANES), 1).astype(F32)
    pw_re, pw_im = powers(lrc, lic, e_l)
    row = _iota((LANES, R5), 0)
    col = _iota((LANES, R5), 1)
    step = lax.shift_right_logical(col, 4)
    sel_pow = (row == (Q5 - 1) - step).astype(F32)
    sel_ch = (row == (col & (S5_CH - 1))).astype(F32)
    ap_re = _dot(pw_re, sel_pow, HI)
    ap_im = _dot(pw_im, sel_pow, HI)
    bt_re = _dot(bb_re, sel_ch, HI)
    bt_im = _dot(bb_im, sel_ch, HI)
    ws_re = ap_re * bt_re - ap_im * bt_im
    ws_im = ap_re * bt_im + ap_im * bt_re
    ws_ref[...] = jnp.concatenate([ws_re, ws_im], axis=0).astype(BF16)

    vmat = _dot(ccat_ref[...], jnp.concatenate([ws_re, -ws_im], axis=0), HI)
    vrow = _iota((S5_CH, R5), 0)
    vcol = _iota((S5_CH, R5), 1)
    vmat = vmat + jnp.where(vcol == (Q5 - 1) * S5_CH + vrow, dcol_ref[...], 0.0)
    for l in range(Q5):
        width = (l + 1) * S5_CH
        r = vmat if width == R5 else pltpu.roll(vmat, width, axis=1)
        t_ref[l * S5_CH:(l + 1) * S5_CH, :] = jnp.where(vcol < width, r, 0.0).astype(BF16)

    lr2, li2 = lr2_ref[...], li2_ref[...]
    e_s = _iota((LANES, LANES), 0).astype(F32)
    p2_re, p2_im = powers(lr2, li2, e_s)
    rrow = _iota((R5, LANES), 0)
    rcol = _iota((R5, LANES), 1)
    sel_l = (rcol == lax.shift_right_logical(rrow, 4) + 1).astype(F32)
    a2_re = _dot(sel_l, p2_re, HI)
    a2_im = _dot(sel_l, p2_im, HI)
    c2re = jnp.broadcast_to(c2re_ref[...][None], (Q5, S5_CH, LANES)).reshape(R5, LANES)
    c2im = jnp.broadcast_to(c2im_ref[...][None], (Q5, S5_CH, LANES)).reshape(R5, LANES)
    lo = rcol < S5_STATE
    wo_ref[...] = jnp.where(lo, c2re * a2_re - c2im * a2_im,
                            -(c2re * a2_im + c2im * a2_re)).astype(BF16)

    lane1 = _iota((1, LANES), 1) < S5_STATE
    aq_re, aq_im = p2_re[Q5:Q5 + 1, :], p2_im[Q5:Q5 + 1, :]
    a1_re, a1_im = p2_re[1:2, :], p2_im[1:2, :]
    aq_ref[...] = jnp.concatenate(
        [aq_re, jnp.where(lane1, -aq_im, aq_im), jnp.where(lane1, aq_im, -aq_im),
         a1_re, jnp.where(lane1, -a1_im, a1_im), jnp.zeros((3, LANES), F32)], axis=0)

    f2_re, f2_im = zoh(lr2, li2)
    bt2re, bt2im = bt2re_ref[...], bt2im_ref[...]
    bbt_ref[...] = jnp.where(_iota((S5_CH, LANES), 1) < S5_STATE,
                             f2_re * bt2re - f2_im * bt2im, f2_re * bt2im + f2_im * bt2re)
    cst_ref[...] = jnp.concatenate([ctre_ref[...], -ctim_ref[...]], axis=0)


def _s5_prep(lam_re, lam_im, log_dt, b_re, b_im, c_re, c_im, d):
    G, P, C = S5_GROUPS, S5_STATE, S5_CH
    pad_b = lambda b: jnp.pad(b, ((0, 0), (0, 0), (0, LANES - C)))
    dup = lambda a: jnp.concatenate([a, a], axis=-1)
    args = (
        lam_re.reshape(G, P, 1), lam_im.reshape(G, P, 1),
        dup(lam_re).reshape(G, 1, 2 * P), dup(lam_im).reshape(G, 1, 2 * P),
        log_dt.reshape(G, 1, 1), pad_b(b_re), pad_b(b_im),
        jnp.concatenate([c_re, c_im], axis=-1), dup(c_re), dup(c_im),
        d.reshape(G, C, 1),
        dup(jnp.swapaxes(b_re, 1, 2)), dup(jnp.swapaxes(b_im, 1, 2)),
        jnp.swapaxes(c_re, 1, 2), jnp.swapaxes(c_im, 1, 2),
    )
    spec = lambda a: pl.BlockSpec((None,) + a.shape[1:], lambda g: (g, 0, 0))
    out_shapes = (
        jax.ShapeDtypeStruct((G, 2 * P, R5), BF16),
        jax.ShapeDtypeStruct((G, R5, R5), BF16),
        jax.ShapeDtypeStruct((G, R5, 2 * P), BF16),
        jax.ShapeDtypeStruct((G, 8, 2 * P), F32),
        jax.ShapeDtypeStruct((G, C, 2 * P), F32),
        jax.ShapeDtypeStruct((G, 2 * P, C), F32),
    )
    return pl.pallas_call(
        _s5_prep_kernel, grid=(G,),
        in_specs=[spec(a) for a in args],
        out_specs=[pl.BlockSpec((None,) + s.shape[1:], lambda g: (g, 0, 0)) for s in out_shapes],
        out_shape=out_shapes, compiler_params=_cp("arbitrary"), name="s5_prep",
    )(*args)


def _s5_inproj_kernel(x_ref, nw_ref, wut_ref, ut_ref):
    x = x_ref[...].reshape(M5, D_MODEL)
    hn = _rms(x, nw_ref[...]).astype(BF16)
    ut = _dot_nt(wut_ref[...], hn)
    ut_ref[...] = ut.astype(BF16).reshape(S5_GROUPS, S5_CH, M5)


def _s5_core_kernel(ut_ref, t_ref, ws_ref, wo_ref, aq_ref, g_ref, hfin_ref, hs_scr):
    ut = ut_ref[...]
    sp = jnp.transpose(_dot(ws_ref[...], ut))
    sq = pltpu.roll(sp, S5_STATE, axis=1)
    a1 = jnp.broadcast_to(aq_ref[0:1, :], (BATCH, LANES))
    a2 = jnp.broadcast_to(aq_ref[1:2, :], (BATCH, LANES))
    a3 = jnp.broadcast_to(aq_ref[2:3, :], (BATCH, LANES))
    hp = jnp.zeros((BATCH, LANES), F32)
    hq = jnp.zeros((BATCH, LANES), F32)
    for k in range(NC5):
        rows = slice(k * BATCH, (k + 1) * BATCH)
        hs_scr[rows, :] = hp
        hp, hq = a1 * hp + a2 * hq + sp[rows, :], a1 * hq + a3 * hp + sq[rows, :]
    hfin_ref[...] = hp
    y = _dot(t_ref[...], ut) + _dot_nt(wo_ref[...], hs_scr[...].astype(BF16))
    g_ref[...] = jax.nn.gelu(y)


def _s5_out_kernel(g_ref, wglut_ref, bglu_ref, nw_ref, wout_ref, o_ref):
    g = g_ref[...].reshape(D_S5, M5)
    z = _dot(wglut_ref[...], g.astype(BF16)) + bglu_ref[...]
    out = g * jax.nn.sigmoid(z)
    ms = jnp.mean(out * out, axis=0, keepdims=True)
    y5 = out * lax.rsqrt(ms + EPS) * nw_ref[...]
    o = _dot_tn(y5.astype(BF16), wout_ref[...])
    o_ref[...] = o.reshape(NC5, BATCH, D_MODEL)


def _s5_prompt(x_prompt, norm_mix_w, wut, prep, wglut, b_glu, s5_norm_w, wout5):
    ws, tmat, wo, aq = prep[:4]
    x_t = jnp.transpose(x_prompt, (1, 0, 2)).reshape(NC5, Q5, BATCH, D_MODEL)
    const2 = lambda s: (0, 0)
    ut = pl.pallas_call(
        _s5_inproj_kernel, grid=(Q5,),
        in_specs=[pl.BlockSpec((NC5, None, BATCH, D_MODEL), lambda s: (0, s, 0, 0)),
                  pl.BlockSpec((1, D_MODEL), const2),
                  pl.BlockSpec((D_S5, D_MODEL), const2)],
        out_specs=pl.BlockSpec((S5_GROUPS, S5_CH, M5), lambda s: (0, s, 0)),
        out_shape=jax.ShapeDtypeStruct((S5_GROUPS, R5, M5), BF16),
        compiler_params=_cp("arbitrary"), name="s5_inproj",
    )(x_t, norm_mix_w.reshape(1, D_MODEL), wut)

    per_g = lambda shape: pl.BlockSpec((None,) + shape, lambda g: (g, 0, 0))
    gact, hfin = pl.pallas_call(
        _s5_core_kernel, grid=(S5_GROUPS,),
        in_specs=[per_g((R5, M5)), per_g((R5, R5)), per_g((LANES, R5)), per_g((R5, LANES)),
                  per_g((8, LANES))],
        out_specs=[per_g((R5, M5)), per_g((BATCH, LANES))],
        out_shape=(jax.ShapeDtypeStruct((S5_GROUPS, R5, M5), F32),
                   jax.ShapeDtypeStruct((S5_GROUPS, BATCH, LANES), F32)),
        scratch_shapes=[pltpu.VMEM((M5, LANES), F32)],
        compiler_params=_cp("arbitrary"), name="s5_core",
    )(ut, tmat, ws, wo, aq)

    o5 = pl.pallas_call(
        _s5_out_kernel, grid=(Q5,),
        in_specs=[pl.BlockSpec((S5_GROUPS, S5_CH, M5), lambda s: (0, s, 0)),
                  pl.BlockSpec((D_S5, D_S5), const2),
                  pl.BlockSpec((D_S5, 1), const2),
                  pl.BlockSpec((D_S5, 1), const2),
                  pl.BlockSpec((D_S5, D_MODEL), const2)],
        out_specs=pl.BlockSpec((NC5, None, BATCH, D_MODEL), lambda s: (0, s, 0, 0)),
        out_shape=jax.ShapeDtypeStruct((NC5, Q5, BATCH, D_MODEL), F32),
        compiler_params=_cp("arbitrary"), name="s5_out",
    )(gact, wglut, b_glu.reshape(D_S5, 1), s5_norm_w.reshape(D_S5, 1), wout5)
    return o5.reshape(SEQ, BATCH * D_MODEL), hfin


def _ssd_prompt_kernel(x_ref, o5_ref, nw_ref, wz_ref, wxbc_ref, wdt_ref, cw_ref, cb_ref,
                       dtb_ref, alog_ref, drep_ref, snw_ref, wout_ref,
                       x1_ref, hout_ref, cout_ref,
                       xp_scr, xs_scr, b_scr, c_scr, z_scr, a_scr, dt_scr, y_scr, h_scr):
    j = pl.program_id(1)

    @pl.when(j == 0)
    def _():
        xp_scr[0:SUBLANES, :] = jnp.zeros((SUBLANES, D_CONV), F32)
        h_scr[...] = jnp.zeros(h_scr.shape, F32)

    x = x_ref[...]
    hn = _rms(x, nw_ref[...]).astype(BF16)
    z_scr[...] = _dot(hn, wz_ref[...])
    xp_scr[SUBLANES:, :] = _dot(hn, wxbc_ref[...])
    dt = jax.nn.softplus(_dot(hn, wdt_ref[...]) + dtb_ref[...])
    dt_scr[...] = dt
    a_scr[...] = dt * (-jnp.exp(alog_ref[...]))

    conv = cb_ref[...] + sum(
        xp_scr[SUBLANES - (SSD_CONV - 1) + k:SUBLANES - (SSD_CONV - 1) + k + TL, :] * cw_ref[k:k + 1, :]
        for k in range(SSD_CONV))
    tail = xp_scr[TL:TL + SUBLANES, :]
    xp_scr[0:SUBLANES, :] = tail
    cout_ref[...] = tail
    conv = _silu(conv)
    xs_scr[...] = conv[:, :D_SSD]
    b_scr[...] = conv[:, D_SSD:D_SSD + 2 * SSD_STATE]
    c_scr[...] = conv[:, D_SSD + 2 * SSD_STATE:]

    li = _iota((QS, QS), 0)
    si = _iota((QS, QS), 1)
    causal = li >= si
    tril = causal.astype(F32)
    lo = si < SSD_HEAD_DIM

    def chunk(ci, carry):
        r0 = pl.multiple_of(ci * QS, QS)
        rows = pl.ds(r0, QS)
        acum = _dot(tril, a_scr[rows, :], HI)
        acum_t = jnp.transpose(acum)
        dt_t = jnp.transpose(dt_scr[rows, :])
        alast = acum[QS - 1:QS, :]
        for g in range(2):
            bg = b_scr[rows, g * SSD_STATE:(g + 1) * SSD_STATE]
            cg = c_scr[rows, g * SSD_STATE:(g + 1) * SSD_STATE]
            bt = jnp.transpose(bg)
            cb = _dot(cg.astype(BF16), bt.astype(BF16))
            for hp in range(4):
                pr = 4 * g + hp
                cols = slice(pr * LANES, (pr + 1) * LANES)
                xs_pair = xs_scr[rows, cols]
                xs_b = xs_pair.astype(BF16)
                h_old = h_scr[pr]
                h_b = h_old.astype(BF16)
                ys, hs = [], []
                for h in (2 * pr, 2 * pr + 1):
                    acol = jnp.broadcast_to(acum[:, h:h + 1], (QS, QS))
                    arow = acum_t[h:h + 1, :]
                    dtrow = dt_t[h:h + 1, :]
                    lmat = jnp.where(causal, jnp.exp(jnp.where(causal, acol - arow, 0.0)), 0.0)
                    m = (cb * lmat * dtrow).astype(BF16)
                    cexp = (cg * jnp.exp(acol)).astype(BF16)
                    al = alast[:, h:h + 1]
                    bts = (bt * (jnp.exp(al - arow) * dtrow)).astype(BF16)
                    ys.append(_dot(m, xs_b) + _dot(cexp, h_b))
                    hs.append(jnp.exp(al) * h_old + _dot(bts, xs_b))
                h_scr[pr] = jnp.where(lo, hs[0], hs[1])
                y = jnp.where(lo, ys[0], ys[1]) + drep_ref[:, cols] * xs_pair
                y_scr[rows, cols] = y * _silu(z_scr[rows, cols])
        half = D_SSD // 2
        for g in range(2):
            yg = y_scr[rows, g * half:(g + 1) * half]
            yg = yg * lax.rsqrt(jnp.mean(yg * yg, axis=-1, keepdims=True) + EPS)
            y_scr[rows, g * half:(g + 1) * half] = yg * snw_ref[:, g * half:(g + 1) * half]
        return carry

    lax.fori_loop(0, TL // QS, chunk, 0)
    x1_ref[...] = x + o5_ref[...] + _dot(y_scr[...].astype(BF16), wout_ref[...])

    @pl.when(j == pl.num_programs(1) - 1)
    def _():
        hout_ref[...] = h_scr[...]


def _ssd_prompt(x_prompt, o5, norm_mix_w, wz, wxbc, wdt, conv_w, conv_b, dtb, alog, drep, snw, wouts):
    nj = SEQ // TL
    c2 = lambda b, j: (0, 0)
    row = lambda n: pl.BlockSpec((1, n), c2)
    return pl.pallas_call(
        _ssd_prompt_kernel, grid=(BATCH, nj),
        in_specs=[pl.BlockSpec((None, TL, D_MODEL), lambda b, j: (b, j, 0)),
                  pl.BlockSpec((TL, D_MODEL), lambda b, j: (j, b)),
                  row(D_MODEL),
                  pl.BlockSpec((D_MODEL, D_SSD), c2), pl.BlockSpec((D_MODEL, D_CONV), c2),
                  pl.BlockSpec((D_MODEL, LANES), c2),
                  pl.BlockSpec((SSD_CONV, D_CONV), c2), row(D_CONV),
                  row(LANES), row(LANES), row(D_SSD), row(D_SSD),
                  pl.BlockSpec((D_SSD, D_MODEL), c2)],
        out_specs=[pl.BlockSpec((None, TL, D_MODEL), lambda b, j: (b, j, 0)),
                   pl.BlockSpec((None, SSD_HEADS // 2, SSD_STATE, LANES), lambda b, j: (b, 0, 0, 0)),
                   pl.BlockSpec((None, SUBLANES, D_CONV), lambda b, j: (b, 0, 0))],
        out_shape=(jax.ShapeDtypeStruct((BATCH, SEQ, D_MODEL), F32),
                   jax.ShapeDtypeStruct((BATCH, SSD_HEADS // 2, SSD_STATE, LANES), F32),
                   jax.ShapeDtypeStruct((BATCH, SUBLANES, D_CONV), F32)),
        scratch_shapes=[pltpu.VMEM((SUBLANES + TL, D_CONV), F32),
                        pltpu.VMEM((TL, D_SSD), F32),
                        pltpu.VMEM((TL, 2 * SSD_STATE), F32),
                        pltpu.VMEM((TL, 2 * SSD_STATE), F32),
                        pltpu.VMEM((TL, D_SSD), F32),
                        pltpu.VMEM((TL, LANES), F32),
                        pltpu.VMEM((TL, LANES), F32),
                        pltpu.VMEM((TL, D_SSD), F32),
                        pltpu.VMEM((SSD_HEADS // 2, SSD_STATE, LANES), F32)],
        compiler_params=_cp("arbitrary", "arbitrary"), name="ssd_prompt",
    )(x_prompt, o5, norm_mix_w.reshape(1, D_MODEL), wz, wxbc, wdt, conv_w, conv_b, dtb, alog,
      drep, snw, wouts)


def _ffn_kernel(x_ref, nw_ref, w1_ref, w2_ref, fw_ref, o_ref):
    x = x_ref[...]
    hf = _rms(x, nw_ref[...]).astype(BF16)
    acc = x
    blk = D_FF // 4
    for c in range(4):
        h1 = jnp.square(jnp.maximum(_dot(hf, w1_ref[:, c * blk:(c + 1) * blk]), 0.0))
        acc = acc + _dot(h1.astype(BF16), w2_ref[c * blk:(c + 1) * blk, :])
    o_ref[...] = _rms(acc, fw_ref[...])


def _ffn(x1, norm_ffn_w, w1, w2, norm_final_w, tm):
    n = x1.shape[0]
    c2 = lambda i: (0, 0)
    return pl.pallas_call(
        _ffn_kernel, grid=(n // tm,),
        in_specs=[pl.BlockSpec((tm, D_MODEL), lambda i: (i, 0)),
                  pl.BlockSpec((1, D_MODEL), c2),
                  pl.BlockSpec((D_MODEL, D_FF), c2),
                  pl.BlockSpec((D_FF, D_MODEL), c2),
                  pl.BlockSpec((1, D_MODEL), c2)],
        out_specs=pl.BlockSpec((tm, D_MODEL), lambda i: (i, 0)),
        out_shape=jax.ShapeDtypeStruct((n, D_MODEL), F32),
        compiler_params=_cp("arbitrary"), name="ffn",
    )(x1, norm_ffn_w.reshape(1, D_MODEL), w1, w2, norm_final_w.reshape(1, D_MODEL))


def _sample_inproj_kernel(x_ref, nw_ref, wut_ref, wz_ref, wxbc_ref, wdt_ref, cw_ref, cb_ref,
                          cbuf_ref, dtb_ref, alog_ref,
                          u_ref, z_ref, xs_ref, b_ref, c_ref, dt_ref, dec_ref, nconv_ref):
    hn = _rms(x_ref[...], nw_ref[...]).astype(BF16)
    u_ref[...] = _dot_nt(hn, wut_ref[...])
    z_ref[...] = _dot(hn, wz_ref[...])
    xbc = _dot(hn, wxbc_ref[...])
    dt = jax.nn.softplus(_dot(hn, wdt_ref[...]) + dtb_ref[...])
    dt_ref[...] = dt
    dec_ref[...] = jnp.exp(dt * (-jnp.exp(alog_ref[...])))
    conv = cb_ref[...] + xbc * cw_ref[SSD_CONV - 1:SSD_CONV, :]
    for k in range(SSD_CONV - 1):
        conv = conv + cbuf_ref[k] * cw_ref[k:k + 1, :]
    nconv_ref[0] = cbuf_ref[1]
    nconv_ref[1] = cbuf_ref[2]
    nconv_ref[2] = xbc
    conv = _silu(conv)
    xs_ref[...] = conv[:, :D_SSD]
    b_ref[...] = conv[:, D_SSD:D_SSD + 2 * SSD_STATE]
    c_ref[...] = conv[:, D_SSD + 2 * SSD_STATE:]


def _sample_s5_kernel(u_ref, h0_ref, aq_ref, bbt_ref, cst_ref, d_ref, hn_ref, y_ref):
    u = u_ref[...]
    h0 = h0_ref[...]
    h0s = pltpu.roll(h0, S5_STATE, axis=1)
    hn = aq_ref[3:4, :] * h0 + aq_ref[4:5, :] * h0s + _dot(u, bbt_ref[...], HI)
    hn_ref[...] = hn
    y_ref[...] = _dot(hn, cst_ref[...], HI) + d_ref[...] * u


def _sample_ssd_kernel(dt_ref, dec_ref, h0_ref, xs_ref, b_ref, c_ref, drep_ref,
                       hn_ref, y_ref, yt_scr):
    blk = pl.program_id(0)
    xs8 = xs_ref[...]
    xt = jnp.transpose(jnp.concatenate([xs8, jnp.zeros((LANES - SB, D_SSD), F32)], axis=0))
    yt_scr[...] = jnp.zeros(yt_scr.shape, F32)
    for jj in range(SB):
        seq = blk * SB + jj
        for h in range(SSD_HEADS):
            g = h // (SSD_HEADS // 2)
            xcol = xt[h * SSD_HEAD_DIM:(h + 1) * SSD_HEAD_DIM, jj:jj + 1] * dt_ref[seq, h]
            brow = b_ref[jj:jj + 1, g * SSD_STATE:(g + 1) * SSD_STATE]
            crow = c_ref[jj:jj + 1, g * SSD_STATE:(g + 1) * SSD_STATE]
            hn = dec_ref[seq, h] * h0_ref[jj, h] + xcol * brow
            hn_ref[jj, h] = hn
            yt_scr[h * SSD_HEAD_DIM:(h + 1) * SSD_HEAD_DIM, jj:jj + 1] = jnp.sum(
                hn * crow, axis=1, keepdims=True)
    y_ref[...] = jnp.transpose(yt_scr[...])[0:SB, :] + drep_ref[...] * xs8


def _sample_mix_kernel(x_ref, y5_ref, ys_ref, z_ref, wglut_ref, bglu_ref, nw5_ref, snw_ref,
                       wout5_ref, wouts_ref, x1_ref):
    g = jax.nn.gelu(y5_ref[...])
    out = g * jax.nn.sigmoid(_dot_nt(g.astype(BF16), wglut_ref[...]) + bglu_ref[...])
    y5 = _rms(out, nw5_ref[...])
    y = ys_ref[...] * _silu(z_ref[...])
    half = D_SSD // 2
    yn = jnp.concatenate(
        [_rms(y[:, i * half:(i + 1) * half], snw_ref[:, i * half:(i + 1) * half]) for i in range(2)],
        axis=1)
    x1_ref[...] = (x_ref[...] + _dot(y5.astype(BF16), wout5_ref[...])
                   + _dot(yn.astype(BF16), wouts_ref[...]))


def _sample_layer(x_sample, st_re, st_im, st_ssd, st_conv, norm_mix_w, wut, wz, wxbc, wdt,
                  conv_w, conv_b, dtb, alog, drep, snw, prep, s5_d, wglut, b_glu, s5_norm_w,
                  wout5, wouts):
    nb = DEC_BATCH
    aq, bbt, cst = prep[3:6]
    xs2 = x_sample.reshape(nb, D_MODEL)
    sds = lambda *s: jax.ShapeDtypeStruct(s, F32)
    u, z, xs, bm, cm, dt, dec, nconv = pl.pallas_call(
        _sample_inproj_kernel,
        out_shape=(sds(nb, D_S5), sds(nb, D_SSD), sds(nb, D_SSD), sds(nb, 2 * SSD_STATE),
                   sds(nb, 2 * SSD_STATE), sds(nb, LANES), sds(nb, LANES),
                   sds(SSD_CONV - 1, nb, D_CONV)),
        compiler_params=_cp(), name="sample_inproj",
    )(xs2, norm_mix_w.reshape(1, D_MODEL), wut, wz, wxbc, wdt, conv_w, conv_b,
      jnp.transpose(st_conv, (1, 0, 2)), dtb, alog)

    per_g = lambda shape: pl.BlockSpec((None,) + shape, lambda g: (g, 0, 0))
    h0p = jnp.transpose(jnp.concatenate([st_re, st_im], axis=-1), (1, 0, 2))
    ug = jnp.transpose(u.reshape(nb, S5_GROUPS, S5_CH), (1, 0, 2))
    hn5, y5g = pl.pallas_call(
        _sample_s5_kernel, grid=(S5_GROUPS,),
        in_specs=[per_g((nb, S5_CH)), per_g((nb, LANES)), per_g((8, LANES)),
                  per_g((S5_CH, LANES)), per_g((LANES, S5_CH)), per_g((1, S5_CH))],
        out_specs=[per_g((nb, LANES)), per_g((nb, S5_CH))],
        out_shape=(sds(S5_GROUPS, nb, LANES), sds(S5_GROUPS, nb, S5_CH)),
        compiler_params=_cp("arbitrary"), name="sample_s5",
    )(ug, h0p, aq, bbt, cst, s5_d.reshape(S5_GROUPS, 1, S5_CH))
    y5 = jnp.transpose(y5g, (1, 0, 2)).reshape(nb, D_S5)

    smem = pl.BlockSpec(memory_space=pltpu.SMEM)
    blk2 = lambda n: pl.BlockSpec((SB, n), lambda i: (i, 0))
    st_spec = pl.BlockSpec((SB, SSD_HEADS, SSD_HEAD_DIM, SSD_STATE), lambda i: (i, 0, 0, 0))
    hn_ssd, ys = pl.pallas_call(
        _sample_ssd_kernel, grid=(nb // SB,),
        in_specs=[smem, smem, st_spec, blk2(D_SSD), blk2(2 * SSD_STATE), blk2(2 * SSD_STATE),
                  pl.BlockSpec((1, D_SSD), lambda i: (0, 0))],
        out_specs=[st_spec, blk2(D_SSD)],
        out_shape=(sds(nb, SSD_HEADS, SSD_HEAD_DIM, SSD_STATE), sds(nb, D_SSD)),
        scratch_shapes=[pltpu.VMEM((D_SSD, LANES), F32)],
        compiler_params=_cp("arbitrary"), name="sample_ssd",
    )(dt[:, :SSD_HEADS], dec[:, :SSD_HEADS], st_ssd, xs, bm, cm, drep)

    x1 = pl.pallas_call(
        _sample_mix_kernel, out_shape=sds(nb, D_MODEL),
        compiler_params=_cp(), name="sample_mix",
    )(xs2, y5, ys, z, wglut, b_glu.reshape(1, D_S5), s5_norm_w.reshape(1, D_S5), snw, wout5, wouts)
    return x1, hn5, hn_ssd, nconv


def kernel(x_prompt, x_sample, state_s5_re, state_s5_im, state_ssd, state_conv, norm_mix_w, w_in, s5_lam_re, s5_lam_im, s5_log_dt, s5_b_re, s5_b_im, s5_c_re, s5_c_im, s5_d, s5_w_glu, s5_b_glu, s5_norm_w, ssd_conv_w, ssd_conv_b, ssd_a_log, ssd_dt_bias, ssd_d, ssd_norm_w, w_out, norm_ffn_w, w_ff1, w_ff2, norm_final_w):
    P = S5_STATE
    w_in0 = w_in[0]
    wut = jnp.transpose(w_in0[:, :D_S5]).astype(BF16)
    wz = w_in0[:, D_S5:D_S5 + D_SSD].astype(BF16)
    wxbc = w_in0[:, D_S5 + D_SSD:D_S5 + D_SSD + D_CONV].astype(BF16)
    wdt = jnp.pad(w_in0[:, D_S5 + D_SSD + D_CONV:], ((0, 0), (0, LANES - SSD_HEADS))).astype(BF16)
    wglut = jnp.transpose(s5_w_glu[0]).astype(BF16)
    wout5 = w_out[0, :D_S5].astype(BF16)
    wouts = w_out[0, D_S5:].astype(BF16)
    w1 = w_ff1[0].astype(BF16)
    w2 = w_ff2[0].astype(BF16)
    pad_h = lambda v: jnp.pad(v.reshape(1, SSD_HEADS), ((0, 0), (0, LANES - SSD_HEADS)))
    dtb, alog = pad_h(ssd_dt_bias[0]), pad_h(ssd_a_log[0])
    drep = jnp.repeat(ssd_d[0], SSD_HEAD_DIM).reshape(1, D_SSD)
    snw = ssd_norm_w[0].reshape(1, D_SSD)
    conv_w, conv_b = ssd_conv_w[0], ssd_conv_b[0].reshape(1, D_CONV)

    prep = _s5_prep(s5_lam_re[0], s5_lam_im[0], s5_log_dt[0], s5_b_re[0], s5_b_im[0],
                    s5_c_re[0], s5_c_im[0], s5_d[0])

    o5, hfin5 = _s5_prompt(x_prompt, norm_mix_w[0], wut, prep, wglut, s5_b_glu[0], s5_norm_w[0], wout5)
    x1p, hT, ctail = _ssd_prompt(x_prompt, o5, norm_mix_w[0], wz, wxbc, wdt, conv_w, conv_b,
                                 dtb, alog, drep, snw, wouts)
    y_prompt = _ffn(x1p.reshape(BATCH * SEQ, D_MODEL), norm_ffn_w[0], w1, w2, norm_final_w,
                    TM_FF).reshape(BATCH, SEQ, D_MODEL)
    hfin5 = jnp.transpose(hfin5, (1, 0, 2))
    np_re, np_im = hfin5[None, :, :, :P], hfin5[None, :, :, P:]
    np_ssd = jnp.transpose(
        hT.reshape(BATCH, SSD_HEADS // 2, SSD_STATE, 2, SSD_HEAD_DIM), (0, 1, 3, 4, 2)
    ).reshape(1, BATCH, SSD_HEADS, SSD_HEAD_DIM, SSD_STATE)
    np_conv = ctail[None, :, SUBLANES - (SSD_CONV - 1):, :]

    x1s, hn5, hn_ssd, nconv = _sample_layer(
        x_sample, state_s5_re[0], state_s5_im[0], state_ssd[0], state_conv[0], norm_mix_w[0],
        wut, wz, wxbc, wdt, conv_w, conv_b, dtb, alog, drep, snw, prep, s5_d[0], wglut,
        s5_b_glu[0], s5_norm_w[0], wout5, wouts)
    y_sample = _ffn(x1s, norm_ffn_w[0], w1, w2, norm_final_w, DEC_BATCH).reshape(DEC_BATCH, 1, D_MODEL)
    hn5 = jnp.transpose(hn5, (1, 0, 2))
    ns_re, ns_im = hn5[None, :, :, :P], hn5[None, :, :, P:]
    ns_ssd = hn_ssd[None]
    ns_conv = jnp.transpose(nconv, (1, 0, 2))[None]

    return (y_prompt, y_sample, np_re, np_im, np_ssd, np_conv, ns_re, ns_im, ns_ssd, ns_conv)
```

```python
import functools

import jax
import jax.numpy as jnp
from jax import lax
from jax.experimental import pallas as pl
from jax.experimental.pallas import tpu as pltpu

F32 = jnp.float32
BF16 = jnp.bfloat16
HI = lax.Precision.HIGHEST
EPS = 1e-5

D_MODEL = 1024
BATCH = 8
SEQ = 2048
DEC_BATCH = 128
D_S5 = 1024
S5_CH = 16
S5_GROUPS = 64
S5_STATE = 64
D_SSD = 1024
SSD_HEADS = 16
SSD_HEAD_DIM = 64
SSD_STATE = 128
SSD_CONV = 4
D_CONV = 1536
D_FF = 4096

LANES = 128
SUBLANES = 8
Q5 = 32
NC5 = SEQ // Q5
M5 = NC5 * BATCH
R5 = Q5 * S5_CH
QS = 128
TL = 512
TM_FF = 512
SB = 16
GS = 8
VMEM_LIMIT = 56 * 1024 * 1024


def _cp(*sem):
    return pltpu.CompilerParams(dimension_semantics=sem, vmem_limit_bytes=VMEM_LIMIT)


def _rms(x, w):
    return x * lax.rsqrt(jnp.mean(x * x, axis=-1, keepdims=True) + EPS) * w


def _dot(a, b, precision=None):
    return jnp.dot(a, b, preferred_element_type=F32, precision=precision)


def _dot_nt(a, b, precision=None):
    return lax.dot_general(a, b, (((1,), (1,)), ((), ())), preferred_element_type=F32,
                           precision=precision)


def _dot_tn(a, b, precision=None):
    return lax.dot_general(a, b, (((0,), (0,)), ((), ())), preferred_element_type=F32,
                           precision=precision)


def _iota(shape, dim):
    return lax.broadcasted_iota(jnp.int32, shape, dim)


def _silu(x):
    return x * jax.nn.sigmoid(x)


def _s5_prep_kernel(lrc_ref, lic_ref, lr2_ref, li2_ref, ldt_ref, bre_ref, bim_ref, ccat_ref,
                    c2re_ref, c2im_ref, dcol_ref, bt2re_ref, bt2im_ref, ctre_ref, ctim_ref,
                    ws_ref, t_ref, wo_ref, aq_ref, bbt_ref, cst_ref):
    dt = jnp.exp(ldt_ref[...])

    def zoh(lr, li):
        mag = jnp.exp(lr * dt)
        ab_re = mag * jnp.cos(li * dt)
        ab_im = mag * jnp.sin(li * dt)
        den = lr * lr + li * li
        nr, ni = ab_re - 1.0, ab_im
        return (nr * lr + ni * li) / den, (ni * lr - nr * li) / den

    def powers(lr, li, e):
        mag = jnp.exp(e * (lr * dt))
        ang = e * (li * dt)
        return mag * jnp.cos(ang), mag * jnp.sin(ang)

    lrc, lic = lrc_ref[...], lic_ref[...]
    f_re, f_im = zoh(lrc, lic)
    bre, bim = bre_ref[...], bim_ref[...]
    bb_re = f_re * bre - f_im * bim
    bb_im = f_re * bim + f_im * bre
    e_l = _iota((S5_STATE, LANES), 1).astype(F32)
    pw_re, pw_im = powers(lrc, lic, e_l)
    row = _iota((LANES, R5), 0)
    col = _iota((LANES, R5), 1)
    step = lax.shift_right_logical(col, 4)
    sel_pow = (row == (Q5 - 1) - step).astype(F32)
    sel_ch = (row == (col & (S5_CH - 1))).astype(F32)
    ap_re = _dot(pw_re, sel_pow, HI)
    ap_im = _dot(pw_im, sel_pow, HI)
    bt_re = _dot(bb_re, sel_ch, HI)
    bt_im = _dot(bb_im, sel_ch, HI)
    ws_re = ap_re * bt_re - ap_im * bt_im
    ws_im = ap_re * bt_im + ap_im * bt_re
    ws_ref[...] = jnp.concatenate([ws_re, ws_im], axis=0).astype(BF16)

    vmat = _dot(ccat_ref[...], jnp.concatenate([ws_re, -ws_im], axis=0), HI)
    vrow = _iota((S5_CH, R5), 0)
    vcol = _iota((S5_CH, R5), 1)
    vmat = vmat + jnp.where(vcol == (Q5 - 1) * S5_CH + vrow, dcol_ref[...], 0.0)
    for l in range(Q5):
        width = (l + 1) * S5_CH
        r = vmat if width == R5 else pltpu.roll(vmat, width, axis=1)
        t_ref[l * S5_CH:(l + 1) * S5_CH, :] = jnp.where(vcol < width, r, 0.0).astype(BF16)

    lr2, li2 = lr2_ref[...], li2_ref[...]
    e_s = _iota((LANES, LANES), 0).astype(F32)
    p2_re, p2_im = powers(lr2, li2, e_s)
    rrow = _iota((R5, LANES), 0)
    rcol = _iota((R5, LANES), 1)
    sel_l = (rcol == lax.shift_right_logical(rrow, 4) + 1).astype(F32)
    a2_re = _dot(sel_l, p2_re, HI)
    a2_im = _dot(sel_l, p2_im, HI)
    c2re = jnp.broadcast_to(c2re_ref[...][None], (Q5, S5_CH, LANES)).reshape(R5, LANES)
    c2im = jnp.broadcast_to(c2im_ref[...][None], (Q5, S5_CH, LANES)).reshape(R5, LANES)
    lo = rcol < S5_STATE
    wo_ref[...] = jnp.where(lo, c2re * a2_re - c2im * a2_im,
                            -(c2re * a2_im + c2im * a2_re)).astype(BF16)

    lane1 = _iota((1, LANES), 1) < S5_STATE
    aq_re, aq_im = p2_re[Q5:Q5 + 1, :], p2_im[Q5:Q5 + 1, :]
    a1_re, a1_im = p2_re[1:2, :], p2_im[1:2, :]
    aq_ref[...] = jnp.concatenate(
        [aq_re, jnp.where(lane1, -aq_im, aq_im), jnp.where(lane1, aq_im, -aq_im),
         a1_re, jnp.where(lane1, -a1_im, a1_im), jnp.zeros((3, LANES), F32)], axis=0)

    f2_re, f2_im = zoh(lr2, li2)
    bt2re, bt2im = bt2re_ref[...], bt2im_ref[...]
    bbt_ref[...] = jnp.where(_iota((S5_CH, LANES), 1) < S5_STATE,
                             f2_re * bt2re - f2_im * bt2im, f2_re * bt2im + f2_im * bt2re)
    cst_ref[...] = jnp.concatenate([ctre_ref[...], -ctim_ref[...]], axis=0)


def _s5_prep(lam_re, lam_im, log_dt, b_re, b_im, c_re, c_im, d):
    G, P, C = S5_GROUPS, S5_STATE, S5_CH
    pad_b = lambda b: jnp.pad(b, ((0, 0), (0, 0), (0, LANES - C)))
    dup = lambda a: jnp.concatenate([a, a], axis=-1)
    args = (
        lam_re.reshape(G, P, 1), lam_im.reshape(G, P, 1),
        dup(lam_re).reshape(G, 1, 2 * P), dup(lam_im).reshape(G, 1, 2 * P),
        log_dt.reshape(G, 1, 1), pad_b(b_re), pad_b(b_im),
        jnp.concatenate([c_re, c_im], axis=-1), dup(c_re), dup(c_im),
        d.reshape(G, C, 1),
        dup(jnp.swapaxes(b_re, 1, 2)), dup(jnp.swapaxes(b_im, 1, 2)),
        jnp.swapaxes(c_re, 1, 2), jnp.swapaxes(c_im, 1, 2),
    )
    spec = lambda a: pl.BlockSpec((None,) + a.shape[1:], lambda g: (g, 0, 0))
    out_shapes = (
        jax.ShapeDtypeStruct((G, 2 * P, R5), BF16),
        jax.ShapeDtypeStruct((G, R5, R5), BF16),
        jax.ShapeDtypeStruct((G, R5, 2 * P), BF16),
        jax.ShapeDtypeStruct((G, 8, 2 * P), F32),
        jax.ShapeDtypeStruct((G, C, 2 * P), F32),
        jax.ShapeDtypeStruct((G, 2 * P, C), F32),
    )
    return pl.pallas_call(
        _s5_prep_kernel, grid=(G,),
        in_specs=[spec(a) for a in args],
        out_specs=[pl.BlockSpec((None,) + s.shape[1:], lambda g: (g, 0, 0)) for s in out_shapes],
        out_shape=out_shapes, compiler_params=_cp("arbitrary"), name="s5_prep",
    )(*args)


def _s5_inproj_kernel(x_ref, nw_ref, wut_ref, ut_ref):
    x = x_ref[...].reshape(M5, D_MODEL)
    hn = _rms(x, nw_ref[...]).astype(BF16)
    ut = _dot_nt(wut_ref[...], hn)
    ut_ref[...] = ut.astype(BF16).reshape(S5_GROUPS, S5_CH, M5)


def _s5_core_kernel(ut_ref, t_ref, ws_ref, wo_ref, aq_ref, g_ref, hfin_ref, hs_scr):
    ut = ut_ref[...]
    sp = jnp.transpose(_dot(ws_ref[...], ut))
    sq = pltpu.roll(sp, S5_STATE, axis=1)
    a1 = jnp.broadcast_to(aq_ref[0:1, :], (BATCH, LANES))
    a2 = jnp.broadcast_to(aq_ref[1:2, :], (BATCH, LANES))
    a3 = jnp.broadcast_to(aq_ref[2:3, :], (BATCH, LANES))
    hp = jnp.zeros((BATCH, LANES), F32)
    hq = jnp.zeros((BATCH, LANES), F32)
    for k in range(NC5):
        rows = slice(k * BATCH, (k + 1) * BATCH)
        hs_scr[rows, :] = hp
        hp, hq = a1 * hp + a2 * hq + sp[rows, :], a1 * hq + a3 * hp + sq[rows, :]
    hfin_ref[...] = hp
    y = _dot(t_ref[...], ut) + _dot_nt(wo_ref[...], hs_scr[...].astype(BF16))
    g_ref[...] = jax.nn.gelu(y)


def _s5_out_kernel(g_ref, wglut_ref, bglu_ref, nw_ref, wout_ref, o_ref):
    g = g_ref[...].reshape(D_S5, M5)
    z = _dot(wglut_ref[...], g.astype(BF16)) + bglu_ref[...]
    out = g * jax.nn.sigmoid(z)
    ms = jnp.mean(out * out, axis=0, keepdims=True)
    y5 = out * lax.rsqrt(ms + EPS) * nw_ref[...]
    o = _dot_tn(y5.astype(BF16), wout_ref[...])
    o_ref[...] = o.reshape(NC5, BATCH, D_MODEL)


def _s5_prompt(x_prompt, norm_mix_w, wut, prep, wglut, b_glu, s5_norm_w, wout5):
    ws, tmat, wo, aq = prep[:4]
    x_t = jnp.transpose(x_prompt, (1, 0, 2)).reshape(NC5, Q5, BATCH, D_MODEL)
    const2 = lambda s: (0, 0)
    ut = pl.pallas_call(
        _s5_inproj_kernel, grid=(Q5,),
        in_specs=[pl.BlockSpec((NC5, None, BATCH, D_MODEL), lambda s: (0, s, 0, 0)),
                  pl.BlockSpec((1, D_MODEL), const2),
                  pl.BlockSpec((D_S5, D_MODEL), const2)],
        out_specs=pl.BlockSpec((S5_GROUPS, S5_CH, M5), lambda s: (0, s, 0)),
        out_shape=jax.ShapeDtypeStruct((S5_GROUPS, R5, M5), BF16),
        compiler_params=_cp("arbitrary"), name="s5_inproj",
    )(x_t, norm_mix_w.reshape(1, D_MODEL), wut)

    per_g = lambda shape: pl.BlockSpec((None,) + shape, lambda g: (g, 0, 0))
    gact, hfin = pl.pallas_call(
        _s5_core_kernel, grid=(S5_GROUPS,),
        in_specs=[per_g((R5, M5)), per_g((R5, R5)), per_g((LANES, R5)), per_g((R5, LANES)),
                  per_g((8, LANES))],
        out_specs=[per_g((R5, M5)), per_g((BATCH, LANES))],
        out_shape=(jax.ShapeDtypeStruct((S5_GROUPS, R5, M5), F32),
                   jax.ShapeDtypeStruct((S5_GROUPS, BATCH, LANES), F32)),
        scratch_shapes=[pltpu.VMEM((M5, LANES), F32)],
        compiler_params=_cp("arbitrary"), name="s5_core",
    )(ut, tmat, ws, wo, aq)

    o5 = pl.pallas_call(
        _s5_out_kernel, grid=(Q5,),
        in_specs=[pl.BlockSpec((S5_GROUPS, S5_CH, M5), lambda s: (0, s, 0)),
                  pl.BlockSpec((D_S5, D_S5), const2),
                  pl.BlockSpec((D_S5, 1), const2),
                  pl.BlockSpec((D_S5, 1), const2),
                  pl.BlockSpec((D_S5, D_MODEL), const2)],
        out_specs=pl.BlockSpec((NC5, None, BATCH, D_MODEL), lambda s: (0, s, 0, 0)),
        out_shape=jax.ShapeDtypeStruct((NC5, Q5, BATCH, D_MODEL), F32),
        compiler_params=_cp("arbitrary"), name="s5_out",
    )(gact, wglut, b_glu.reshape(D_S5, 1), s5_norm_w.reshape(D_S5, 1), wout5)
    return o5.reshape(SEQ, BATCH * D_MODEL), hfin


def _ssd_prompt_kernel(x_ref, o5_ref, nw_ref, wz_ref, wxbc_ref, wdt_ref, cw_ref, cb_ref,
                       dtb_ref, alog_ref, drep_ref, snw_ref, wout_ref,
                       x1_ref, hout_ref, cout_ref,
                       xp_scr, xs_scr, b_scr, c_scr, z_scr, a_scr, dt_scr, y_scr, h_scr):
    j = pl.program_id(1)

    @pl.when(j == 0)
    def _():
        xp_scr[0:SUBLANES, :] = jnp.zeros((SUBLANES, D_CONV), F32)
        h_scr[...] = jnp.zeros(h_scr.shape, F32)

    x = x_ref[...]
    hn = _rms(x, nw_ref[...]).astype(BF16)
    z_scr[...] = _dot(hn, wz_ref[...])
    xp_scr[SUBLANES:, :] = _dot(hn, wxbc_ref[...])
    dt = jax.nn.softplus(_dot(hn, wdt_ref[...]) + dtb_ref[...])
    dt_scr[...] = dt
    a_scr[...] = dt * (-jnp.exp(alog_ref[...]))

    conv = cb_ref[...] + sum(
        xp_scr[SUBLANES - (SSD_CONV - 1) + k:SUBLANES - (SSD_CONV - 1) + k + TL, :] * cw_ref[k:k + 1, :]
        for k in range(SSD_CONV))
    tail = xp_scr[TL:TL + SUBLANES, :]
    xp_scr[0:SUBLANES, :] = tail
    cout_ref[...] = tail
    conv = _silu(conv)
    xs_scr[...] = conv[:, :D_SSD]
    b_scr[...] = conv[:, D_SSD:D_SSD + 2 * SSD_STATE]
    c_scr[...] = conv[:, D_SSD + 2 * SSD_STATE:]

    li = _iota((QS, QS), 0)
    si = _iota((QS, QS), 1)
    causal = li >= si
    tril = causal.astype(F32)
    lo = si < SSD_HEAD_DIM

    def chunk(ci, carry):
        r0 = pl.multiple_of(ci * QS, QS)
        rows = pl.ds(r0, QS)
        acum = _dot(tril, a_scr[rows, :], HI)
        acum_t = jnp.transpose(acum)
        dt_t = jnp.transpose(dt_scr[rows, :])
        alast = acum[QS - 1:QS, :]
        alast_t = acum_t[:, QS - 1:QS]
        for g in range(2):
            bg = b_scr[rows, g * SSD_STATE:(g + 1) * SSD_STATE]
            cg_b = c_scr[rows, g * SSD_STATE:(g + 1) * SSD_STATE].astype(BF16)
            bt = jnp.transpose(bg)
            cb = _dot(cg_b, bt.astype(BF16))
            h_grp = jnp.concatenate([h_scr[4 * g + i] for i in range(4)], axis=1)
            y_off = _dot(cg_b, h_grp.astype(BF16))
            for hp in range(4):
                pr = 4 * g + hp
                cols = slice(pr * LANES, (pr + 1) * LANES)
                xs_pair = xs_scr[rows, cols]
                xs_b = xs_pair.astype(BF16)
                zero = jnp.zeros_like(xs_b)
                xs_half = (jnp.where(lo, xs_b, zero), jnp.where(lo, zero, xs_b))
                acols, y, st = [], None, None
                for par, h in enumerate((2 * pr, 2 * pr + 1)):
                    acol = jnp.broadcast_to(acum[:, h:h + 1], (QS, QS))
                    arow = acum_t[h:h + 1, :]
                    dtrow = dt_t[h:h + 1, :]
                    lmat = jnp.exp(jnp.where(causal, acol - arow, -1e30))
                    m = (cb * lmat * dtrow).astype(BF16)
                    bts = (bt * (jnp.exp(alast_t[h:h + 1, :] - arow) * dtrow)).astype(BF16)
                    yd = _dot(m, xs_half[par])
                    sd = _dot(bts, xs_half[par])
                    y = yd if y is None else y + yd
                    st = sd if st is None else st + sd
                    acols.append(acol)
                h0, h1 = 2 * pr, 2 * pr + 1
                decay = jnp.exp(jnp.where(lo[0:1, :], alast[:, h0:h0 + 1], alast[:, h1:h1 + 1]))
                h_scr[pr] = decay * h_scr[pr] + st
                y = (y + jnp.exp(jnp.where(lo, acols[0], acols[1])) * y_off[:, hp * LANES:(hp + 1) * LANES]
                     + drep_ref[:, cols] * xs_pair)
                y_scr[rows, cols] = y * _silu(z_scr[rows, cols])
        half = D_SSD // 2
        for g in range(2):
            yg = y_scr[rows, g * half:(g + 1) * half]
            yg = yg * lax.rsqrt(jnp.mean(yg * yg, axis=-1, keepdims=True) + EPS)
            y_scr[rows, g * half:(g + 1) * half] = yg * snw_ref[:, g * half:(g + 1) * half]
        return carry

    lax.fori_loop(0, TL // QS, chunk, 0)
    x1_ref[...] = x + o5_ref[...] + _dot(y_scr[...].astype(BF16), wout_ref[...])

    @pl.when(j == pl.num_programs(1) - 1)
    def _():
        hout_ref[...] = h_scr[...]


def _ssd_prompt(x_prompt, o5, norm_mix_w, wz, wxbc, wdt, conv_w, conv_b, dtb, alog, drep, snw, wouts):
    nj = SEQ // TL
    c2 = lambda b, j: (0, 0)
    row = lambda n: pl.BlockSpec((1, n), c2)
    return pl.pallas_call(
        _ssd_prompt_kernel, grid=(BATCH, nj),
        in_specs=[pl.BlockSpec((None, TL, D_MODEL), lambda b, j: (b, j, 0)),
                  pl.BlockSpec((TL, D_MODEL), lambda b, j: (j, b)),
                  row(D_MODEL),
                  pl.BlockSpec((D_MODEL, D_SSD), c2), pl.BlockSpec((D_MODEL, D_CONV), c2),
                  pl.BlockSpec((D_MODEL, LANES), c2),
                  pl.BlockSpec((SSD_CONV, D_CONV), c2), row(D_CONV),
                  row(LANES), row(LANES), row(D_SSD), row(D_SSD),
                  pl.BlockSpec((D_SSD, D_MODEL), c2)],
        out_specs=[pl.BlockSpec((None, TL, D_MODEL), lambda b, j: (b, j, 0)),
                   pl.BlockSpec((None, SSD_HEADS // 2, SSD_STATE, LANES), lambda b, j: (b, 0, 0, 0)),
                   pl.BlockSpec((None, SUBLANES, D_CONV), lambda b, j: (b, 0, 0))],
        out_shape=(jax.ShapeDtypeStruct((BATCH, SEQ, D_MODEL), F32),
                   jax.ShapeDtypeStruct((BATCH, SSD_HEADS // 2, SSD_STATE, LANES), F32),
                   jax.ShapeDtypeStruct((BATCH, SUBLANES, D_CONV), F32)),
        scratch_shapes=[pltpu.VMEM((SUBLANES + TL, D_CONV), F32),
                        pltpu.VMEM((TL, D_SSD), F32),
                        pltpu.VMEM((TL, 2 * SSD_STATE), F32),
                        pltpu.VMEM((TL, 2 * SSD_STATE), F32),
                        pltpu.VMEM((TL, D_SSD), F32),
                        pltpu.VMEM((TL, LANES), F32),
                        pltpu.VMEM((TL, LANES), F32),
                        pltpu.VMEM((TL, D_SSD), F32),
                        pltpu.VMEM((SSD_HEADS // 2, SSD_STATE, LANES), F32)],
        compiler_params=_cp("arbitrary", "arbitrary"), name="ssd_prompt",
    )(x_prompt, o5, norm_mix_w.reshape(1, D_MODEL), wz, wxbc, wdt, conv_w, conv_b, dtb, alog,
      drep, snw, wouts)


def _ffn_kernel(x_ref, nw_ref, w1_ref, w2_ref, fw_ref, o_ref):
    x = x_ref[...]
    hf = _rms(x, nw_ref[...]).astype(BF16)
    acc = x
    blk = D_FF // 4
    for c in range(4):
        h1 = jnp.square(jnp.maximum(_dot(hf, w1_ref[:, c * blk:(c + 1) * blk]), 0.0))
        acc = acc + _dot(h1.astype(BF16), w2_ref[c * blk:(c + 1) * blk, :])
    o_ref[...] = _rms(acc, fw_ref[...])


def _ffn(x1, norm_ffn_w, w1, w2, norm_final_w, tm):
    n = x1.shape[0]
    c2 = lambda i: (0, 0)
    return pl.pallas_call(
        _ffn_kernel, grid=(n // tm,),
        in_specs=[pl.BlockSpec((tm, D_MODEL), lambda i: (i, 0)),
                  pl.BlockSpec((1, D_MODEL), c2),
                  pl.BlockSpec((D_MODEL, D_FF), c2),
                  pl.BlockSpec((D_FF, D_MODEL), c2),
                  pl.BlockSpec((1, D_MODEL), c2)],
        out_specs=pl.BlockSpec((tm, D_MODEL), lambda i: (i, 0)),
        out_shape=jax.ShapeDtypeStruct((n, D_MODEL), F32),
        compiler_params=_cp("arbitrary"), name="ffn",
    )(x1, norm_ffn_w.reshape(1, D_MODEL), w1, w2, norm_final_w.reshape(1, D_MODEL))


def _sample_inproj_kernel(x_ref, nw_ref, wut_ref, wz_ref, wxbc_ref, wdt_ref, cw_ref, cb_ref,
                          cbuf_ref, dtb_ref, alog_ref,
                          u_ref, z_ref, xs_ref, xdt_ref, b_ref, c_ref, dec_ref, nconv_ref):
    hn = _rms(x_ref[...], nw_ref[...]).astype(BF16)
    u_ref[...] = _dot_nt(hn, wut_ref[...])
    z_ref[...] = _dot(hn, wz_ref[...])
    xbc = _dot(hn, wxbc_ref[...])
    dt = jax.nn.softplus(_dot(hn, wdt_ref[...]) + dtb_ref[...])
    dec_ref[...] = jnp.exp(dt * (-jnp.exp(alog_ref[...])))
    head_of_col = lax.shift_right_logical(_iota((LANES, D_SSD), 1), 6)
    dt_rep = _dot(dt, (_iota((LANES, D_SSD), 0) == head_of_col).astype(F32), HI)
    conv = cb_ref[...] + xbc * cw_ref[SSD_CONV - 1:SSD_CONV, :]
    for k in range(SSD_CONV - 1):
        conv = conv + cbuf_ref[k] * cw_ref[k:k + 1, :]
    nconv_ref[0] = cbuf_ref[1]
    nconv_ref[1] = cbuf_ref[2]
    nconv_ref[2] = xbc
    conv = _silu(conv)
    xs_ref[...] = conv[:, :D_SSD]
    xdt_ref[...] = conv[:, :D_SSD] * dt_rep
    b_ref[...] = conv[:, D_SSD:D_SSD + 2 * SSD_STATE]
    c_ref[...] = conv[:, D_SSD + 2 * SSD_STATE:]


def _sample_s5_kernel(u_ref, h0_ref, aq_ref, bbt_ref, cst_ref, d_ref, hn_ref, y_ref):
    for i in range(GS):
        u = u_ref[i]
        h0 = h0_ref[i]
        h0s = pltpu.roll(h0, S5_STATE, axis=1)
        hn = aq_ref[i, 3:4, :] * h0 + aq_ref[i, 4:5, :] * h0s + _dot(u, bbt_ref[i], HI)
        hn_ref[i] = hn
        y_ref[i] = _dot(hn, cst_ref[i], HI) + d_ref[i] * u


def _sample_ssd_kernel(dec_ref, h0_ref, xdt_ref, xs_ref, b_ref, c_ref, drep_ref, hn_ref, y_ref):
    blk = pl.program_id(0)
    hpg = SSD_HEADS // 2
    half = D_SSD // 2
    xt = jnp.transpose(jnp.concatenate(
        [xdt_ref[...], jnp.zeros((LANES - SB, D_SSD), F32)], axis=0))
    c_b = c_ref[...].astype(BF16)
    rowi = _iota((SB, half), 0)
    ys = [jnp.zeros((SB, half), F32) for _ in range(2)]
    for jj in range(SB):
        seq = blk * SB + jj
        for g in range(2):
            brow = b_ref[jj:jj + 1, g * SSD_STATE:(g + 1) * SSD_STATE]
            parts = []
            for h in range(g * hpg, (g + 1) * hpg):
                xcol = xt[h * SSD_HEAD_DIM:(h + 1) * SSD_HEAD_DIM, jj:jj + 1]
                hn = dec_ref[seq, h] * h0_ref[jj, h] + xcol * brow
                hn_ref[jj, h] = hn
                parts.append(hn.astype(BF16))
            y_all = _dot_nt(c_b[:, g * SSD_STATE:(g + 1) * SSD_STATE], jnp.concatenate(parts, axis=0))
            ys[g] = jnp.where(rowi == jj, y_all, ys[g])
    y_ref[...] = jnp.concatenate(ys, axis=1) + drep_ref[...] * xs_ref[...]


def _sample_mix_kernel(x_ref, y5_ref, ys_ref, z_ref, wglut_ref, bglu_ref, nw5_ref, snw_ref,
                       wout5_ref, wouts_ref, x1_ref):
    g = jax.nn.gelu(y5_ref[...])
    out = g * jax.nn.sigmoid(_dot_nt(g.astype(BF16), wglut_ref[...]) + bglu_ref[...])
    y5 = _rms(out, nw5_ref[...])
    y = ys_ref[...] * _silu(z_ref[...])
    half = D_SSD // 2
    yn = jnp.concatenate(
        [_rms(y[:, i * half:(i + 1) * half], snw_ref[:, i * half:(i + 1) * half]) for i in range(2)],
        axis=1)
    x1_ref[...] = (x_ref[...] + _dot(y5.astype(BF16), wout5_ref[...])
                   + _dot(yn.astype(BF16), wouts_ref[...]))


def _sample_layer(x_sample, st_re, st_im, st_ssd, st_conv, norm_mix_w, wut, wz, wxbc, wdt,
                  conv_w, conv_b, dtb, alog, drep, snw, prep, s5_d, wglut, b_glu, s5_norm_w,
                  wout5, wouts):
    nb = DEC_BATCH
    aq, bbt, cst = prep[3:6]
    xs2 = x_sample.reshape(nb, D_MODEL)
    sds = lambda *s: jax.ShapeDtypeStruct(s, F32)
    u, z, xs, xdt, bm, cm, dec, nconv = pl.pallas_call(
        _sample_inproj_kernel,
        out_shape=(sds(nb, D_S5), sds(nb, D_SSD), sds(nb, D_SSD), sds(nb, D_SSD),
                   sds(nb, 2 * SSD_STATE), sds(nb, 2 * SSD_STATE), sds(nb, LANES),
                   sds(SSD_CONV - 1, nb, D_CONV)),
        compiler_params=_cp(), name="sample_inproj",
    )(xs2, norm_mix_w.reshape(1, D_MODEL), wut, wz, wxbc, wdt, conv_w, conv_b,
      jnp.transpose(st_conv, (1, 0, 2)), dtb, alog)

    per_g = lambda shape: pl.BlockSpec((GS,) + shape, lambda g: (g, 0, 0))
    h0p = jnp.transpose(jnp.concatenate([st_re, st_im], axis=-1), (1, 0, 2))
    ug = jnp.transpose(u.reshape(nb, S5_GROUPS, S5_CH), (1, 0, 2))
    hn5, y5g = pl.pallas_call(
        _sample_s5_kernel, grid=(S5_GROUPS // GS,),
        in_specs=[per_g((nb, S5_CH)), per_g((nb, LANES)), per_g((8, LANES)),
                  per_g((S5_CH, LANES)), per_g((LANES, S5_CH)), per_g((1, S5_CH))],
        out_specs=[per_g((nb, LANES)), per_g((nb, S5_CH))],
        out_shape=(sds(S5_GROUPS, nb, LANES), sds(S5_GROUPS, nb, S5_CH)),
        compiler_params=_cp("arbitrary"), name="sample_s5",
    )(ug, h0p, aq, bbt, cst, s5_d.reshape(S5_GROUPS, 1, S5_CH))
    y5 = jnp.transpose(y5g, (1, 0, 2)).reshape(nb, D_S5)

    smem = pl.BlockSpec(memory_space=pltpu.SMEM)
    blk2 = lambda n: pl.BlockSpec((SB, n), lambda i: (i, 0))
    st_spec = pl.BlockSpec((SB, SSD_HEADS, SSD_HEAD_DIM, SSD_STATE), lambda i: (i, 0, 0, 0))
    hn_ssd, ys = pl.pallas_call(
        _sample_ssd_kernel, grid=(nb // SB,),
        in_specs=[smem, st_spec, blk2(D_SSD), blk2(D_SSD), blk2(2 * SSD_STATE),
                  blk2(2 * SSD_STATE), pl.BlockSpec((1, D_SSD), lambda i: (0, 0))],
        out_specs=[st_spec, blk2(D_SSD)],
        out_shape=(sds(nb, SSD_HEADS, SSD_HEAD_DIM, SSD_STATE), sds(nb, D_SSD)),
        compiler_params=_cp("arbitrary"), name="sample_ssd",
    )(dec[:, :SSD_HEADS], st_ssd, xdt, xs, bm, cm, drep)

    x1 = pl.pallas_call(
        _sample_mix_kernel, out_shape=sds(nb, D_MODEL),
        compiler_params=_cp(), name="sample_mix",
    )(xs2, y5, ys, z, wglut, b_glu.reshape(1, D_S5), s5_norm_w.reshape(1, D_S5), snw, wout5, wouts)
    return x1, hn5, hn_ssd, nconv


def kernel(x_prompt, x_sample, state_s5_re, state_s5_im, state_ssd, state_conv, norm_mix_w, w_in, s5_lam_re, s5_lam_im, s5_log_dt, s5_b_re, s5_b_im, s5_c_re, s5_c_im, s5_d, s5_w_glu, s5_b_glu, s5_norm_w, ssd_conv_w, ssd_conv_b, ssd_a_log, ssd_dt_bias, ssd_d, ssd_norm_w, w_out, norm_ffn_w, w_ff1, w_ff2, norm_final_w):
    P = S5_STATE
    w_in0 = w_in[0]
    wut = jnp.transpose(w_in0[:, :D_S5]).astype(BF16)
    wz = w_in0[:, D_S5:D_S5 + D_SSD].astype(BF16)
    wxbc = w_in0[:, D_S5 + D_SSD:D_S5 + D_SSD + D_CONV].astype(BF16)
    wdt = jnp.pad(w_in0[:, D_S5 + D_SSD + D_CONV:], ((0, 0), (0, LANES - SSD_HEADS))).astype(BF16)
    wglut = jnp.transpose(s5_w_glu[0]).astype(BF16)
    wout5 = w_out[0, :D_S5].astype(BF16)
    wouts = w_out[0, D_S5:].astype(BF16)
    w1 = w_ff1[0].astype(BF16)
    w2 = w_ff2[0].astype(BF16)
    pad_h = lambda v: jnp.pad(v.reshape(1, SSD_HEADS), ((0, 0), (0, LANES - SSD_HEADS)))
    dtb, alog = pad_h(ssd_dt_bias[0]), pad_h(ssd_a_log[0])
    drep = jnp.repeat(ssd_d[0], SSD_HEAD_DIM).reshape(1, D_SSD)
    snw = ssd_norm_w[0].reshape(1, D_SSD)
    conv_w, conv_b = ssd_conv_w[0], ssd_conv_b[0].reshape(1, D_CONV)

    prep = _s5_prep(s5_lam_re[0], s5_lam_im[0], s5_log_dt[0], s5_b_re[0], s5_b_im[0],
                    s5_c_re[0], s5_c_im[0], s5_d[0])

    o5, hfin5 = _s5_prompt(x_prompt, norm_mix_w[0], wut, prep, wglut, s5_b_glu[0], s5_norm_w[0], wout5)
    x1p, hT, ctail = _ssd_prompt(x_prompt, o5, norm_mix_w[0], wz, wxbc, wdt, conv_w, conv_b,
                                 dtb, alog, drep, snw, wouts)
    y_prompt = _ffn(x1p.reshape(BATCH * SEQ, D_MODEL), norm_ffn_w[0], w1, w2, norm_final_w,
                    TM_FF).reshape(BATCH, SEQ, D_MODEL)
    hfin5 = jnp.transpose(hfin5, (1, 0, 2))
    np_re, np_im = hfin5[None, :, :, :P], hfin5[None, :, :, P:]
    np_ssd = jnp.transpose(
        hT.reshape(BATCH, SSD_HEADS // 2, SSD_STATE, 2, SSD_HEAD_DIM), (0, 1, 3, 4, 2)
    ).reshape(1, BATCH, SSD_HEADS, SSD_HEAD_DIM, SSD_STATE)
    np_conv = ctail[None, :, SUBLANES - (SSD_CONV - 1):, :]

    x1s, hn5, hn_ssd, nconv = _sample_layer(
        x_sample, state_s5_re[0], state_s5_im[0], state_ssd[0], state_conv[0], norm_mix_w[0],
        wut, wz, wxbc, wdt, conv_w, conv_b, dtb, alog, drep, snw, prep, s5_d[0], wglut,
        s5_b_glu[0], s5_norm_w[0], wout5, wouts)
    y_sample = _ffn(x1s, norm_ffn_w[0], w1, w2, norm_final_w, DEC_BATCH).reshape(DEC_BATCH, 1, D_MODEL)
    hn5 = jnp.transpose(hn5, (1, 0, 2))
    ns_re, ns_im = hn5[None, :, :, :P], hn5[None, :, :, P:]
    ns_ssd = hn_ssd[None]
    ns_conv = jnp.transpose(nconv, (1, 0, 2))[None]

    return (y_prompt, y_sample, np_re, np_im, np_ssd, np_conv, ns_re, ns_im, ns_ssd, ns_conv)
```

```python
import functools

import jax
import jax.numpy as jnp
from jax import lax
from jax.experimental import pallas as pl
from jax.experimental.pallas import tpu as pltpu

F32 = jnp.float32
BF16 = jnp.bfloat16
HI = lax.Precision.HIGHEST
EPS = 1e-5

D_MODEL = 1024
BATCH = 8
SEQ = 2048
DEC_BATCH = 128
D_S5 = 1024
S5_CH = 16
S5_GROUPS = 64
S5_STATE = 64
D_SSD = 1024
SSD_HEADS = 16
SSD_HEAD_DIM = 64
SSD_STATE = 128
SSD_CONV = 4
D_CONV = 1536
D_FF = 4096

LANES = 128
SUBLANES = 8
Q5 = 32
NC5 = SEQ // Q5
M5 = NC5 * BATCH
R5 = Q5 * S5_CH
QS = 128
TL = 512
TM_FF = 512
SB = 16
GS = 8
GP = 4
GC = 4
VMEM_LIMIT = 56 * 1024 * 1024


def _cp(*sem):
    return pltpu.CompilerParams(dimension_semantics=sem, vmem_limit_bytes=VMEM_LIMIT)


def _rms(x, w):
    return x * lax.rsqrt(jnp.mean(x * x, axis=-1, keepdims=True) + EPS) * w


def _dot(a, b, precision=None):
    return jnp.dot(a, b, preferred_element_type=F32, precision=precision)


def _dot_nt(a, b, precision=None):
    return lax.dot_general(a, b, (((1,), (1,)), ((), ())), preferred_element_type=F32,
                           precision=precision)


def _dot_tn(a, b, precision=None):
    return lax.dot_general(a, b, (((0,), (0,)), ((), ())), preferred_element_type=F32,
                           precision=precision)


def _iota(shape, dim):
    return lax.broadcasted_iota(jnp.int32, shape, dim)


def _silu(x):
    return x * jax.nn.sigmoid(x)


def _dot_split(a, b):
    a_hi = a.astype(BF16)
    b_hi = b.astype(BF16)
    a_lo = (a - a_hi.astype(F32)).astype(BF16)
    b_lo = (b - b_hi.astype(F32)).astype(BF16)
    return _dot(a_hi, b_hi) + _dot(a_lo, b_hi) + _dot(a_hi, b_lo)


def _rows_x16(x):
    return jnp.broadcast_to(x.reshape(Q5, 1, LANES), (Q5, S5_CH, LANES)).reshape(R5, LANES)


def _tile_x32(x):
    return jnp.broadcast_to(x[None], (Q5, S5_CH, LANES)).reshape(R5, LANES)


def _s5_prep_kernel(lr2_ref, li2_ref, ldt_ref, ccat_ref, c2re_ref, c2im_ref, dcol_ref,
                    bt2re_ref, bt2im_ref, ctre_ref, ctim_ref,
                    ws_ref, t_ref, wo_ref, aq_ref, bbt_ref, cst_ref):
    for i in range(GP):
        _s5_prep_group(lr2_ref[i], li2_ref[i], ldt_ref[i], ccat_ref[i], c2re_ref[i], c2im_ref[i],
                       dcol_ref[i], bt2re_ref[i], bt2im_ref[i], ctre_ref[i], ctim_ref[i],
                       ws_ref.at[i], t_ref.at[i], wo_ref.at[i], aq_ref.at[i], bbt_ref.at[i],
                       cst_ref.at[i])


def _s5_prep_group(lr2, li2, ldt, ccat, c2re, c2im, dcol, bt2re, bt2im, ctre, ctim,
                   ws_ref, t_ref, wo_ref, aq_ref, bbt_ref, cst_ref):
    dt = jnp.exp(ldt)

    def zoh(lr, li):
        mag = jnp.exp(lr * dt)
        ab_re = mag * jnp.cos(li * dt)
        ab_im = mag * jnp.sin(li * dt)
        den = lr * lr + li * li
        nr, ni = ab_re - 1.0, ab_im
        return (nr * lr + ni * li) / den, (ni * lr - nr * li) / den

    def powers(lr, li, e):
        mag = jnp.exp(e * (lr * dt))
        ang = e * (li * dt)
        return mag * jnp.cos(ang), mag * jnp.sin(ang)

    lane1 = _iota((1, LANES), 1) < S5_STATE
    lo16 = _iota((S5_CH, LANES), 1) < S5_STATE
    f_re, f_im = zoh(lr2, li2)
    bb_a = f_re * bt2re - f_im * bt2im
    bb_b = f_re * bt2im + f_im * bt2re
    bbt = jnp.where(lo16, bb_a, bb_b)
    bbt_sw = jnp.where(lo16, bb_b, bb_a)
    bbt_ref[...] = bbt

    r = _iota((2 * 40, LANES), 0)
    expo = jnp.where(r < 40, r, jnp.maximum(71 - r, 0)).astype(F32)
    tab_re, tab_im = powers(lr2, li2, expo)
    aq_re, aq_im = tab_re[Q5:Q5 + 1, :], tab_im[Q5:Q5 + 1, :]
    a1_re, a1_im = tab_re[1:2, :], tab_im[1:2, :]
    aq_ref[...] = jnp.concatenate(
        [aq_re, jnp.where(lane1, -aq_im, aq_im), jnp.where(lane1, aq_im, -aq_im),
         a1_re, jnp.where(lane1, -a1_im, a1_im), jnp.zeros((3, LANES), F32)], axis=0)

    ain_re, ain_im = _rows_x16(tab_re[40:40 + Q5, :]), _rows_x16(tab_im[40:40 + Q5, :])
    sign = jnp.where(lane1, -1.0, 1.0)
    ws_t = ain_re * _tile_x32(bbt) + (sign * ain_im) * _tile_x32(bbt_sw)
    ws = jnp.transpose(ws_t)
    ws_ref[...] = ws.astype(BF16)

    upper = _iota((LANES, R5), 0) < S5_STATE
    vmat = _dot_split(ccat, jnp.where(upper, ws, -ws))
    vrow = _iota((S5_CH, R5), 0)
    vcol = _iota((S5_CH, R5), 1)
    vmat = vmat + jnp.where(vcol == (Q5 - 1) * S5_CH + vrow, dcol, 0.0)
    for l in range(Q5):
        width = (l + 1) * S5_CH
        rolled = vmat if width == R5 else pltpu.roll(vmat, width, axis=1)
        t_ref[l * S5_CH:(l + 1) * S5_CH, :] = jnp.where(vcol < width, rolled, 0.0).astype(BF16)

    aout_re, aout_im = _rows_x16(tab_re[1:Q5 + 1, :]), _rows_x16(tab_im[1:Q5 + 1, :])
    c2re_t, c2im_t = _tile_x32(c2re), _tile_x32(c2im)
    wo_ref[...] = jnp.where(_iota((R5, LANES), 1) < S5_STATE,
                            c2re_t * aout_re - c2im_t * aout_im,
                            -(c2re_t * aout_im + c2im_t * aout_re)).astype(BF16)
    cst_ref[...] = jnp.concatenate([ctre, -ctim], axis=0)


def _s5_prep(lam_re, lam_im, log_dt, b_re, b_im, c_re, c_im, d):
    G, P, C = S5_GROUPS, S5_STATE, S5_CH
    dup = lambda a: jnp.concatenate([a, a], axis=-1)
    args = (
        dup(lam_re).reshape(G, 1, 2 * P), dup(lam_im).reshape(G, 1, 2 * P),
        log_dt.reshape(G, 1, 1),
        jnp.concatenate([c_re, c_im], axis=-1), dup(c_re), dup(c_im),
        d.reshape(G, C, 1),
        dup(jnp.swapaxes(b_re, 1, 2)), dup(jnp.swapaxes(b_im, 1, 2)),
        jnp.swapaxes(c_re, 1, 2), jnp.swapaxes(c_im, 1, 2),
    )
    spec = lambda a: pl.BlockSpec((GP,) + a.shape[1:], lambda g: (g, 0, 0))
    out_shapes = (
        jax.ShapeDtypeStruct((G, 2 * P, R5), BF16),
        jax.ShapeDtypeStruct((G, R5, R5), BF16),
        jax.ShapeDtypeStruct((G, R5, 2 * P), BF16),
        jax.ShapeDtypeStruct((G, 8, 2 * P), F32),
        jax.ShapeDtypeStruct((G, C, 2 * P), F32),
        jax.ShapeDtypeStruct((G, 2 * P, C), F32),
    )
    return pl.pallas_call(
        _s5_prep_kernel, grid=(G // GP,),
        in_specs=[spec(a) for a in args],
        out_specs=[pl.BlockSpec((GP,) + s.shape[1:], lambda g: (g, 0, 0)) for s in out_shapes],
        out_shape=out_shapes, compiler_params=_cp("arbitrary"), name="s5_prep",
    )(*args)


def _s5_inproj_kernel(x_ref, nw_ref, wut_ref, ut_ref):
    x = x_ref[...].reshape(M5, D_MODEL)
    hn = _rms(x, nw_ref[...]).astype(BF16)
    ut = _dot_nt(wut_ref[...], hn)
    ut_ref[...] = ut.astype(BF16).reshape(S5_GROUPS, S5_CH, M5)


def _s5_core_kernel(ut_ref, t_ref, ws_ref, wo_ref, aq_ref, g_ref, hfin_ref, hs_scr):
    for i in range(GC):
        ut = ut_ref[i]
        sp = jnp.transpose(_dot(ws_ref[i], ut))
        sq = pltpu.roll(sp, S5_STATE, axis=1)
        a1 = jnp.broadcast_to(aq_ref[i, 0:1, :], (BATCH, LANES))
        a2 = jnp.broadcast_to(aq_ref[i, 1:2, :], (BATCH, LANES))
        a3 = jnp.broadcast_to(aq_ref[i, 2:3, :], (BATCH, LANES))
        hp = jnp.zeros((BATCH, LANES), F32)
        hq = jnp.zeros((BATCH, LANES), F32)
        for k in range(NC5):
            rows = slice(k * BATCH, (k + 1) * BATCH)
            hs_scr[i, rows, :] = hp
            hp, hq = a1 * hp + a2 * hq + sp[rows, :], a1 * hq + a3 * hp + sq[rows, :]
        hfin_ref[i] = hp
        g_ref[i] = _dot(t_ref[i], ut)
    for i in range(GC):
        g_ref[i] = g_ref[i] + _dot_nt(wo_ref[i], hs_scr[i].astype(BF16))


def _s5_out_kernel(g_ref, wglut_ref, bglu_ref, nw_ref, wout_ref, o_ref):
    g = jax.nn.gelu(g_ref[...].reshape(D_S5, M5))
    z = _dot(wglut_ref[...], g.astype(BF16)) + bglu_ref[...]
    out = g * jax.nn.sigmoid(z)
    ms = jnp.mean(out * out, axis=0, keepdims=True)
    y5 = out * lax.rsqrt(ms + EPS) * nw_ref[...]
    o = _dot_tn(y5.astype(BF16), wout_ref[...])
    o_ref[...] = o.reshape(NC5, BATCH, D_MODEL)


def _s5_prompt(x_prompt, norm_mix_w, wut, prep, wglut, b_glu, s5_norm_w, wout5):
    ws, tmat, wo, aq = prep[:4]
    x_t = jnp.transpose(x_prompt, (1, 0, 2)).reshape(NC5, Q5, BATCH, D_MODEL)
    const2 = lambda s: (0, 0)
    ut = pl.pallas_call(
        _s5_inproj_kernel, grid=(Q5,),
        in_specs=[pl.BlockSpec((NC5, None, BATCH, D_MODEL), lambda s: (0, s, 0, 0)),
                  pl.BlockSpec((1, D_MODEL), const2),
                  pl.BlockSpec((D_S5, D_MODEL), const2)],
        out_specs=pl.BlockSpec((S5_GROUPS, S5_CH, M5), lambda s: (0, s, 0)),
        out_shape=jax.ShapeDtypeStruct((S5_GROUPS, R5, M5), BF16),
        compiler_params=_cp("arbitrary"), name="s5_inproj",
    )(x_t, norm_mix_w.reshape(1, D_MODEL), wut)

    per_g = lambda shape: pl.BlockSpec((GC,) + shape, lambda g: (g, 0, 0))
    gact, hfin = pl.pallas_call(
        _s5_core_kernel, grid=(S5_GROUPS // GC,),
        in_specs=[per_g((R5, M5)), per_g((R5, R5)), per_g((LANES, R5)), per_g((R5, LANES)),
                  per_g((8, LANES))],
        out_specs=[per_g((R5, M5)), per_g((BATCH, LANES))],
        out_shape=(jax.ShapeDtypeStruct((S5_GROUPS, R5, M5), F32),
                   jax.ShapeDtypeStruct((S5_GROUPS, BATCH, LANES), F32)),
        scratch_shapes=[pltpu.VMEM((GC, M5, LANES), F32)],
        compiler_params=_cp("arbitrary"), name="s5_core",
    )(ut, tmat, ws, wo, aq)

    o5 = pl.pallas_call(
        _s5_out_kernel, grid=(Q5,),
        in_specs=[pl.BlockSpec((S5_GROUPS, S5_CH, M5), lambda s: (0, s, 0)),
                  pl.BlockSpec((D_S5, D_S5), const2),
                  pl.BlockSpec((D_S5, 1), const2),
                  pl.BlockSpec((D_S5, 1), const2),
                  pl.BlockSpec((D_S5, D_MODEL), const2)],
        out_specs=pl.BlockSpec((NC5, None, BATCH, D_MODEL), lambda s: (0, s, 0, 0)),
        out_shape=jax.ShapeDtypeStruct((NC5, Q5, BATCH, D_MODEL), F32),
        compiler_params=_cp("arbitrary"), name="s5_out",
    )(gact, wglut, b_glu.reshape(D_S5, 1), s5_norm_w.reshape(D_S5, 1), wout5)
    return o5.reshape(SEQ, BATCH * D_MODEL), hfin


def _ssd_prompt_kernel(x_ref, o5_ref, nw_ref, wz_ref, wxbc_ref, wdt_ref, cw_ref, cb_ref,
                       dtb_ref, alog_ref, drep_ref, snw_ref, wout_ref,
                       x1_ref, hout_ref, cout_ref,
                       xp_scr, xs_scr, b_scr, c_scr, z_scr, a_scr, dt_scr, y_scr, h_scr):
    j = pl.program_id(1)

    @pl.when(j == 0)
    def _():
        xp_scr[0:SUBLANES, :] = jnp.zeros((SUBLANES, D_CONV), F32)
        h_scr[...] = jnp.zeros(h_scr.shape, F32)

    x = x_ref[...]
    hn = _rms(x, nw_ref[...]).astype(BF16)
    z_scr[...] = _dot(hn, wz_ref[...])
    xp_scr[SUBLANES:, :] = _dot(hn, wxbc_ref[...])
    dt = jax.nn.softplus(_dot(hn, wdt_ref[...]) + dtb_ref[...])
    dt_scr[...] = dt
    a_scr[...] = dt * (-jnp.exp(alog_ref[...]))

    conv = cb_ref[...] + sum(
        xp_scr[SUBLANES - (SSD_CONV - 1) + k:SUBLANES - (SSD_CONV - 1) + k + TL, :] * cw_ref[k:k + 1, :]
        for k in range(SSD_CONV))
    tail = xp_scr[TL:TL + SUBLANES, :]
    xp_scr[0:SUBLANES, :] = tail
    cout_ref[...] = tail
    conv = _silu(conv)
    xs_scr[...] = conv[:, :D_SSD]
    b_scr[...] = conv[:, D_SSD:D_SSD + 2 * SSD_STATE]
    c_scr[...] = conv[:, D_SSD + 2 * SSD_STATE:]

    li = _iota((QS, QS), 0)
    si = _iota((QS, QS), 1)
    causal = li >= si
    tril = causal.astype(F32)
    lo = si < SSD_HEAD_DIM

    def chunk(ci, carry):
        rows = pl.ds(ci * QS, QS)
        acum = _dot(tril, a_scr[rows, :], HI)
        acum_t = jnp.transpose(acum)
        dt_t = jnp.transpose(dt_scr[rows, :])
        alast = acum[QS - 1:QS, :]
        alast_t = acum_t[:, QS - 1:QS]
        for g in range(2):
            bg = b_scr[rows, g * SSD_STATE:(g + 1) * SSD_STATE]
            cg_b = c_scr[rows, g * SSD_STATE:(g + 1) * SSD_STATE].astype(BF16)
            bt = jnp.transpose(bg)
            cb = _dot(cg_b, bt.astype(BF16))
            h_grp = jnp.concatenate([h_scr[4 * g + i] for i in range(4)], axis=1)
            y_off = _dot(cg_b, h_grp.astype(BF16))
            for hp in range(4):
                pr = 4 * g + hp
                cols = slice(pr * LANES, (pr + 1) * LANES)
                xs_pair = xs_scr[rows, cols]
                xs_b = xs_pair.astype(BF16)
                zero = jnp.zeros_like(xs_b)
                xs_half = (jnp.where(lo, xs_b, zero), jnp.where(lo, zero, xs_b))
                acols, y, st = [], None, None
                for par, h in enumerate((2 * pr, 2 * pr + 1)):
                    acol = jnp.broadcast_to(acum[:, h:h + 1], (QS, QS))
                    arow = acum_t[h:h + 1, :]
                    dtrow = dt_t[h:h + 1, :]
                    lmat = jnp.exp(jnp.where(causal, acol - arow, -1e30))
                    m = (cb * lmat * dtrow).astype(BF16)
                    bts = (bt * (jnp.exp(alast_t[h:h + 1, :] - arow) * dtrow)).astype(BF16)
                    yd = _dot(m, xs_half[par])
                    sd = _dot(bts, xs_half[par])
                    y = yd if y is None else y + yd
                    st = sd if st is None else st + sd
                    acols.append(acol)
                h0, h1 = 2 * pr, 2 * pr + 1
                decay = jnp.exp(jnp.where(lo[0:1, :], alast[:, h0:h0 + 1], alast[:, h1:h1 + 1]))
                h_scr[pr] = decay * h_scr[pr] + st
                y = (y + jnp.exp(jnp.where(lo, acols[0], acols[1])) * y_off[:, hp * LANES:(hp + 1) * LANES]
                     + drep_ref[:, cols] * xs_pair)
                y_scr[rows, cols] = y * _silu(z_scr[rows, cols])
        half = D_SSD // 2
        for g in range(2):
            yg = y_scr[rows, g * half:(g + 1) * half]
            yg = yg * lax.rsqrt(jnp.mean(yg * yg, axis=-1, keepdims=True) + EPS)
            y_scr[rows, g * half:(g + 1) * half] = yg * snw_ref[:, g * half:(g + 1) * half]
        return carry

    for ci in range(TL // QS):
        chunk(ci, 0)
    x1_ref[...] = x + o5_ref[...] + _dot(y_scr[...].astype(BF16), wout_ref[...])

    @pl.when(j == pl.num_programs(1) - 1)
    def _():
        hout_ref[...] = h_scr[...]


def _ssd_prompt(x_prompt, o5, norm_mix_w, wz, wxbc, wdt, conv_w, conv_b, dtb, alog, drep, snw, wouts):
    nj = SEQ // TL
    c2 = lambda b, j: (0, 0)
    row = lambda n: pl.BlockSpec((1, n), c2)
    return pl.pallas_call(
        _ssd_prompt_kernel, grid=(BATCH, nj),
        in_specs=[pl.BlockSpec((None, TL, D_MODEL), lambda b, j: (b, j, 0)),
                  pl.BlockSpec((TL, D_MODEL), lambda b, j: (j, b)),
                  row(D_MODEL),
                  pl.BlockSpec((D_MODEL, D_SSD), c2), pl.BlockSpec((D_MODEL, D_CONV), c2),
                  pl.BlockSpec((D_MODEL, LANES), c2),
                  pl.BlockSpec((SSD_CONV, D_CONV), c2), row(D_CONV),
                  row(LANES), row(LANES), row(D_SSD), row(D_SSD),
                  pl.BlockSpec((D_SSD, D_MODEL), c2)],
        out_specs=[pl.BlockSpec((None, TL, D_MODEL), lambda b, j: (b, j, 0)),
                   pl.BlockSpec((None, SSD_HEADS // 2, SSD_STATE, LANES), lambda b, j: (b, 0, 0, 0)),
                   pl.BlockSpec((None, SUBLANES, D_CONV), lambda b, j: (b, 0, 0))],
        out_shape=(jax.ShapeDtypeStruct((BATCH, SEQ, D_MODEL), F32),
                   jax.ShapeDtypeStruct((BATCH, SSD_HEADS // 2, SSD_STATE, LANES), F32),
                   jax.ShapeDtypeStruct((BATCH, SUBLANES, D_CONV), F32)),
        scratch_shapes=[pltpu.VMEM((SUBLANES + TL, D_CONV), F32),
                        pltpu.VMEM((TL, D_SSD), F32),
                        pltpu.VMEM((TL, 2 * SSD_STATE), F32),
                        pltpu.VMEM((TL, 2 * SSD_STATE), F32),
                        pltpu.VMEM((TL, D_SSD), F32),
                        pltpu.VMEM((TL, LANES), F32),
                        pltpu.VMEM((TL, LANES), F32),
                        pltpu.VMEM((TL, D_SSD), F32),
                        pltpu.VMEM((SSD_HEADS // 2, SSD_STATE, LANES), F32)],
        compiler_params=_cp("arbitrary", "arbitrary"), name="ssd_prompt",
    )(x_prompt, o5, norm_mix_w.reshape(1, D_MODEL), wz, wxbc, wdt, conv_w, conv_b, dtb, alog,
      drep, snw, wouts)


def _ffn_kernel(x_ref, nw_ref, w1_ref, w2_ref, fw_ref, o_ref):
    x = x_ref[...]
    hf = _rms(x, nw_ref[...]).astype(BF16)
    acc = x
    blk = D_FF // 4
    for c in range(4):
        h1 = jnp.square(jnp.maximum(_dot(hf, w1_ref[:, c * blk:(c + 1) * blk]), 0.0))
        acc = acc + _dot(h1.astype(BF16), w2_ref[c * blk:(c + 1) * blk, :])
    o_ref[...] = _rms(acc, fw_ref[...])


def _ffn(x1, norm_ffn_w, w1, w2, norm_final_w, tm):
    n = x1.shape[0]
    c2 = lambda i: (0, 0)
    return pl.pallas_call(
        _ffn_kernel, grid=(n // tm,),
        in_specs=[pl.BlockSpec((tm, D_MODEL), lambda i: (i, 0)),
                  pl.BlockSpec((1, D_MODEL), c2),
                  pl.BlockSpec((D_MODEL, D_FF), c2, pipeline_mode=pl.Buffered(1)),
                  pl.BlockSpec((D_FF, D_MODEL), c2, pipeline_mode=pl.Buffered(1)),
                  pl.BlockSpec((1, D_MODEL), c2)],
        out_specs=pl.BlockSpec((tm, D_MODEL), lambda i: (i, 0)),
        out_shape=jax.ShapeDtypeStruct((n, D_MODEL), F32),
        compiler_params=_cp("arbitrary"), name="ffn",
    )(x1, norm_ffn_w.reshape(1, D_MODEL), w1, w2, norm_final_w.reshape(1, D_MODEL))


def _sample_inproj_kernel(x_ref, nw_ref, wut_ref, wz_ref, wxbc_ref, wdt_ref, cw_ref, cb_ref,
                          cbuf_ref, dtb_ref, alog_ref,
                          u_ref, z_ref, xs_ref, xdt_ref, b_ref, c_ref, dec_ref, nconv_ref):
    hn = _rms(x_ref[...], nw_ref[...]).astype(BF16)
    u_ref[...] = _dot_nt(hn, wut_ref[...])
    z_ref[...] = _dot(hn, wz_ref[...])
    xbc = _dot(hn, wxbc_ref[...])
    dt = jax.nn.softplus(_dot(hn, wdt_ref[...]) + dtb_ref[...])
    dec_ref[...] = jnp.exp(dt * (-jnp.exp(alog_ref[...])))
    head_of_col = lax.shift_right_logical(_iota((LANES, D_SSD), 1), 6)
    dt_rep = _dot(dt, (_iota((LANES, D_SSD), 0) == head_of_col).astype(F32), HI)
    conv = cb_ref[...] + xbc * cw_ref[SSD_CONV - 1:SSD_CONV, :]
    for k in range(SSD_CONV - 1):
        conv = conv + cbuf_ref[k] * cw_ref[k:k + 1, :]
    nconv_ref[0] = cbuf_ref[1]
    nconv_ref[1] = cbuf_ref[2]
    nconv_ref[2] = xbc
    conv = _silu(conv)
    xs_ref[...] = conv[:, :D_SSD]
    xdt_ref[...] = conv[:, :D_SSD] * dt_rep
    b_ref[...] = conv[:, D_SSD:D_SSD + 2 * SSD_STATE]
    c_ref[...] = conv[:, D_SSD + 2 * SSD_STATE:]


def _sample_s5_kernel(u_ref, h0_ref, aq_ref, bbt_ref, cst_ref, d_ref, hn_ref, y_ref):
    for i in range(GS):
        u = u_ref[i]
        h0 = h0_ref[i]
        h0s = pltpu.roll(h0, S5_STATE, axis=1)
        hn = aq_ref[i, 3:4, :] * h0 + aq_ref[i, 4:5, :] * h0s + _dot(u, bbt_ref[i], HI)
        hn_ref[i] = hn
        y_ref[i] = _dot(hn, cst_ref[i], HI) + d_ref[i] * u


def _sample_ssd_kernel(dec_ref, h0_ref, xdt_ref, xs_ref, b_ref, c_ref, drep_ref, hn_ref, y_ref):
    blk = pl.program_id(0)
    hpg = SSD_HEADS // 2
    half = D_SSD // 2
    xt = jnp.transpose(jnp.concatenate(
        [xdt_ref[...], jnp.zeros((LANES - SB, D_SSD), F32)], axis=0))
    c_b = c_ref[...].astype(BF16)
    rowi = _iota((SB, half), 0)
    ys = [jnp.zeros((SB, half), F32) for _ in range(2)]
    for jj in range(SB):
        seq = blk * SB + jj
        for g in range(2):
            brow = b_ref[jj:jj + 1, g * SSD_STATE:(g + 1) * SSD_STATE]
            parts = []
            for h in range(g * hpg, (g + 1) * hpg):
                xcol = xt[h * SSD_HEAD_DIM:(h + 1) * SSD_HEAD_DIM, jj:jj + 1]
                hn = dec_ref[seq, h] * h0_ref[jj, h] + xcol * brow
                hn_ref[jj, h] = hn
                parts.append(hn.astype(BF16))
            y_all = _dot_nt(c_b[:, g * SSD_STATE:(g + 1) * SSD_STATE], jnp.concatenate(parts, axis=0))
            ys[g] = jnp.where(rowi == jj, y_all, ys[g])
    y_ref[...] = jnp.concatenate(ys, axis=1) + drep_ref[...] * xs_ref[...]


def _sample_mix_kernel(x_ref, y5_ref, ys_ref, z_ref, wglut_ref, bglu_ref, nw5_ref, snw_ref,
                       wout5_ref, wouts_ref, x1_ref):
    g = jax.nn.gelu(y5_ref[...])
    out = g * jax.nn.sigmoid(_dot_nt(g.astype(BF16), wglut_ref[...]) + bglu_ref[...])
    y5 = _rms(out, nw5_ref[...])
    y = ys_ref[...] * _silu(z_ref[...])
    half = D_SSD // 2
    yn = jnp.concatenate(
        [_rms(y[:, i * half:(i + 1) * half], snw_ref[:, i * half:(i + 1) * half]) for i in range(2)],
        axis=1)
    x1_ref[...] = (x_ref[...] + _dot(y5.astype(BF16), wout5_ref[...])
                   + _dot(yn.astype(BF16), wouts_ref[...]))


def _sample_layer(x_sample, st_re, st_im, st_ssd, st_conv, norm_mix_w, wut, wz, wxbc, wdt,
                  conv_w, conv_b, dtb, alog, drep, snw, prep, s5_d, wglut, b_glu, s5_norm_w,
                  wout5, wouts):
    nb = DEC_BATCH
    aq, bbt, cst = prep[3:6]
    xs2 = x_sample.reshape(nb, D_MODEL)
    sds = lambda *s: jax.ShapeDtypeStruct(s, F32)
    u, z, xs, xdt, bm, cm, dec, nconv = pl.pallas_call(
        _sample_inproj_kernel,
        out_shape=(sds(nb, D_S5), sds(nb, D_SSD), sds(nb, D_SSD), sds(nb, D_SSD),
                   sds(nb, 2 * SSD_STATE), sds(nb, 2 * SSD_STATE), sds(nb, LANES),
                   sds(SSD_CONV - 1, nb, D_CONV)),
        compiler_params=_cp(), name="sample_inproj",
    )(xs2, norm_mix_w.reshape(1, D_MODEL), wut, wz, wxbc, wdt, conv_w, conv_b,
      jnp.transpose(st_conv, (1, 0, 2)), dtb, alog)

    per_g = lambda shape: pl.BlockSpec((GS,) + shape, lambda g: (g, 0, 0))
    h0p = jnp.transpose(jnp.concatenate([st_re, st_im], axis=-1), (1, 0, 2))
    ug = jnp.transpose(u.reshape(nb, S5_GROUPS, S5_CH), (1, 0, 2))
    hn5, y5g = pl.pallas_call(
        _sample_s5_kernel, grid=(S5_GROUPS // GS,),
        in_specs=[per_g((nb, S5_CH)), per_g((nb, LANES)), per_g((8, LANES)),
                  per_g((S5_CH, LANES)), per_g((LANES, S5_CH)), per_g((1, S5_CH))],
        out_specs=[per_g((nb, LANES)), per_g((nb, S5_CH))],
        out_shape=(sds(S5_GROUPS, nb, LANES), sds(S5_GROUPS, nb, S5_CH)),
        compiler_params=_cp("arbitrary"), name="sample_s5",
    )(ug, h0p, aq, bbt, cst, s5_d.reshape(S5_GROUPS, 1, S5_CH))
    y5 = jnp.transpose(y5g, (1, 0, 2)).reshape(nb, D_S5)

    smem = pl.BlockSpec(memory_space=pltpu.SMEM)
    blk2 = lambda n: pl.BlockSpec((SB, n), lambda i: (i, 0))
    st_spec = pl.BlockSpec((SB, SSD_HEADS, SSD_HEAD_DIM, SSD_STATE), lambda i: (i, 0, 0, 0))
    hn_ssd, ys = pl.pallas_call(
        _sample_ssd_kernel, grid=(nb // SB,),
        in_specs=[smem, st_spec, blk2(D_SSD), blk2(D_SSD), blk2(2 * SSD_STATE),
                  blk2(2 * SSD_STATE), pl.BlockSpec((1, D_SSD), lambda i: (0, 0))],
        out_specs=[st_spec, blk2(D_SSD)],
        out_shape=(sds(nb, SSD_HEADS, SSD_HEAD_DIM, SSD_STATE), sds(nb, D_SSD)),
        compiler_params=_cp("arbitrary"), name="sample_ssd",
    )(dec[:, :SSD_HEADS], st_ssd, xdt, xs, bm, cm, drep)

    x1 = pl.pallas_call(
        _sample_mix_kernel, out_shape=sds(nb, D_MODEL),
        compiler_params=_cp(), name="sample_mix",
    )(xs2, y5, ys, z, wglut, b_glu.reshape(1, D_S5), s5_norm_w.reshape(1, D_S5), snw, wout5, wouts)
    return x1, hn5, hn_ssd, nconv


def kernel(x_prompt, x_sample, state_s5_re, state_s5_im, state_ssd, state_conv, norm_mix_w, w_in, s5_lam_re, s5_lam_im, s5_log_dt, s5_b_re, s5_b_im, s5_c_re, s5_c_im, s5_d, s5_w_glu, s5_b_glu, s5_norm_w, ssd_conv_w, ssd_conv_b, ssd_a_log, ssd_dt_bias, ssd_d, ssd_norm_w, w_out, norm_ffn_w, w_ff1, w_ff2, norm_final_w):
    P = S5_STATE
    w_in0 = w_in[0]
    wut = jnp.transpose(w_in0[:, :D_S5]).astype(BF16)
    wz = w_in0[:, D_S5:D_S5 + D_SSD].astype(BF16)
    wxbc = w_in0[:, D_S5 + D_SSD:D_S5 + D_SSD + D_CONV].astype(BF16)
    wdt = jnp.pad(w_in0[:, D_S5 + D_SSD + D_CONV:], ((0, 0), (0, LANES - SSD_HEADS))).astype(BF16)
    wglut = jnp.transpose(s5_w_glu[0]).astype(BF16)
    wout5 = w_out[0, :D_S5].astype(BF16)
    wouts = w_out[0, D_S5:].astype(BF16)
    w1 = w_ff1[0].astype(BF16)
    w2 = w_ff2[0].astype(BF16)
    pad_h = lambda v: jnp.pad(v.reshape(1, SSD_HEADS), ((0, 0), (0, LANES - SSD_HEADS)))
    dtb, alog = pad_h(ssd_dt_bias[0]), pad_h(ssd_a_log[0])
    drep = jnp.repeat(ssd_d[0], SSD_HEAD_DIM).reshape(1, D_SSD)
    snw = ssd_norm_w[0].reshape(1, D_SSD)
    conv_w, conv_b = ssd_conv_w[0], ssd_conv_b[0].reshape(1, D_CONV)

    prep = _s5_prep(s5_lam_re[0], s5_lam_im[0], s5_log_dt[0], s5_b_re[0], s5_b_im[0],
                    s5_c_re[0], s5_c_im[0], s5_d[0])

    o5, hfin5 = _s5_prompt(x_prompt, norm_mix_w[0], wut, prep, wglut, s5_b_glu[0], s5_norm_w[0], wout5)
    x1p, hT, ctail = _ssd_prompt(x_prompt, o5, norm_mix_w[0], wz, wxbc, wdt, conv_w, conv_b,
                                 dtb, alog, drep, snw, wouts)
    y_prompt = _ffn(x1p.reshape(BATCH * SEQ, D_MODEL), norm_ffn_w[0], w1, w2, norm_final_w,
                    TM_FF).reshape(BATCH, SEQ, D_MODEL)
    hfin5 = jnp.transpose(hfin5, (1, 0, 2))
    np_re, np_im = hfin5[None, :, :, :P], hfin5[None, :, :, P:]
    np_ssd = jnp.transpose(
        hT.reshape(BATCH, SSD_HEADS // 2, SSD_STATE, 2, SSD_HEAD_DIM), (0, 1, 3, 4, 2)
    ).reshape(1, BATCH, SSD_HEADS, SSD_HEAD_DIM, SSD_STATE)
    np_conv = ctail[None, :, SUBLANES - (SSD_CONV - 1):, :]

    x1s, hn5, hn_ssd, nconv = _sample_layer(
        x_sample, state_s5_re[0], state_s5_im[0], state_ssd[0], state_conv[0], norm_mix_w[0],
        wut, wz, wxbc, wdt, conv_w, conv_b, dtb, alog, drep, snw, prep, s5_d[0], wglut,
        s5_b_glu[0], s5_norm_w[0], wout5, wouts)
    y_sample = _ffn(x1s, norm_ffn_w[0], w1, w2, norm_final_w, DEC_BATCH).reshape(DEC_BATCH, 1, D_MODEL)
    hn5 = jnp.transpose(hn5, (1, 0, 2))
    ns_re, ns_im = hn5[None, :, :, :P], hn5[None, :, :, P:]
    ns_ssd = hn_ssd[None]
    ns_conv = jnp.transpose(nconv, (1, 0, 2))[None]

    return (y_prompt, y_sample, np_re, np_im, np_ssd, np_conv, ns_re, ns_im, ns_ssd, ns_conv)
```

```python
import functools

import jax
import jax.numpy as jnp
from jax import lax
from jax.experimental import pallas as pl
from jax.experimental.pallas import tpu as pltpu

F32 = jnp.float32
BF16 = jnp.bfloat16
HI = lax.Precision.HIGHEST
EPS = 1e-5

D_MODEL = 1024
BATCH = 8
SEQ = 2048
DEC_BATCH = 128
D_S5 = 1024
S5_CH = 16
S5_GROUPS = 64
S5_STATE = 64
D_SSD = 1024
SSD_HEADS = 16
SSD_HEAD_DIM = 64
SSD_STATE = 128
SSD_CONV = 4
D_CONV = 1536
D_FF = 4096

LANES = 128
SUBLANES = 8
Q5 = 32
NC5 = SEQ // Q5
M5 = NC5 * BATCH
R5 = Q5 * S5_CH
QS = 128
TL = 512
TM_FF = 512
SB = 16
GS = 8
GP = 4
GC = 4
VMEM_LIMIT = 56 * 1024 * 1024
VMEM_LIMIT_FUSED = 62 * 1024 * 1024


def _cp(*sem):
    return pltpu.CompilerParams(dimension_semantics=sem, vmem_limit_bytes=VMEM_LIMIT)


def _rms(x, w):
    return x * lax.rsqrt(jnp.mean(x * x, axis=-1, keepdims=True) + EPS) * w


def _dot(a, b, precision=None):
    return jnp.dot(a, b, preferred_element_type=F32, precision=precision)


def _dot_nt(a, b, precision=None):
    return lax.dot_general(a, b, (((1,), (1,)), ((), ())), preferred_element_type=F32,
                           precision=precision)


def _dot_tn(a, b, precision=None):
    return lax.dot_general(a, b, (((0,), (0,)), ((), ())), preferred_element_type=F32,
                           precision=precision)


def _iota(shape, dim):
    return lax.broadcasted_iota(jnp.int32, shape, dim)


def _silu(x):
    return x * jax.nn.sigmoid(x)


def _dot_split(a, b):
    a_hi = a.astype(BF16)
    b_hi = b.astype(BF16)
    a_lo = (a - a_hi.astype(F32)).astype(BF16)
    b_lo = (b - b_hi.astype(F32)).astype(BF16)
    return _dot(a_hi, b_hi) + _dot(a_lo, b_hi) + _dot(a_hi, b_lo)


def _rows_x16(x):
    return jnp.broadcast_to(x.reshape(Q5, 1, LANES), (Q5, S5_CH, LANES)).reshape(R5, LANES)


def _tile_x32(x):
    return jnp.broadcast_to(x[None], (Q5, S5_CH, LANES)).reshape(R5, LANES)


def _s5_prep_kernel(lr2_ref, li2_ref, ldt_ref, ccat_ref, c2re_ref, c2im_ref, dcol_ref,
                    bt2re_ref, bt2im_ref, ctre_ref, ctim_ref,
                    ws_ref, t_ref, wo_ref, aq_ref, bbt_ref, cst_ref):
    for i in range(GP):
        _s5_prep_group(lr2_ref[i], li2_ref[i], ldt_ref[i], ccat_ref[i], c2re_ref[i], c2im_ref[i],
                       dcol_ref[i], bt2re_ref[i], bt2im_ref[i], ctre_ref[i], ctim_ref[i],
                       ws_ref.at[i], t_ref.at[i], wo_ref.at[i], aq_ref.at[i], bbt_ref.at[i],
                       cst_ref.at[i])


def _s5_prep_group(lr2, li2, ldt, ccat, c2re, c2im, dcol, bt2re, bt2im, ctre, ctim,
                   ws_ref, t_ref, wo_ref, aq_ref, bbt_ref, cst_ref):
    dt = jnp.exp(ldt)

    def zoh(lr, li):
        mag = jnp.exp(lr * dt)
        ab_re = mag * jnp.cos(li * dt)
        ab_im = mag * jnp.sin(li * dt)
        den = lr * lr + li * li
        nr, ni = ab_re - 1.0, ab_im
        return (nr * lr + ni * li) / den, (ni * lr - nr * li) / den

    def powers(lr, li, e):
        mag = jnp.exp(e * (lr * dt))
        ang = e * (li * dt)
        return mag * jnp.cos(ang), mag * jnp.sin(ang)

    lane1 = _iota((1, LANES), 1) < S5_STATE
    lo16 = _iota((S5_CH, LANES), 1) < S5_STATE
    f_re, f_im = zoh(lr2, li2)
    bb_a = f_re * bt2re - f_im * bt2im
    bb_b = f_re * bt2im + f_im * bt2re
    bbt = jnp.where(lo16, bb_a, bb_b)
    bbt_sw = jnp.where(lo16, bb_b, bb_a)
    bbt_ref[...] = bbt

    r = _iota((2 * 40, LANES), 0)
    expo = jnp.where(r < 40, r, jnp.maximum(71 - r, 0)).astype(F32)
    tab_re, tab_im = powers(lr2, li2, expo)
    aq_re, aq_im = tab_re[Q5:Q5 + 1, :], tab_im[Q5:Q5 + 1, :]
    a1_re, a1_im = tab_re[1:2, :], tab_im[1:2, :]
    aq_ref[...] = jnp.concatenate(
        [aq_re, jnp.where(lane1, -aq_im, aq_im), jnp.where(lane1, aq_im, -aq_im),
         a1_re, jnp.where(lane1, -a1_im, a1_im), jnp.zeros((3, LANES), F32)], axis=0)

    ain_re, ain_im = _rows_x16(tab_re[40:40 + Q5, :]), _rows_x16(tab_im[40:40 + Q5, :])
    sign = jnp.where(lane1, -1.0, 1.0)
    ws_t = ain_re * _tile_x32(bbt) + (sign * ain_im) * _tile_x32(bbt_sw)
    ws = jnp.transpose(ws_t)
    ws_ref[...] = ws.astype(BF16)

    upper = _iota((LANES, R5), 0) < S5_STATE
    vmat = _dot_split(ccat, jnp.where(upper, ws, -ws))
    vrow = _iota((S5_CH, R5), 0)
    vcol = _iota((S5_CH, R5), 1)
    vmat = vmat + jnp.where(vcol == (Q5 - 1) * S5_CH + vrow, dcol, 0.0)
    for l in range(Q5):
        width = (l + 1) * S5_CH
        rolled = vmat if width == R5 else pltpu.roll(vmat, width, axis=1)
        t_ref[l * S5_CH:(l + 1) * S5_CH, :] = jnp.where(vcol < width, rolled, 0.0).astype(BF16)

    aout_re, aout_im = _rows_x16(tab_re[1:Q5 + 1, :]), _rows_x16(tab_im[1:Q5 + 1, :])
    c2re_t, c2im_t = _tile_x32(c2re), _tile_x32(c2im)
    wo_ref[...] = jnp.where(_iota((R5, LANES), 1) < S5_STATE,
                            c2re_t * aout_re - c2im_t * aout_im,
                            -(c2re_t * aout_im + c2im_t * aout_re)).astype(BF16)
    cst_ref[...] = jnp.concatenate([ctre, -ctim], axis=0)


def _s5_prep(lam_re, lam_im, log_dt, b_re, b_im, c_re, c_im, d):
    G, P, C = S5_GROUPS, S5_STATE, S5_CH
    dup = lambda a: jnp.concatenate([a, a], axis=-1)
    args = (
        dup(lam_re).reshape(G, 1, 2 * P), dup(lam_im).reshape(G, 1, 2 * P),
        log_dt.reshape(G, 1, 1),
        jnp.concatenate([c_re, c_im], axis=-1), dup(c_re), dup(c_im),
        d.reshape(G, C, 1),
        dup(jnp.swapaxes(b_re, 1, 2)), dup(jnp.swapaxes(b_im, 1, 2)),
        jnp.swapaxes(c_re, 1, 2), jnp.swapaxes(c_im, 1, 2),
    )
    spec = lambda a: pl.BlockSpec((GP,) + a.shape[1:], lambda g: (g, 0, 0))
    out_shapes = (
        jax.ShapeDtypeStruct((G, 2 * P, R5), BF16),
        jax.ShapeDtypeStruct((G, R5, R5), BF16),
        jax.ShapeDtypeStruct((G, R5, 2 * P), BF16),
        jax.ShapeDtypeStruct((G, 8, 2 * P), F32),
        jax.ShapeDtypeStruct((G, C, 2 * P), F32),
        jax.ShapeDtypeStruct((G, 2 * P, C), F32),
    )
    return pl.pallas_call(
        _s5_prep_kernel, grid=(G // GP,),
        in_specs=[spec(a) for a in args],
        out_specs=[pl.BlockSpec((GP,) + s.shape[1:], lambda g: (g, 0, 0)) for s in out_shapes],
        out_shape=out_shapes, compiler_params=_cp("arbitrary"), name="s5_prep",
    )(*args)


def _s5_inproj_kernel(x_ref, nw_ref, wut_ref, ut_ref):
    x = x_ref[...].reshape(M5, D_MODEL)
    hn = _rms(x, nw_ref[...]).astype(BF16)
    ut = _dot_nt(wut_ref[...], hn)
    ut_ref[...] = ut.astype(BF16).reshape(S5_GROUPS, S5_CH, M5)


def _s5_core_kernel(ut_ref, t_ref, ws_ref, wo_ref, aq_ref, y_ref, hfin_ref,
                    hs_scr, sp_scr, sq_scr, yin_scr):
    for i in range(GC):
        ut = ut_ref[i]
        sp = jnp.transpose(_dot(ws_ref[i], ut))
        sp_scr[i] = sp
        sq_scr[i] = pltpu.roll(sp, S5_STATE, axis=1)
        yin_scr[i] = _dot(t_ref[i], ut)
    for i in range(GC):
        a1 = jnp.broadcast_to(aq_ref[i, 0:1, :], (BATCH, LANES))
        a2 = jnp.broadcast_to(aq_ref[i, 1:2, :], (BATCH, LANES))
        a3 = jnp.broadcast_to(aq_ref[i, 2:3, :], (BATCH, LANES))
        hp = jnp.zeros((BATCH, LANES), F32)
        hq = jnp.zeros((BATCH, LANES), F32)
        for k in range(NC5):
            rows = pl.ds(k * BATCH, BATCH)
            hs_scr[i, rows, :] = hp
            hp, hq = (a1 * hp + a2 * hq + sp_scr[i, rows, :],
                      a1 * hq + a3 * hp + sq_scr[i, rows, :])
        hfin_ref[i] = hp
    for i in range(GC):
        y = yin_scr[i] + _dot_nt(wo_ref[i], hs_scr[i].astype(BF16))
        y_ref[i] = y.astype(BF16)


def _s5_out_kernel(g_ref, wglut_ref, bglu_ref, nw_ref, wout_ref, o_ref):
    g = jax.nn.gelu(g_ref[...].astype(F32).reshape(D_S5, M5))
    z = _dot(wglut_ref[...], g.astype(BF16)) + bglu_ref[...]
    out = g * jax.nn.sigmoid(z)
    ms = jnp.mean(out * out, axis=0, keepdims=True)
    y5 = out * lax.rsqrt(ms + EPS) * nw_ref[...]
    o = _dot_tn(y5.astype(BF16), wout_ref[...])
    o_ref[...] = o.reshape(NC5, BATCH, D_MODEL)


def _s5_prompt(x_prompt, norm_mix_w, wut, prep, wglut, b_glu, s5_norm_w, wout5):
    ws, tmat, wo, aq = prep[:4]
    x_v = jnp.transpose(x_prompt, (1, 0, 2)).reshape(NC5, Q5, BATCH, D_MODEL)
    step_blk = pl.BlockSpec((NC5, None, BATCH, D_MODEL), lambda s: (0, s, 0, 0))
    const2 = lambda s: (0, 0)
    ut = pl.pallas_call(
        _s5_inproj_kernel, grid=(Q5,),
        in_specs=[step_blk,
                  pl.BlockSpec((1, D_MODEL), const2),
                  pl.BlockSpec((D_S5, D_MODEL), const2)],
        out_specs=pl.BlockSpec((S5_GROUPS, S5_CH, M5), lambda s: (0, s, 0)),
        out_shape=jax.ShapeDtypeStruct((S5_GROUPS, R5, M5), BF16),
        compiler_params=_cp("arbitrary"), name="s5_inproj",
    )(x_v, norm_mix_w.reshape(1, D_MODEL), wut)

    per_g = lambda shape: pl.BlockSpec((GC,) + shape, lambda g: (g, 0, 0))
    gact, hfin = pl.pallas_call(
        _s5_core_kernel, grid=(S5_GROUPS // GC,),
        in_specs=[per_g((R5, M5)), per_g((R5, R5)), per_g((LANES, R5)), per_g((R5, LANES)),
                  per_g((8, LANES))],
        out_specs=[per_g((R5, M5)), per_g((BATCH, LANES))],
        out_shape=(jax.ShapeDtypeStruct((S5_GROUPS, R5, M5), BF16),
                   jax.ShapeDtypeStruct((S5_GROUPS, BATCH, LANES), F32)),
        scratch_shapes=[pltpu.VMEM((GC, M5, LANES), F32) for _ in range(3)]
        + [pltpu.VMEM((GC, R5, M5), F32)],
        compiler_params=_cp("arbitrary"), name="s5_core",
    )(ut, tmat, ws, wo, aq)

    o5 = pl.pallas_call(
        _s5_out_kernel, grid=(Q5,),
        in_specs=[pl.BlockSpec((S5_GROUPS, S5_CH, M5), lambda s: (0, s, 0)),
                  pl.BlockSpec((D_S5, D_S5), const2),
                  pl.BlockSpec((D_S5, 1), const2),
                  pl.BlockSpec((D_S5, 1), const2),
                  pl.BlockSpec((D_S5, D_MODEL), const2)],
        out_specs=step_blk,
        out_shape=jax.ShapeDtypeStruct((NC5, Q5, BATCH, D_MODEL), F32),
        compiler_params=_cp("arbitrary"), name="s5_out",
    )(gact, wglut, b_glu.reshape(D_S5, 1), s5_norm_w.reshape(D_S5, 1), wout5)
    return o5.reshape(SEQ, BATCH * D_MODEL), hfin


def _ffn_rows(x, nw_ref, w1_ref, w2_ref, fw_ref):
    hf = _rms(x, nw_ref[...]).astype(BF16)
    acc = x
    blk = D_FF // 4
    for c in range(4):
        h1 = jnp.square(jnp.maximum(_dot(hf, w1_ref[:, c * blk:(c + 1) * blk]), 0.0))
        acc = acc + _dot(h1.astype(BF16), w2_ref[c * blk:(c + 1) * blk, :])
    return _rms(acc, fw_ref[...])


def _ssd_ffn_kernel(x_ref, o5_ref, nw_ref, wz_ref, wxbc_ref, wdt_ref, cw_ref, cb_ref,
                    dtb_ref, alog_ref, drep_ref, snw_ref, wout_ref,
                    fnw_ref, w1_ref, w2_ref, fw_ref,
                    y_ref, hout_ref, cout_ref,
                    xp_scr, xs_scr, b_scr, c_scr, z_scr, a_scr, dt_scr, y_scr, h_scr, x1_scr,
                    hf_scr, h1_scr, acc_scr):
    t = pl.program_id(0)
    n_tiles = pl.num_programs(0) - 1
    j = lax.rem(t, SEQ // TL)

    @pl.when(t == 0)
    def _():
        x1_scr[...] = jnp.zeros(x1_scr.shape, F32)

    @pl.when(j == 0)
    def _():
        xp_scr[0:SUBLANES, :] = jnp.zeros((SUBLANES, D_CONV), F32)
        h_scr[...] = jnp.zeros(h_scr.shape, F32)

    blk = D_FF // 4

    def ffn_up(c):
        h1 = jnp.square(jnp.maximum(_dot(hf_scr[...], w1_ref[:, c * blk:(c + 1) * blk]), 0.0))
        h1_scr[...] = h1.astype(BF16)

    def ffn_down(c):
        return _dot(h1_scr[...], w2_ref[c * blk:(c + 1) * blk, :])

    x1_prev = x1_scr[...]
    hf_scr[...] = _rms(x1_prev, fnw_ref[...]).astype(BF16)
    acc_scr[...] = x1_prev

    x = x_ref[...]
    hn = _rms(x, nw_ref[...]).astype(BF16)
    z_scr[...] = _dot(hn, wz_ref[...])
    xp_scr[SUBLANES:, :] = _dot(hn, wxbc_ref[...])
    dt = jax.nn.softplus(_dot(hn, wdt_ref[...]) + dtb_ref[...])
    dt_scr[...] = dt
    a_scr[...] = dt * (-jnp.exp(alog_ref[...]))
    ffn_up(0)

    conv = cb_ref[...] + sum(
        xp_scr[SUBLANES - (SSD_CONV - 1) + k:SUBLANES - (SSD_CONV - 1) + k + TL, :] * cw_ref[k:k + 1, :]
        for k in range(SSD_CONV))
    tail = xp_scr[TL:TL + SUBLANES, :]
    xp_scr[0:SUBLANES, :] = tail
    cout_ref[...] = tail
    conv = _silu(conv)
    xs_scr[...] = conv[:, :D_SSD]
    b_scr[...] = conv[:, D_SSD:D_SSD + 2 * SSD_STATE]
    c_scr[...] = conv[:, D_SSD + 2 * SSD_STATE:]
    acc_scr[...] += ffn_down(0)

    li = _iota((QS, QS), 0)
    si = _iota((QS, QS), 1)
    causal = li >= si
    tril = causal.astype(F32)
    lo = si < SSD_HEAD_DIM

    def chunk(ci, carry):
        rows = pl.ds(ci * QS, QS)
        acum = _dot(tril, a_scr[rows, :], HI)
        acum_t = jnp.transpose(acum)
        dt_t = jnp.transpose(dt_scr[rows, :])
        alast = acum[QS - 1:QS, :]
        alast_t = acum_t[:, QS - 1:QS]
        for g in range(2):
            bg = b_scr[rows, g * SSD_STATE:(g + 1) * SSD_STATE]
            cg_b = c_scr[rows, g * SSD_STATE:(g + 1) * SSD_STATE].astype(BF16)
            bt = jnp.transpose(bg)
            cb = _dot(cg_b, bt.astype(BF16))
            h_grp = jnp.concatenate([h_scr[4 * g + i] for i in range(4)], axis=1)
            y_off = _dot(cg_b, h_grp.astype(BF16))
            for hp in range(4):
                pr = 4 * g + hp
                cols = slice(pr * LANES, (pr + 1) * LANES)
                xs_pair = xs_scr[rows, cols]
                xs_b = xs_pair.astype(BF16)
                zero = jnp.zeros_like(xs_b)
                xs_half = (jnp.where(lo, xs_b, zero), jnp.where(lo, zero, xs_b))
                acols, y, st = [], None, None
                for par, h in enumerate((2 * pr, 2 * pr + 1)):
                    acol = jnp.broadcast_to(acum[:, h:h + 1], (QS, QS))
                    arow = acum_t[h:h + 1, :]
                    dtrow = dt_t[h:h + 1, :]
                    lmat = jnp.exp(jnp.where(causal, acol - arow, -1e30))
                    m = (cb * lmat * dtrow).astype(BF16)
                    bts = (bt * (jnp.exp(alast_t[h:h + 1, :] - arow) * dtrow)).astype(BF16)
                    yd = _dot(m, xs_half[par])
                    sd = _dot(bts, xs_half[par])
                    y = yd if y is None else y + yd
                    st = sd if st is None else st + sd
                    acols.append(acol)
                h0, h1 = 2 * pr, 2 * pr + 1
                decay = jnp.exp(jnp.where(lo[0:1, :], alast[:, h0:h0 + 1], alast[:, h1:h1 + 1]))
                h_scr[pr] = decay * h_scr[pr] + st
                y = (y + jnp.exp(jnp.where(lo, acols[0], acols[1])) * y_off[:, hp * LANES:(hp + 1) * LANES]
                     + drep_ref[:, cols] * xs_pair)
                y_scr[rows, cols] = y * _silu(z_scr[rows, cols])
        half = D_SSD // 2
        for g in range(2):
            yg = y_scr[rows, g * half:(g + 1) * half]
            yg = yg * lax.rsqrt(jnp.mean(yg * yg, axis=-1, keepdims=True) + EPS)
            y_scr[rows, g * half:(g + 1) * half] = yg * snw_ref[:, g * half:(g + 1) * half]
        return carry

    chunk(0, 0)
    ffn_up(1)
    chunk(1, 0)
    acc_scr[...] += ffn_down(1)
    chunk(2, 0)
    ffn_up(2)
    chunk(3, 0)
    acc_scr[...] += ffn_down(2)
    x1_scr[...] = x + o5_ref[...] + _dot(y_scr[...].astype(BF16), wout_ref[...])
    ffn_up(3)
    y_ref[...] = _rms(acc_scr[...] + ffn_down(3), fw_ref[...])

    @pl.when(jnp.logical_and(j == SEQ // TL - 1, t < n_tiles))
    def _():
        hout_ref[...] = h_scr[...]


def _ssd_ffn_prompt(x_prompt, o5, norm_mix_w, wz, wxbc, wdt, conv_w, conv_b, dtb, alog, drep, snw,
                    wouts, norm_ffn_w, w1, w2, norm_final_w):
    nj = SEQ // TL
    n_tiles = BATCH * nj
    c2 = lambda t: (0, 0)
    once = pl.Buffered(1)
    row = lambda n: pl.BlockSpec((1, n), c2, pipeline_mode=once)
    mat = lambda r, c: pl.BlockSpec((r, c), c2, pipeline_mode=once)

    def tile(t):
        tt = jnp.minimum(t, n_tiles - 1)
        return tt // nj, lax.rem(tt, nj)

    def prev(t):
        tp = jnp.maximum(t - 1, 0)
        return tp // nj, lax.rem(tp, nj)

    return pl.pallas_call(
        _ssd_ffn_kernel, grid=(n_tiles + 1,),
        in_specs=[pl.BlockSpec((None, TL, D_MODEL), lambda t: (*tile(t), 0)),
                  pl.BlockSpec((TL, D_MODEL), lambda t: tile(t)[::-1]),
                  row(D_MODEL),
                  mat(D_MODEL, D_SSD), mat(D_MODEL, D_CONV), mat(D_MODEL, LANES),
                  mat(SSD_CONV, D_CONV), row(D_CONV),
                  row(LANES), row(LANES), row(D_SSD), row(D_SSD),
                  mat(D_SSD, D_MODEL),
                  row(D_MODEL), mat(D_MODEL, D_FF), mat(D_FF, D_MODEL), row(D_MODEL)],
        out_specs=[pl.BlockSpec((None, TL, D_MODEL), lambda t: (*prev(t), 0)),
                   pl.BlockSpec((None, SSD_HEADS // 2, SSD_STATE, LANES), lambda t: (tile(t)[0], 0, 0, 0)),
                   pl.BlockSpec((None, SUBLANES, D_CONV), lambda t: (tile(t)[0], 0, 0))],
        out_shape=(jax.ShapeDtypeStruct((BATCH, SEQ, D_MODEL), F32),
                   jax.ShapeDtypeStruct((BATCH, SSD_HEADS // 2, SSD_STATE, LANES), F32),
                   jax.ShapeDtypeStruct((BATCH, SUBLANES, D_CONV), F32)),
        scratch_shapes=[pltpu.VMEM((SUBLANES + TL, D_CONV), F32),
                        pltpu.VMEM((TL, D_SSD), F32),
                        pltpu.VMEM((TL, 2 * SSD_STATE), F32),
                        pltpu.VMEM((TL, 2 * SSD_STATE), F32),
                        pltpu.VMEM((TL, D_SSD), F32),
                        pltpu.VMEM((TL, LANES), F32),
                        pltpu.VMEM((TL, LANES), F32),
                        pltpu.VMEM((TL, D_SSD), F32),
                        pltpu.VMEM((SSD_HEADS // 2, SSD_STATE, LANES), F32),
                        pltpu.VMEM((TL, D_MODEL), F32),
                        pltpu.VMEM((TL, D_MODEL), BF16),
                        pltpu.VMEM((TL, D_FF // 4), BF16),
                        pltpu.VMEM((TL, D_MODEL), F32)],
        compiler_params=pltpu.CompilerParams(dimension_semantics=("arbitrary",),
                                             vmem_limit_bytes=VMEM_LIMIT_FUSED),
        name="ssd_ffn",
    )(x_prompt, o5, norm_mix_w.reshape(1, D_MODEL), wz, wxbc, wdt, conv_w, conv_b, dtb, alog,
      drep, snw, wouts, norm_ffn_w.reshape(1, D_MODEL), w1, w2, norm_final_w.reshape(1, D_MODEL))


def _ffn_kernel(x_ref, nw_ref, w1_ref, w2_ref, fw_ref, o_ref):
    o_ref[...] = _ffn_rows(x_ref[...], nw_ref, w1_ref, w2_ref, fw_ref)


def _ffn(x1, norm_ffn_w, w1, w2, norm_final_w, tm):
    n = x1.shape[0]
    c2 = lambda i: (0, 0)
    return pl.pallas_call(
        _ffn_kernel, grid=(n // tm,),
        in_specs=[pl.BlockSpec((tm, D_MODEL), lambda i: (i, 0)),
                  pl.BlockSpec((1, D_MODEL), c2),
                  pl.BlockSpec((D_MODEL, D_FF), c2, pipeline_mode=pl.Buffered(1)),
                  pl.BlockSpec((D_FF, D_MODEL), c2, pipeline_mode=pl.Buffered(1)),
                  pl.BlockSpec((1, D_MODEL), c2)],
        out_specs=pl.BlockSpec((tm, D_MODEL), lambda i: (i, 0)),
        out_shape=jax.ShapeDtypeStruct((n, D_MODEL), F32),
        compiler_params=_cp("arbitrary"), name="ffn",
    )(x1, norm_ffn_w.reshape(1, D_MODEL), w1, w2, norm_final_w.reshape(1, D_MODEL))


def _sample_inproj_kernel(x_ref, nw_ref, wut_ref, wz_ref, wxbc_ref, wdt_ref, cw_ref, cb_ref,
                          cbuf_ref, dtb_ref, alog_ref,
                          u_ref, z_ref, xs_ref, xdt_ref, b_ref, c_ref, dec_ref, nconv_ref):
    hn = _rms(x_ref[...], nw_ref[...]).astype(BF16)
    u_ref[...] = _dot_nt(hn, wut_ref[...])
    z_ref[...] = _dot(hn, wz_ref[...])
    xbc = _dot(hn, wxbc_ref[...])
    dt = jax.nn.softplus(_dot(hn, wdt_ref[...]) + dtb_ref[...])
    dec_ref[...] = jnp.exp(dt * (-jnp.exp(alog_ref[...])))
    head_of_col = lax.shift_right_logical(_iota((LANES, D_SSD), 1), 6)
    dt_rep = _dot(dt, (_iota((LANES, D_SSD), 0) == head_of_col).astype(F32), HI)
    conv = cb_ref[...] + xbc * cw_ref[SSD_CONV - 1:SSD_CONV, :]
    for k in range(SSD_CONV - 1):
        conv = conv + cbuf_ref[k] * cw_ref[k:k + 1, :]
    nconv_ref[0] = cbuf_ref[1]
    nconv_ref[1] = cbuf_ref[2]
    nconv_ref[2] = xbc
    conv = _silu(conv)
    xs_ref[...] = conv[:, :D_SSD]
    xdt_ref[...] = conv[:, :D_SSD] * dt_rep
    b_ref[...] = conv[:, D_SSD:D_SSD + 2 * SSD_STATE]
    c_ref[...] = conv[:, D_SSD + 2 * SSD_STATE:]


def _sample_s5_kernel(u_ref, h0_ref, aq_ref, bbt_ref, cst_ref, d_ref, hn_ref, y_ref):
    for i in range(GS):
        u = u_ref[i]
        h0 = h0_ref[i]
        h0s = pltpu.roll(h0, S5_STATE, axis=1)
        hn = aq_ref[i, 3:4, :] * h0 + aq_ref[i, 4:5, :] * h0s + _dot(u, bbt_ref[i], HI)
        hn_ref[i] = hn
        y_ref[i] = _dot(hn, cst_ref[i], HI) + d_ref[i] * u


def _sample_ssd_kernel(dec_ref, h0_ref, xdt_ref, xs_ref, b_ref, c_ref, drep_ref, hn_ref, y_ref):
    blk = pl.program_id(0)
    hpg = SSD_HEADS // 2
    half = D_SSD // 2
    xt = jnp.transpose(jnp.concatenate(
        [xdt_ref[...], jnp.zeros((LANES - SB, D_SSD), F32)], axis=0))
    c_b = c_ref[...].astype(BF16)
    rowi = _iota((SB, half), 0)
    ys = [jnp.zeros((SB, half), F32) for _ in range(2)]
    for jj in range(SB):
        seq = blk * SB + jj
        for g in range(2):
            brow = b_ref[jj:jj + 1, g * SSD_STATE:(g + 1) * SSD_STATE]
            parts = []
            for h in range(g * hpg, (g + 1) * hpg):
                xcol = xt[h * SSD_HEAD_DIM:(h + 1) * SSD_HEAD_DIM, jj:jj + 1]
                hn = dec_ref[seq, h] * h0_ref[jj, h] + xcol * brow
                hn_ref[jj, h] = hn
                parts.append(hn.astype(BF16))
            y_all = _dot_nt(c_b[:, g * SSD_STATE:(g + 1) * SSD_STATE], jnp.concatenate(parts, axis=0))
            ys[g] = jnp.where(rowi == jj, y_all, ys[g])
    y_ref[...] = jnp.concatenate(ys, axis=1) + drep_ref[...] * xs_ref[...]


def _sample_mix_kernel(x_ref, y5_ref, ys_ref, z_ref, wglut_ref, bglu_ref, nw5_ref, snw_ref,
                       wout5_ref, wouts_ref, x1_ref):
    g = jax.nn.gelu(y5_ref[...])
    out = g * jax.nn.sigmoid(_dot_nt(g.astype(BF16), wglut_ref[...]) + bglu_ref[...])
    y5 = _rms(out, nw5_ref[...])
    y = ys_ref[...] * _silu(z_ref[...])
    half = D_SSD // 2
    yn = jnp.concatenate(
        [_rms(y[:, i * half:(i + 1) * half], snw_ref[:, i * half:(i + 1) * half]) for i in range(2)],
        axis=1)
    x1_ref[...] = (x_ref[...] + _dot(y5.astype(BF16), wout5_ref[...])
                   + _dot(yn.astype(BF16), wouts_ref[...]))


def _sample_layer(x_sample, st_re, st_im, st_ssd, st_conv, norm_mix_w, wut, wz, wxbc, wdt,
                  conv_w, conv_b, dtb, alog, drep, snw, prep, s5_d, wglut, b_glu, s5_norm_w,
                  wout5, wouts):
    nb = DEC_BATCH
    aq, bbt, cst = prep[3:6]
    xs2 = x_sample.reshape(nb, D_MODEL)
    sds = lambda *s: jax.ShapeDtypeStruct(s, F32)
    u, z, xs, xdt, bm, cm, dec, nconv = pl.pallas_call(
        _sample_inproj_kernel,
        out_shape=(sds(nb, D_S5), sds(nb, D_SSD), sds(nb, D_SSD), sds(nb, D_SSD),
                   sds(nb, 2 * SSD_STATE), sds(nb, 2 * SSD_STATE), sds(nb, LANES),
                   sds(SSD_CONV - 1, nb, D_CONV)),
        compiler_params=_cp(), name="sample_inproj",
    )(xs2, norm_mix_w.reshape(1, D_MODEL), wut, wz, wxbc, wdt, conv_w, conv_b,
      jnp.transpose(st_conv, (1, 0, 2)), dtb, alog)

    per_g = lambda shape: pl.BlockSpec((GS,) + shape, lambda g: (g, 0, 0))
    h0p = jnp.transpose(jnp.concatenate([st_re, st_im], axis=-1), (1, 0, 2))
    ug = jnp.transpose(u.reshape(nb, S5_GROUPS, S5_CH), (1, 0, 2))
    hn5, y5g = pl.pallas_call(
        _sample_s5_kernel, grid=(S5_GROUPS // GS,),
        in_specs=[per_g((nb, S5_CH)), per_g((nb, LANES)), per_g((8, LANES)),
                  per_g((S5_CH, LANES)), per_g((LANES, S5_CH)), per_g((1, S5_CH))],
        out_specs=[per_g((nb, LANES)), per_g((nb, S5_CH))],
        out_shape=(sds(S5_GROUPS, nb, LANES), sds(S5_GROUPS, nb, S5_CH)),
        compiler_params=_cp("arbitrary"), name="sample_s5",
    )(ug, h0p, aq, bbt, cst, s5_d.reshape(S5_GROUPS, 1, S5_CH))
    y5 = jnp.transpose(y5g, (1, 0, 2)).reshape(nb, D_S5)

    smem = pl.BlockSpec(memory_space=pltpu.SMEM)
    blk2 = lambda n: pl.BlockSpec((SB, n), lambda i: (i, 0))
    st_spec = pl.BlockSpec((SB, SSD_HEADS, SSD_HEAD_DIM, SSD_STATE), lambda i: (i, 0, 0, 0))
    hn_ssd, ys = pl.pallas_call(
        _sample_ssd_kernel, grid=(nb // SB,),
        in_specs=[smem, st_spec, blk2(D_SSD), blk2(D_SSD), blk2(2 * SSD_STATE),
                  blk2(2 * SSD_STATE), pl.BlockSpec((1, D_SSD), lambda i: (0, 0))],
        out_specs=[st_spec, blk2(D_SSD)],
        out_shape=(sds(nb, SSD_HEADS, SSD_HEAD_DIM, SSD_STATE), sds(nb, D_SSD)),
        compiler_params=_cp("arbitrary"), name="sample_ssd",
    )(dec[:, :SSD_HEADS], st_ssd, xdt, xs, bm, cm, drep)

    x1 = pl.pallas_call(
        _sample_mix_kernel, out_shape=sds(nb, D_MODEL),
        compiler_params=_cp(), name="sample_mix",
    )(xs2, y5, ys, z, wglut, b_glu.reshape(1, D_S5), s5_norm_w.reshape(1, D_S5), snw, wout5, wouts)
    return x1, hn5, hn_ssd, nconv


def kernel(x_prompt, x_sample, state_s5_re, state_s5_im, state_ssd, state_conv, norm_mix_w, w_in, s5_lam_re, s5_lam_im, s5_log_dt, s5_b_re, s5_b_im, s5_c_re, s5_c_im, s5_d, s5_w_glu, s5_b_glu, s5_norm_w, ssd_conv_w, ssd_conv_b, ssd_a_log, ssd_dt_bias, ssd_d, ssd_norm_w, w_out, norm_ffn_w, w_ff1, w_ff2, norm_final_w):
    P = S5_STATE
    w_in0 = w_in[0]
    wut = jnp.transpose(w_in0[:, :D_S5]).astype(BF16)
    wz = w_in0[:, D_S5:D_S5 + D_SSD].astype(BF16)
    wxbc = w_in0[:, D_S5 + D_SSD:D_S5 + D_SSD + D_CONV].astype(BF16)
    wdt = jnp.pad(w_in0[:, D_S5 + D_SSD + D_CONV:], ((0, 0), (0, LANES - SSD_HEADS))).astype(BF16)
    wglut = jnp.transpose(s5_w_glu[0]).astype(BF16)
    wout5 = w_out[0, :D_S5].astype(BF16)
    wouts = w_out[0, D_S5:].astype(BF16)
    w1 = w_ff1[0].astype(BF16)
    w2 = w_ff2[0].astype(BF16)
    pad_h = lambda v: jnp.pad(v.reshape(1, SSD_HEADS), ((0, 0), (0, LANES - SSD_HEADS)))
    dtb, alog = pad_h(ssd_dt_bias[0]), pad_h(ssd_a_log[0])
    drep = jnp.repeat(ssd_d[0], SSD_HEAD_DIM).reshape(1, D_SSD)
    snw = ssd_norm_w[0].reshape(1, D_SSD)
    conv_w, conv_b = ssd_conv_w[0], ssd_conv_b[0].reshape(1, D_CONV)

    prep = _s5_prep(s5_lam_re[0], s5_lam_im[0], s5_log_dt[0], s5_b_re[0], s5_b_im[0],
                    s5_c_re[0], s5_c_im[0], s5_d[0])

    o5, hfin5 = _s5_prompt(x_prompt, norm_mix_w[0], wut, prep, wglut, s5_b_glu[0], s5_norm_w[0], wout5)
    y_prompt, hT, ctail = _ssd_ffn_prompt(x_prompt, o5, norm_mix_w[0], wz, wxbc, wdt, conv_w, conv_b,
                                          dtb, alog, drep, snw, wouts, norm_ffn_w[0], w1, w2,
                                          norm_final_w)
    hfin5 = jnp.transpose(hfin5, (1, 0, 2))
    np_re, np_im = hfin5[None, :, :, :P], hfin5[None, :, :, P:]
    np_ssd = jnp.transpose(
        hT.reshape(BATCH, SSD_HEADS // 2, SSD_STATE, 2, SSD_HEAD_DIM), (0, 1, 3, 4, 2)
    ).reshape(1, BATCH, SSD_HEADS, SSD_HEAD_DIM, SSD_STATE)
    np_conv = ctail[None, :, SUBLANES - (SSD_CONV - 1):, :]

    x1s, hn5, hn_ssd, nconv = _sample_layer(
        x_sample, state_s5_re[0], state_s5_im[0], state_ssd[0], state_conv[0], norm_mix_w[0],
        wut, wz, wxbc, wdt, conv_w, conv_b, dtb, alog, drep, snw, prep, s5_d[0], wglut,
        s5_b_glu[0], s5_norm_w[0], wout5, wouts)
    y_sample = _ffn(x1s, norm_ffn_w[0], w1, w2, norm_final_w, DEC_BATCH).reshape(DEC_BATCH, 1, D_MODEL)
    hn5 = jnp.transpose(hn5, (1, 0, 2))
    ns_re, ns_im = hn5[None, :, :, :P], hn5[None, :, :, P:]
    ns_ssd = hn_ssd[None]
    ns_conv = jnp.transpose(nconv, (1, 0, 2))[None]

    return (y_prompt, y_sample, np_re, np_im, np_ssd, np_conv, ns_re, ns_im, ns_ssd, ns_conv)
```

```python
import functools

import jax
import jax.numpy as jnp
from jax import lax
from jax.experimental import pallas as pl
from jax.experimental.pallas import tpu as pltpu

F32 = jnp.float32
BF16 = jnp.bfloat16
HI = lax.Precision.HIGHEST
EPS = 1e-5

D_MODEL = 1024
BATCH = 8
SEQ = 2048
DEC_BATCH = 128
D_S5 = 1024
S5_CH = 16
S5_GROUPS = 64
S5_STATE = 64
D_SSD = 1024
SSD_HEADS = 16
SSD_HEAD_DIM = 64
SSD_STATE = 128
SSD_CONV = 4
D_CONV = 1536
D_FF = 4096

LANES = 128
SUBLANES = 8
Q5 = 32
NC5 = SEQ // Q5
M5 = NC5 * BATCH
R5 = Q5 * S5_CH
QS = 128
TL = 512
XBC_BLK = 512
TM_FF = 512
SB = 16
GS = 8
GP = 4
GC = 4
VMEM_LIMIT = 56 * 1024 * 1024
VMEM_LIMIT_FUSED = 62 * 1024 * 1024


def _cp(*sem):
    return pltpu.CompilerParams(dimension_semantics=sem, vmem_limit_bytes=VMEM_LIMIT)


def _rms(x, w):
    return x * lax.rsqrt(jnp.mean(x * x, axis=-1, keepdims=True) + EPS) * w


def _dot(a, b, precision=None):
    return jnp.dot(a, b, preferred_element_type=F32, precision=precision)


def _dot_nt(a, b, precision=None):
    return lax.dot_general(a, b, (((1,), (1,)), ((), ())), preferred_element_type=F32,
                           precision=precision)


def _dot_tn(a, b, precision=None):
    return lax.dot_general(a, b, (((0,), (0,)), ((), ())), preferred_element_type=F32,
                           precision=precision)


def _iota(shape, dim):
    return lax.broadcasted_iota(jnp.int32, shape, dim)


def _silu(x):
    return x * jax.nn.sigmoid(x)


def _dot_split(a, b):
    a_hi = a.astype(BF16)
    b_hi = b.astype(BF16)
    a_lo = (a - a_hi.astype(F32)).astype(BF16)
    b_lo = (b - b_hi.astype(F32)).astype(BF16)
    return _dot(a_hi, b_hi) + _dot(a_lo, b_hi) + _dot(a_hi, b_lo)


def _rows_x16(x):
    return jnp.broadcast_to(x.reshape(Q5, 1, LANES), (Q5, S5_CH, LANES)).reshape(R5, LANES)


def _tile_x32(x):
    return jnp.broadcast_to(x[None], (Q5, S5_CH, LANES)).reshape(R5, LANES)


def _s5_prep_kernel(lr2_ref, li2_ref, ldt_ref, ccat_ref, c2re_ref, c2im_ref, dcol_ref,
                    bt2re_ref, bt2im_ref, ctre_ref, ctim_ref,
                    ws_ref, t_ref, wo_ref, aq_ref, bbt_ref, cst_ref):
    for i in range(GP):
        _s5_prep_group(lr2_ref[i], li2_ref[i], ldt_ref[i], ccat_ref[i], c2re_ref[i], c2im_ref[i],
                       dcol_ref[i], bt2re_ref[i], bt2im_ref[i], ctre_ref[i], ctim_ref[i],
                       ws_ref.at[i], t_ref.at[i], wo_ref.at[i], aq_ref.at[i], bbt_ref.at[i],
                       cst_ref.at[i])


def _s5_prep_group(lr2, li2, ldt, ccat, c2re, c2im, drow, bt2re, bt2im, ctre, ctim,
                   ws_ref, t_ref, wo_ref, aq_ref, bbt_ref, cst_ref):
    dt = jnp.exp(ldt)

    def zoh(lr, li):
        mag = jnp.exp(lr * dt)
        ab_re = mag * jnp.cos(li * dt)
        ab_im = mag * jnp.sin(li * dt)
        den = lr * lr + li * li
        nr, ni = ab_re - 1.0, ab_im
        return (nr * lr + ni * li) / den, (ni * lr - nr * li) / den

    def powers(lr, li, e):
        mag = jnp.exp(e * (lr * dt))
        ang = e * (li * dt)
        return mag * jnp.cos(ang), mag * jnp.sin(ang)

    lane1 = _iota((1, LANES), 1) < S5_STATE
    lo16 = _iota((S5_CH, LANES), 1) < S5_STATE
    f_re, f_im = zoh(lr2, li2)
    bb_a = f_re * bt2re - f_im * bt2im
    bb_b = f_re * bt2im + f_im * bt2re
    bbt = jnp.where(lo16, bb_a, bb_b)
    bbt_sw = jnp.where(lo16, bb_b, bb_a)
    bbt_ref[...] = bbt

    r = _iota((2 * 40, LANES), 0)
    expo = jnp.where(r < 40, r, jnp.maximum(71 - r, 0)).astype(F32)
    tab_re, tab_im = powers(lr2, li2, expo)
    aq_re, aq_im = tab_re[Q5:Q5 + 1, :], tab_im[Q5:Q5 + 1, :]
    a1_re, a1_im = tab_re[1:2, :], tab_im[1:2, :]
    aq_ref[...] = jnp.concatenate(
        [aq_re, jnp.where(lane1, -aq_im, aq_im), jnp.where(lane1, aq_im, -aq_im),
         a1_re, jnp.where(lane1, -a1_im, a1_im), jnp.zeros((3, LANES), F32)], axis=0)

    ain_re, ain_im = _rows_x16(tab_re[40:40 + Q5, :]), _rows_x16(tab_im[40:40 + Q5, :])
    sign = jnp.where(lane1, -1.0, 1.0)
    ws_t = ain_re * _tile_x32(bbt) + (sign * ain_im) * _tile_x32(bbt_sw)
    ws = jnp.transpose(ws_t)
    ws_ref[...] = ws.astype(BF16)

    upper = _iota((LANES, R5), 0) < S5_STATE
    vmat = _dot_split(ccat, jnp.where(upper, ws, -ws))
    vrow = _iota((S5_CH, R5), 0)
    vcol = _iota((S5_CH, R5), 1)
    vmat = vmat + jnp.where(vcol == (Q5 - 1) * S5_CH + vrow, drow, 0.0)
    for l in range(Q5):
        width = (l + 1) * S5_CH
        rolled = vmat if width == R5 else pltpu.roll(vmat, width, axis=1)
        t_ref[l * S5_CH:(l + 1) * S5_CH, :] = jnp.where(vcol < width, rolled, 0.0).astype(BF16)

    aout_re, aout_im = _rows_x16(tab_re[1:Q5 + 1, :]), _rows_x16(tab_im[1:Q5 + 1, :])
    c2re_t, c2im_t = _tile_x32(c2re), _tile_x32(c2im)
    wo_ref[...] = jnp.where(_iota((R5, LANES), 1) < S5_STATE,
                            c2re_t * aout_re - c2im_t * aout_im,
                            -(c2re_t * aout_im + c2im_t * aout_re)).astype(BF16)
    cst_ref[...] = jnp.concatenate([jnp.transpose(ctre), -jnp.transpose(ctim)], axis=0)


def _s5_prep(lam_re, lam_im, log_dt, b_re, b_im, c_re, c_im, d):
    G, P, C = S5_GROUPS, S5_STATE, S5_CH
    dup = lambda a: jnp.concatenate([a, a], axis=-1)
    args = (
        dup(lam_re).reshape(G, 1, 2 * P), dup(lam_im).reshape(G, 1, 2 * P),
        log_dt.reshape(G, 1, 1),
        jnp.concatenate([c_re, c_im], axis=-1), dup(c_re), dup(c_im),
        jnp.pad(d.reshape(G, 1, C), ((0, 0), (0, 0), (R5 - C, 0))),
        dup(jnp.swapaxes(b_re, 1, 2)), dup(jnp.swapaxes(b_im, 1, 2)),
        c_re, c_im,
    )
    spec = lambda a: pl.BlockSpec((GP,) + a.shape[1:], lambda g: (g, 0, 0))
    out_shapes = (
        jax.ShapeDtypeStruct((G, 2 * P, R5), BF16),
        jax.ShapeDtypeStruct((G, R5, R5), BF16),
        jax.ShapeDtypeStruct((G, R5, 2 * P), BF16),
        jax.ShapeDtypeStruct((G, 8, 2 * P), F32),
        jax.ShapeDtypeStruct((G, C, 2 * P), F32),
        jax.ShapeDtypeStruct((G, 2 * P, C), F32),
    )
    return pl.pallas_call(
        _s5_prep_kernel, grid=(G // GP,),
        in_specs=[spec(a) for a in args],
        out_specs=[pl.BlockSpec((GP,) + s.shape[1:], lambda g: (g, 0, 0)) for s in out_shapes],
        out_shape=out_shapes, compiler_params=_cp("arbitrary"), name="s5_prep",
    )(*args)


def _s5_inproj_kernel(x_ref, nw_ref, wut_ref, ut_ref):
    x = x_ref[...].reshape(M5, D_MODEL)
    hn = _rms(x, nw_ref[...]).astype(BF16)
    ut = _dot_nt(wut_ref[...], hn)
    ut_ref[...] = ut.astype(BF16).reshape(S5_GROUPS, S5_CH, M5)


def _s5_core_kernel(ut_ref, t_ref, ws_ref, wo_ref, aq_ref, y_ref, hfin_ref,
                    hs_scr, sp_scr, sq_scr, yin_scr):
    for i in range(GC):
        ut = ut_ref[i]
        sp = jnp.transpose(_dot(ws_ref[i], ut))
        sp_scr[i] = sp
        sq_scr[i] = pltpu.roll(sp, S5_STATE, axis=1)
        yin_scr[i] = _dot(t_ref[i], ut)
    for i in range(GC):
        a1 = jnp.broadcast_to(aq_ref[i, 0:1, :], (BATCH, LANES))
        a2 = jnp.broadcast_to(aq_ref[i, 1:2, :], (BATCH, LANES))
        a3 = jnp.broadcast_to(aq_ref[i, 2:3, :], (BATCH, LANES))
        hp = jnp.zeros((BATCH, LANES), F32)
        hq = jnp.zeros((BATCH, LANES), F32)
        for k in range(NC5):
            rows = pl.ds(k * BATCH, BATCH)
            hs_scr[i, rows, :] = hp
            hp, hq = (a1 * hp + a2 * hq + sp_scr[i, rows, :],
                      a1 * hq + a3 * hp + sq_scr[i, rows, :])
        hfin_ref[i] = hp
    for i in range(GC):
        y = yin_scr[i] + _dot_nt(wo_ref[i], hs_scr[i].astype(BF16))
        y_ref[i] = y.astype(BF16)


def _s5_out_kernel(g_ref, wglut_ref, bglu_ref, nw_ref, wout_ref, o_ref):
    g = jax.nn.gelu(g_ref[...].astype(F32).reshape(D_S5, M5))
    z = _dot(wglut_ref[...], g.astype(BF16)) + bglu_ref[...]
    out = g * jax.nn.sigmoid(z)
    ms = jnp.mean(out * out, axis=0, keepdims=True)
    y5 = out * lax.rsqrt(ms + EPS) * nw_ref[...]
    o = _dot_tn(y5.astype(BF16), wout_ref[...])
    o_ref[...] = o.reshape(NC5, BATCH, D_MODEL)


def _s5_prompt(x_prompt, norm_mix_w, wut, prep, wglut, b_glu, s5_norm_w, wout5):
    ws, tmat, wo, aq = prep[:4]
    x_v = jnp.transpose(x_prompt, (1, 0, 2)).reshape(NC5, Q5, BATCH, D_MODEL)
    step_blk = pl.BlockSpec((NC5, None, BATCH, D_MODEL), lambda s: (0, s, 0, 0))
    const2 = lambda s: (0, 0)
    ut = pl.pallas_call(
        _s5_inproj_kernel, grid=(Q5,),
        in_specs=[step_blk,
                  pl.BlockSpec((1, D_MODEL), const2),
                  pl.BlockSpec((D_S5, D_MODEL), const2)],
        out_specs=pl.BlockSpec((S5_GROUPS, S5_CH, M5), lambda s: (0, s, 0)),
        out_shape=jax.ShapeDtypeStruct((S5_GROUPS, R5, M5), BF16),
        compiler_params=_cp("arbitrary"), name="s5_inproj",
    )(x_v, norm_mix_w.reshape(1, D_MODEL), wut)

    per_g = lambda shape: pl.BlockSpec((GC,) + shape, lambda g: (g, 0, 0))
    gact, hfin = pl.pallas_call(
        _s5_core_kernel, grid=(S5_GROUPS // GC,),
        in_specs=[per_g((R5, M5)), per_g((R5, R5)), per_g((LANES, R5)), per_g((R5, LANES)),
                  per_g((8, LANES))],
        out_specs=[per_g((R5, M5)), per_g((BATCH, LANES))],
        out_shape=(jax.ShapeDtypeStruct((S5_GROUPS, R5, M5), BF16),
                   jax.ShapeDtypeStruct((S5_GROUPS, BATCH, LANES), F32)),
        scratch_shapes=[pltpu.VMEM((GC, M5, LANES), F32) for _ in range(3)]
        + [pltpu.VMEM((GC, R5, M5), F32)],
        compiler_params=_cp("arbitrary"), name="s5_core",
    )(ut, tmat, ws, wo, aq)

    o5 = pl.pallas_call(
        _s5_out_kernel, grid=(Q5,),
        in_specs=[pl.BlockSpec((S5_GROUPS, S5_CH, M5), lambda s: (0, s, 0)),
                  pl.BlockSpec((D_S5, D_S5), const2),
                  pl.BlockSpec((D_S5, 1), const2),
                  pl.BlockSpec((D_S5, 1), const2),
                  pl.BlockSpec((D_S5, D_MODEL), const2)],
        out_specs=step_blk,
        out_shape=jax.ShapeDtypeStruct((NC5, Q5, BATCH, D_MODEL), F32),
        compiler_params=_cp("arbitrary"), name="s5_out",
    )(gact, wglut, b_glu.reshape(D_S5, 1), s5_norm_w.reshape(D_S5, 1), wout5)
    return o5.reshape(SEQ, BATCH * D_MODEL), hfin


def _ffn_rows(x, nw_ref, w1_ref, w2_ref, fw_ref):
    hf = _rms(x, nw_ref[...]).astype(BF16)
    acc = x
    blk = D_FF // 4
    for c in range(4):
        h1 = jnp.square(jnp.maximum(_dot(hf, w1_ref[:, c * blk:(c + 1) * blk]), 0.0))
        acc = acc + _dot(h1.astype(BF16), w2_ref[c * blk:(c + 1) * blk, :])
    return _rms(acc, fw_ref[...])


def _ssd_ffn_kernel(x_ref, o5_ref, nw_ref, wz_ref, wxbc0_ref, wxbc1_ref, wxbc2_ref, wdt_ref,
                    cw_ref, cb_ref, dtb_ref, alog_ref, drep_ref, snw_ref, wout_ref,
                    fnw_ref, w1_ref, w2_ref, fw_ref,
                    y_ref, hout_ref, cout_ref,
                    xp_scr, xs_scr, b_scr, c_scr, z_scr, a_scr, dt_scr, y_scr, h_scr, x1_scr,
                    hf_scr, h1_scr, acc_scr):
    wxbc_refs = (wxbc0_ref, wxbc1_ref, wxbc2_ref)
    t = pl.program_id(0)
    n_tiles = pl.num_programs(0) - 1
    j = lax.rem(t, SEQ // TL)

    @pl.when(t == 0)
    def _():
        x1_scr[...] = jnp.zeros(x1_scr.shape, F32)

    @pl.when(j == 0)
    def _():
        xp_scr[0:SUBLANES, :] = jnp.zeros((SUBLANES, D_CONV), F32)
        h_scr[...] = jnp.zeros(h_scr.shape, F32)

    blk = D_FF // 4
    sl = blk // 4

    def ffn_up(c, q):
        cols = slice(c * blk + q * sl, c * blk + (q + 1) * sl)
        h1 = jnp.square(jnp.maximum(_dot(hf_scr[...], w1_ref[:, cols]), 0.0))
        h1_scr[:, q * sl:(q + 1) * sl] = h1.astype(BF16)

    def ffn_down(c, q):
        cols = slice(q * sl, (q + 1) * sl)
        acc_scr[:, cols] += _dot(h1_scr[...], w2_ref[c * blk:(c + 1) * blk, cols])

    ffn_slices = iter([functools.partial(f, c, q) for c in range(4)
                       for f in (ffn_up, ffn_down) for q in range(4)])

    def ffn_step(n=1):
        for _ in range(n):
            next(ffn_slices)()

    x1_prev = x1_scr[...]
    hf_scr[...] = _rms(x1_prev, fnw_ref[...]).astype(BF16)
    acc_scr[...] = x1_prev

    x = x_ref[...]
    hn = _rms(x, nw_ref[...]).astype(BF16)
    z_scr[...] = _dot_nt(hn, wz_ref[...])
    for i, w_ref in enumerate(wxbc_refs):
        xp_scr[SUBLANES:, i * XBC_BLK:(i + 1) * XBC_BLK] = _dot_nt(hn, w_ref[...])
    wdt = jnp.concatenate([wdt_ref[...], jnp.zeros((LANES - SSD_HEADS, D_MODEL), BF16)], axis=0)
    dt = jax.nn.softplus(_dot_nt(hn, wdt) + dtb_ref[...])
    dt_scr[...] = dt
    a_scr[...] = dt * (-jnp.exp(alog_ref[...]))
    ffn_step(4)

    conv = cb_ref[...] + sum(
        xp_scr[SUBLANES - (SSD_CONV - 1) + k:SUBLANES - (SSD_CONV - 1) + k + TL, :] * cw_ref[k:k + 1, :]
        for k in range(SSD_CONV))
    tail = xp_scr[TL:TL + SUBLANES, :]
    xp_scr[0:SUBLANES, :] = tail
    cout_ref[...] = tail
    conv = _silu(conv)
    xs_scr[...] = conv[:, :D_SSD]
    b_scr[...] = conv[:, D_SSD:D_SSD + 2 * SSD_STATE]
    c_scr[...] = conv[:, D_SSD + 2 * SSD_STATE:]
    ffn_step(4)

    li = _iota((QS, QS), 0)
    si = _iota((QS, QS), 1)
    causal = li >= si
    tril = causal.astype(F32)
    lo = si < SSD_HEAD_DIM

    def chunk(ci, carry):
        rows = pl.ds(ci * QS, QS)
        acum = _dot(tril, a_scr[rows, :], HI)
        acum_t = jnp.transpose(acum)
        dt_t = jnp.transpose(dt_scr[rows, :])
        alast = acum[QS - 1:QS, :]
        alast_t = acum_t[:, QS - 1:QS]
        for g in range(2):
            bg = b_scr[rows, g * SSD_STATE:(g + 1) * SSD_STATE]
            cg_b = c_scr[rows, g * SSD_STATE:(g + 1) * SSD_STATE].astype(BF16)
            bt = jnp.transpose(bg)
            cb = _dot(cg_b, bt.astype(BF16))
            h_grp = jnp.concatenate([h_scr[4 * g + i] for i in range(4)], axis=1)
            y_off = _dot(cg_b, h_grp.astype(BF16))
            for hp in range(4):
                pr = 4 * g + hp
                cols = slice(pr * LANES, (pr + 1) * LANES)
                xs_pair = xs_scr[rows, cols]
                xs_b = xs_pair.astype(BF16)
                zero = jnp.zeros_like(xs_b)
                xs_half = (jnp.where(lo, xs_b, zero), jnp.where(lo, zero, xs_b))
                acols, y, st = [], None, None
                for par, h in enumerate((2 * pr, 2 * pr + 1)):
                    acol = jnp.broadcast_to(acum[:, h:h + 1], (QS, QS))
                    arow = acum_t[h:h + 1, :]
                    dtrow = dt_t[h:h + 1, :]
                    lmat = jnp.exp(jnp.where(causal, acol - arow, -1e30))
                    m = (cb * lmat * dtrow).astype(BF16)
                    bts = (bt * (jnp.exp(alast_t[h:h + 1, :] - arow) * dtrow)).astype(BF16)
                    yd = _dot(m, xs_half[par])
                    sd = _dot(bts, xs_half[par])
                    y = yd if y is None else y + yd
                    st = sd if st is None else st + sd
                    acols.append(acol)
                h0, h1 = 2 * pr, 2 * pr + 1
                decay = jnp.exp(jnp.where(lo[0:1, :], alast[:, h0:h0 + 1], alast[:, h1:h1 + 1]))
                h_scr[pr] = decay * h_scr[pr] + st
                y = (y + jnp.exp(jnp.where(lo, acols[0], acols[1])) * y_off[:, hp * LANES:(hp + 1) * LANES]
                     + drep_ref[:, cols] * xs_pair)
                y_scr[rows, cols] = y * _silu(z_scr[rows, cols])
                if hp != 3:
                    ffn_step()
        half = D_SSD // 2
        for g in range(2):
            yg = y_scr[rows, g * half:(g + 1) * half]
            yg = yg * lax.rsqrt(jnp.mean(yg * yg, axis=-1, keepdims=True) + EPS)
            y_scr[rows, g * half:(g + 1) * half] = yg * snw_ref[:, g * half:(g + 1) * half]
        return carry

    for ci in range(TL // QS):
        chunk(ci, 0)
    assert next(ffn_slices, None) is None, "every FFN slice must have been issued"
    x1_scr[...] = x + o5_ref[...] + _dot(y_scr[...].astype(BF16), wout_ref[...])
    y_ref[...] = _rms(acc_scr[...], fw_ref[...])

    @pl.when(jnp.logical_and(j == SEQ // TL - 1, t < n_tiles))
    def _():
        for pr in range(SSD_HEADS // 2):
            hout_ref[2 * pr:2 * pr + 2] = jnp.transpose(h_scr[pr]).reshape(2, SSD_HEAD_DIM, SSD_STATE)


def _ssd_ffn_prompt(x_prompt, o5, norm_mix_w, w_int, conv_w, conv_b, dtb, alog, drep, snw,
                    w_out_b, norm_ffn_w, w1, w2, norm_final_w):
    nj = SEQ // TL
    n_tiles = BATCH * nj
    c2 = lambda t: (0, 0)
    once = pl.Buffered(1)
    row = lambda n: pl.BlockSpec((1, n), c2, pipeline_mode=once)
    mat = lambda r, c: pl.BlockSpec((r, c), c2, pipeline_mode=once)
    wrows = lambda n, i: pl.BlockSpec((n, D_MODEL), lambda t: (i, 0), pipeline_mode=once)
    xbc0 = (D_S5 + D_SSD) // XBC_BLK

    def tile(t):
        tt = jnp.minimum(t, n_tiles - 1)
        return tt // nj, lax.rem(tt, nj)

    def prev(t):
        tp = jnp.maximum(t - 1, 0)
        return tp // nj, lax.rem(tp, nj)

    return pl.pallas_call(
        _ssd_ffn_kernel, grid=(n_tiles + 1,),
        in_specs=[pl.BlockSpec((None, TL, D_MODEL), lambda t: (*tile(t), 0)),
                  pl.BlockSpec((TL, D_MODEL), lambda t: tile(t)[::-1]),
                  row(D_MODEL),
                  wrows(D_SSD, D_S5 // D_SSD),
                  wrows(XBC_BLK, xbc0), wrows(XBC_BLK, xbc0 + 1), wrows(XBC_BLK, xbc0 + 2),
                  wrows(SSD_HEADS, (D_S5 + D_SSD + D_CONV) // SSD_HEADS),
                  mat(SSD_CONV, D_CONV), row(D_CONV),
                  row(LANES), row(LANES), row(D_SSD), row(D_SSD),
                  wrows(D_SSD, D_S5 // D_SSD),
                  row(D_MODEL), mat(D_MODEL, D_FF), mat(D_FF, D_MODEL), row(D_MODEL)],
        out_specs=[pl.BlockSpec((None, TL, D_MODEL), lambda t: (*prev(t), 0)),
                   pl.BlockSpec((None, SSD_HEADS, SSD_HEAD_DIM, SSD_STATE), lambda t: (tile(t)[0], 0, 0, 0)),
                   pl.BlockSpec((None, SUBLANES, D_CONV), lambda t: (tile(t)[0], 0, 0))],
        out_shape=(jax.ShapeDtypeStruct((BATCH, SEQ, D_MODEL), F32),
                   jax.ShapeDtypeStruct((BATCH, SSD_HEADS, SSD_HEAD_DIM, SSD_STATE), F32),
                   jax.ShapeDtypeStruct((BATCH, SUBLANES, D_CONV), F32)),
        scratch_shapes=[pltpu.VMEM((SUBLANES + TL, D_CONV), F32),
                        pltpu.VMEM((TL, D_SSD), F32),
                        pltpu.VMEM((TL, 2 * SSD_STATE), F32),
                        pltpu.VMEM((TL, 2 * SSD_STATE), F32),
                        pltpu.VMEM((TL, D_SSD), F32),
                        pltpu.VMEM((TL, LANES), F32),
                        pltpu.VMEM((TL, LANES), F32),
                        pltpu.VMEM((TL, D_SSD), F32),
                        pltpu.VMEM((SSD_HEADS // 2, SSD_STATE, LANES), F32),
                        pltpu.VMEM((TL, D_MODEL), F32),
                        pltpu.VMEM((TL, D_MODEL), BF16),
                        pltpu.VMEM((TL, D_FF // 4), BF16),
                        pltpu.VMEM((TL, D_MODEL), F32)],
        compiler_params=pltpu.CompilerParams(dimension_semantics=("arbitrary",),
                                             vmem_limit_bytes=VMEM_LIMIT_FUSED),
        name="ssd_ffn",
    )(x_prompt, o5, norm_mix_w.reshape(1, D_MODEL), w_int, w_int, w_int, w_int, w_int,
      conv_w, conv_b, dtb, alog, drep, snw, w_out_b,
      norm_ffn_w.reshape(1, D_MODEL), w1, w2, norm_final_w.reshape(1, D_MODEL))


def _ffn_kernel(x_ref, nw_ref, w1_ref, w2_ref, fw_ref, o_ref):
    o_ref[...] = _ffn_rows(x_ref[...], nw_ref, w1_ref, w2_ref, fw_ref)


def _ffn(x1, norm_ffn_w, w1, w2, norm_final_w, tm):
    n = x1.shape[0]
    c2 = lambda i: (0, 0)
    return pl.pallas_call(
        _ffn_kernel, grid=(n // tm,),
        in_specs=[pl.BlockSpec((tm, D_MODEL), lambda i: (i, 0)),
                  pl.BlockSpec((1, D_MODEL), c2),
                  pl.BlockSpec((D_MODEL, D_FF), c2, pipeline_mode=pl.Buffered(1)),
                  pl.BlockSpec((D_FF, D_MODEL), c2, pipeline_mode=pl.Buffered(1)),
                  pl.BlockSpec((1, D_MODEL), c2)],
        out_specs=pl.BlockSpec((tm, D_MODEL), lambda i: (i, 0)),
        out_shape=jax.ShapeDtypeStruct((n, D_MODEL), F32),
        compiler_params=_cp("arbitrary"), name="ffn",
    )(x1, norm_ffn_w.reshape(1, D_MODEL), w1, w2, norm_final_w.reshape(1, D_MODEL))


def _sample_inproj_kernel(x_ref, nw_ref, win_ref, cw_ref, cb_ref, cbuf_ref, dtb_ref, alog_ref,
                          u_ref, z_ref, xs_ref, xdt_ref, b_ref, c_ref, dec_ref, nconv_ref):
    hn = _rms(x_ref[...], nw_ref[...]).astype(BF16)
    o_z, o_xbc, o_dt = D_S5, D_S5 + D_SSD, D_S5 + D_SSD + D_CONV
    u_ref[...] = _dot_nt(hn, win_ref[0:o_z, :])
    z_ref[...] = _dot_nt(hn, win_ref[o_z:o_xbc, :])
    xbc = _dot_nt(hn, win_ref[o_xbc:o_dt, :])
    wdt = jnp.concatenate([win_ref[o_dt:o_dt + SSD_HEADS, :],
                           jnp.zeros((LANES - SSD_HEADS, D_MODEL), BF16)], axis=0)
    dt = jax.nn.softplus(_dot_nt(hn, wdt) + dtb_ref[...])
    dec_ref[...] = jnp.exp(dt * (-jnp.exp(alog_ref[...])))
    head_of_col = lax.shift_right_logical(_iota((LANES, D_SSD), 1), 6)
    dt_rep = _dot(dt, (_iota((LANES, D_SSD), 0) == head_of_col).astype(F32), HI)
    conv = cb_ref[...] + xbc * cw_ref[SSD_CONV - 1:SSD_CONV, :]
    for k in range(SSD_CONV - 1):
        conv = conv + cbuf_ref[:, k * D_CONV:(k + 1) * D_CONV] * cw_ref[k:k + 1, :]
    nconv_ref[:, 0:2 * D_CONV] = cbuf_ref[:, D_CONV:3 * D_CONV]
    nconv_ref[:, 2 * D_CONV:3 * D_CONV] = xbc
    conv = _silu(conv)
    xs_ref[...] = conv[:, :D_SSD]
    xdt_ref[...] = conv[:, :D_SSD] * dt_rep
    b_ref[...] = conv[:, D_SSD:D_SSD + 2 * SSD_STATE]
    c_ref[...] = conv[:, D_SSD + 2 * SSD_STATE:]


def _sample_s5_kernel(u_ref, h0_ref, aq_ref, bbt_ref, cst_ref, d_ref, hn_ref, y_ref):
    for i in range(GS):
        u = u_ref[i]
        h0 = h0_ref[i]
        h0s = pltpu.roll(h0, S5_STATE, axis=1)
        hn = aq_ref[i, 3:4, :] * h0 + aq_ref[i, 4:5, :] * h0s + _dot(u, bbt_ref[i], HI)
        hn_ref[i] = hn
        y_ref[i] = _dot(hn, cst_ref[i], HI) + d_ref[i] * u


def _sample_ssd_kernel(dec_ref, h0_ref, xdt_ref, xs_ref, b_ref, c_ref, drep_ref, hn_ref, y_ref):
    blk = pl.program_id(0)
    hpg = SSD_HEADS // 2
    half = D_SSD // 2
    xt = jnp.transpose(jnp.concatenate(
        [xdt_ref[...], jnp.zeros((LANES - SB, D_SSD), F32)], axis=0))
    c_b = c_ref[...].astype(BF16)
    rowi = _iota((SB, half), 0)
    ys = [jnp.zeros((SB, half), F32) for _ in range(2)]
    for jj in range(SB):
        seq = blk * SB + jj
        for g in range(2):
            brow = b_ref[jj:jj + 1, g * SSD_STATE:(g + 1) * SSD_STATE]
            parts = []
            for h in range(g * hpg, (g + 1) * hpg):
                xcol = xt[h * SSD_HEAD_DIM:(h + 1) * SSD_HEAD_DIM, jj:jj + 1]
                hn = dec_ref[seq, h] * h0_ref[jj, h] + xcol * brow
                hn_ref[jj, h] = hn
                parts.append(hn.astype(BF16))
            y_all = _dot_nt(c_b[:, g * SSD_STATE:(g + 1) * SSD_STATE], jnp.concatenate(parts, axis=0))
            ys[g] = jnp.where(rowi == jj, y_all, ys[g])
    y_ref[...] = jnp.concatenate(ys, axis=1) + drep_ref[...] * xs_ref[...]


def _sample_mix_kernel(x_ref, y5_ref, ys_ref, z_ref, wglut_ref, bglu_ref, nw5_ref, snw_ref,
                       wout_ref, x1_ref):
    g = jax.nn.gelu(y5_ref[...])
    out = g * jax.nn.sigmoid(_dot_nt(g.astype(BF16), wglut_ref[...]) + bglu_ref[...])
    y5 = _rms(out, nw5_ref[...])
    y = ys_ref[...] * _silu(z_ref[...])
    half = D_SSD // 2
    yn = jnp.concatenate(
        [_rms(y[:, i * half:(i + 1) * half], snw_ref[:, i * half:(i + 1) * half]) for i in range(2)],
        axis=1)
    x1_ref[...] = (x_ref[...] + _dot(y5.astype(BF16), wout_ref[0:D_S5, :])
                   + _dot(yn.astype(BF16), wout_ref[D_S5:, :]))


def _sample_layer(x_sample, st_re, st_im, st_ssd, st_conv, norm_mix_w, w_int,
                  conv_w, conv_b, dtb, alog, drep, snw, prep, s5_d, wglut, b_glu, s5_norm_w,
                  w_out_b):
    nb = DEC_BATCH
    aq, bbt, cst = prep[3:6]
    xs2 = x_sample.reshape(nb, D_MODEL)
    sds = lambda *s: jax.ShapeDtypeStruct(s, F32)
    u, z, xs, xdt, bm, cm, dec, nconv = pl.pallas_call(
        _sample_inproj_kernel,
        out_shape=(sds(nb, D_S5), sds(nb, D_SSD), sds(nb, D_SSD), sds(nb, D_SSD),
                   sds(nb, 2 * SSD_STATE), sds(nb, 2 * SSD_STATE), sds(nb, LANES),
                   sds(nb, (SSD_CONV - 1) * D_CONV)),
        compiler_params=_cp(), name="sample_inproj",
    )(xs2, norm_mix_w.reshape(1, D_MODEL), w_int, conv_w, conv_b,
      st_conv.reshape(nb, (SSD_CONV - 1) * D_CONV), dtb, alog)

    per_g = lambda shape: pl.BlockSpec((GS,) + shape, lambda g: (g, 0, 0))
    h0p = jnp.transpose(jnp.concatenate([st_re, st_im], axis=-1), (1, 0, 2))
    ug = jnp.transpose(u.reshape(nb, S5_GROUPS, S5_CH), (1, 0, 2))
    hn5, y5g = pl.pallas_call(
        _sample_s5_kernel, grid=(S5_GROUPS // GS,),
        in_specs=[per_g((nb, S5_CH)), per_g((nb, LANES)), per_g((8, LANES)),
                  per_g((S5_CH, LANES)), per_g((LANES, S5_CH)), per_g((1, S5_CH))],
        out_specs=[per_g((nb, LANES)), per_g((nb, S5_CH))],
        out_shape=(sds(S5_GROUPS, nb, LANES), sds(S5_GROUPS, nb, S5_CH)),
        compiler_params=_cp("arbitrary"), name="sample_s5",
    )(ug, h0p, aq, bbt, cst, s5_d.reshape(S5_GROUPS, 1, S5_CH))
    y5 = jnp.transpose(y5g, (1, 0, 2)).reshape(nb, D_S5)

    smem = pl.BlockSpec(memory_space=pltpu.SMEM)
    blk2 = lambda n: pl.BlockSpec((SB, n), lambda i: (i, 0))
    st_spec = pl.BlockSpec((SB, SSD_HEADS, SSD_HEAD_DIM, SSD_STATE), lambda i: (i, 0, 0, 0))
    hn_ssd, ys = pl.pallas_call(
        _sample_ssd_kernel, grid=(nb // SB,),
        in_specs=[smem, st_spec, blk2(D_SSD), blk2(D_SSD), blk2(2 * SSD_STATE),
                  blk2(2 * SSD_STATE), pl.BlockSpec((1, D_SSD), lambda i: (0, 0))],
        out_specs=[st_spec, blk2(D_SSD)],
        out_shape=(sds(nb, SSD_HEADS, SSD_HEAD_DIM, SSD_STATE), sds(nb, D_SSD)),
        compiler_params=_cp("arbitrary"), name="sample_ssd",
    )(dec[:, :SSD_HEADS], st_ssd, xdt, xs, bm, cm, drep)

    x1 = pl.pallas_call(
        _sample_mix_kernel, out_shape=sds(nb, D_MODEL),
        compiler_params=_cp(), name="sample_mix",
    )(xs2, y5, ys, z, wglut, b_glu.reshape(1, D_S5), s5_norm_w.reshape(1, D_S5), snw, w_out_b)
    return x1, hn5, hn_ssd, nconv.reshape(nb, SSD_CONV - 1, D_CONV)


def kernel(x_prompt, x_sample, state_s5_re, state_s5_im, state_ssd, state_conv, norm_mix_w, w_in, s5_lam_re, s5_lam_im, s5_log_dt, s5_b_re, s5_b_im, s5_c_re, s5_c_im, s5_d, s5_w_glu, s5_b_glu, s5_norm_w, ssd_conv_w, ssd_conv_b, ssd_a_log, ssd_dt_bias, ssd_d, ssd_norm_w, w_out, norm_ffn_w, w_ff1, w_ff2, norm_final_w):
    P = S5_STATE
    w_int = jnp.transpose(w_in[0]).astype(BF16)
    wglut = jnp.transpose(s5_w_glu[0]).astype(BF16)
    w_out_b = w_out[0].astype(BF16)
    w1 = w_ff1[0].astype(BF16)
    w2 = w_ff2[0].astype(BF16)
    pad_h = lambda v: jnp.pad(v.reshape(1, SSD_HEADS), ((0, 0), (0, LANES - SSD_HEADS)))
    dtb, alog = pad_h(ssd_dt_bias[0]), pad_h(ssd_a_log[0])
    drep = jnp.repeat(ssd_d[0], SSD_HEAD_DIM).reshape(1, D_SSD)
    snw = ssd_norm_w[0].reshape(1, D_SSD)
    conv_w, conv_b = ssd_conv_w[0], ssd_conv_b[0].reshape(1, D_CONV)

    prep = _s5_prep(s5_lam_re[0], s5_lam_im[0], s5_log_dt[0], s5_b_re[0], s5_b_im[0],
                    s5_c_re[0], s5_c_im[0], s5_d[0])

    o5, hfin5 = _s5_prompt(x_prompt, norm_mix_w[0], w_int, prep, wglut, s5_b_glu[0], s5_norm_w[0],
                           w_out_b)
    y_prompt, h_ssd, ctail = _ssd_ffn_prompt(x_prompt, o5, norm_mix_w[0], w_int, conv_w, conv_b,
                                             dtb, alog, drep, snw, w_out_b, norm_ffn_w[0], w1, w2,
                                             norm_final_w)
    hfin5 = jnp.transpose(hfin5, (1, 0, 2))
    np_re, np_im = hfin5[None, :, :, :P], hfin5[None, :, :, P:]
    np_ssd = h_ssd[None]
    np_conv = ctail[None, :, SUBLANES - (SSD_CONV - 1):, :]

    x1s, hn5, hn_ssd, nconv = _sample_layer(
        x_sample, state_s5_re[0], state_s5_im[0], state_ssd[0], state_conv[0], norm_mix_w[0],
        w_int, conv_w, conv_b, dtb, alog, drep, snw, prep, s5_d[0], wglut,
        s5_b_glu[0], s5_norm_w[0], w_out_b)
    y_sample = _ffn(x1s, norm_ffn_w[0], w1, w2, norm_final_w, DEC_BATCH).reshape(DEC_BATCH, 1, D_MODEL)
    hn5 = jnp.transpose(hn5, (1, 0, 2))
    ns_re, ns_im = hn5[None, :, :, :P], hn5[None, :, :, P:]
    ns_ssd = hn_ssd[None]
    ns_conv = nconv[None]

    return (y_prompt, y_sample, np_re, np_im, np_ssd, np_conv, ns_re, ns_im, ns_ssd, ns_conv)
```

```python
import functools

import jax
import jax.numpy as jnp
from jax import lax
from jax.experimental import pallas as pl
from jax.experimental.pallas import tpu as pltpu

F32 = jnp.float32
BF16 = jnp.bfloat16
HI = lax.Precision.HIGHEST
EPS = 1e-5

D_MODEL = 1024
BATCH = 8
SEQ = 2048
DEC_BATCH = 128
D_S5 = 1024
S5_CH = 16
S5_GROUPS = 64
S5_STATE = 64
D_SSD = 1024
SSD_HEADS = 16
SSD_HEAD_DIM = 64
SSD_STATE = 128
SSD_CONV = 4
D_CONV = 1536
D_FF = 4096

LANES = 128
SUBLANES = 8
Q5 = 32
NC5 = SEQ // Q5
M5 = NC5 * BATCH
R5 = Q5 * S5_CH
QS = 128
TL = 512
XBC_BLK = 512
TM_FF = 512
SB = 16
GS = 8
GP = 8
GC = 4
VMEM_LIMIT = 56 * 1024 * 1024
VMEM_LIMIT_FUSED = 62 * 1024 * 1024


def _cp(*sem):
    return pltpu.CompilerParams(dimension_semantics=sem, vmem_limit_bytes=VMEM_LIMIT)


def _rms(x, w):
    return x * lax.rsqrt(jnp.mean(x * x, axis=-1, keepdims=True) + EPS) * w


def _dot(a, b, precision=None):
    return jnp.dot(a, b, preferred_element_type=F32, precision=precision)


def _dot_nt(a, b, precision=None):
    return lax.dot_general(a, b, (((1,), (1,)), ((), ())), preferred_element_type=F32,
                           precision=precision)


def _dot_tn(a, b, precision=None):
    return lax.dot_general(a, b, (((0,), (0,)), ((), ())), preferred_element_type=F32,
                           precision=precision)


def _iota(shape, dim):
    return lax.broadcasted_iota(jnp.int32, shape, dim)


def _silu(x):
    return x * jax.nn.sigmoid(x)


def _dot_split(a, b):
    a_hi = a.astype(BF16)
    b_hi = b.astype(BF16)
    a_lo = (a - a_hi.astype(F32)).astype(BF16)
    b_lo = (b - b_hi.astype(F32)).astype(BF16)
    return _dot(a_hi, b_hi) + _dot(a_lo, b_hi) + _dot(a_hi, b_lo)


def _rows_x16(x):
    return jnp.broadcast_to(x.reshape(Q5, 1, LANES), (Q5, S5_CH, LANES)).reshape(R5, LANES)


def _tile_x32(x):
    return jnp.broadcast_to(x[None], (Q5, S5_CH, LANES)).reshape(R5, LANES)


def _s5_prep_kernel(lre_ref, lim_ref, ldt_ref, cre_ref, cim_ref, d_ref, bre_ref, bim_ref,
                    ws_ref, t_ref, wo_ref, aq_ref, a1c_ref, bst_ref, cneg_ref, dcb_ref):
    dup = lambda a: jnp.concatenate([a, a], axis=1)
    for i in range(GP):
        c_re, c_im = cre_ref[i], cim_ref[i]
        drow = jnp.concatenate([jnp.zeros((1, R5 - S5_CH), F32), d_ref[i:i + 1, :]], axis=1)
        _s5_prep_group(dup(lre_ref[i:i + 1, :]), dup(lim_ref[i:i + 1, :]), ldt_ref[i:i + 1, :],
                       jnp.concatenate([c_re, c_im], axis=1), dup(c_re), dup(c_im), drow,
                       dup(jnp.transpose(bre_ref[i])), dup(jnp.transpose(bim_ref[i])),
                       ws_ref.at[i], t_ref.at[i], wo_ref.at[i], aq_ref.at[i], a1c_ref.at[i],
                       bst_ref.at[i], cneg_ref.at[i], dcb_ref.at[i])


def _col_bcast(row):
    return jnp.transpose(jnp.broadcast_to(row, (LANES, LANES)))


def _s5_prep_group(lr2, li2, ldt, ccat, c2re, c2im, drow, bt2re, bt2im,
                   ws_ref, t_ref, wo_ref, aq_ref, a1c_ref, bst_ref, cneg_ref, dcb_ref):
    dt = jnp.exp(ldt)

    def zoh(lr, li):
        mag = jnp.exp(lr * dt)
        ab_re = mag * jnp.cos(li * dt)
        ab_im = mag * jnp.sin(li * dt)
        den = lr * lr + li * li
        nr, ni = ab_re - 1.0, ab_im
        return (nr * lr + ni * li) / den, (ni * lr - nr * li) / den

    def powers(lr, li, e):
        mag = jnp.exp(e * (lr * dt))
        ang = e * (li * dt)
        return mag * jnp.cos(ang), mag * jnp.sin(ang)

    lane1 = _iota((1, LANES), 1) < S5_STATE
    lo16 = _iota((S5_CH, LANES), 1) < S5_STATE
    f_re, f_im = zoh(lr2, li2)
    bb_a = f_re * bt2re - f_im * bt2im
    bb_b = f_re * bt2im + f_im * bt2re
    bbt = jnp.where(lo16, bb_a, bb_b)
    bbt_sw = jnp.where(lo16, bb_b, bb_a)

    r = _iota((2 * 40, LANES), 0)
    expo = jnp.where(r < 40, r, jnp.maximum(71 - r, 0)).astype(F32)
    tab_re, tab_im = powers(lr2, li2, expo)
    aq_re, aq_im = tab_re[Q5:Q5 + 1, :], tab_im[Q5:Q5 + 1, :]
    a1_re, a1_im = tab_re[1:2, :], tab_im[1:2, :]
    aq_ref[...] = jnp.concatenate(
        [aq_re, jnp.where(lane1, -aq_im, aq_im), jnp.where(lane1, aq_im, -aq_im),
         jnp.zeros((5, LANES), F32)], axis=0)

    a1c_ref[0] = _col_bcast(a1_re)[0:S5_STATE, :]
    a1c_ref[1] = _col_bcast(a1_im)[0:S5_STATE, :]
    bst_ref[...] = jnp.transpose(
        jnp.concatenate([bbt, jnp.zeros((LANES - S5_CH, LANES), F32)], axis=0))[:, 0:S5_CH]
    cneg_ref[...] = jnp.where(lo16, c2re, -c2im)
    dcb_ref[...] = _col_bcast(drow[:, R5 - LANES:])[LANES - S5_CH:, :]

    ain_re, ain_im = _rows_x16(tab_re[40:40 + Q5, :]), _rows_x16(tab_im[40:40 + Q5, :])
    sign = jnp.where(lane1, -1.0, 1.0)
    ws_t = ain_re * _tile_x32(bbt) + (sign * ain_im) * _tile_x32(bbt_sw)
    ws = jnp.transpose(ws_t)
    ws_ref[...] = ws.astype(BF16)

    upper = _iota((LANES, R5), 0) < S5_STATE
    vmat = _dot_split(ccat, jnp.where(upper, ws, -ws))
    vrow = _iota((S5_CH, R5), 0)
    vcol = _iota((S5_CH, R5), 1)
    vmat = vmat + jnp.where(vcol == (Q5 - 1) * S5_CH + vrow, drow, 0.0)
    for l in range(Q5):
        width = (l + 1) * S5_CH
        rolled = vmat if width == R5 else pltpu.roll(vmat, width, axis=1)
        t_ref[l * S5_CH:(l + 1) * S5_CH, :] = jnp.where(vcol < width, rolled, 0.0).astype(BF16)

    aout_re, aout_im = _rows_x16(tab_re[1:Q5 + 1, :]), _rows_x16(tab_im[1:Q5 + 1, :])
    c2re_t, c2im_t = _tile_x32(c2re), _tile_x32(c2im)
    wo_ref[...] = jnp.where(_iota((R5, LANES), 1) < S5_STATE,
                            c2re_t * aout_re - c2im_t * aout_im,
                            -(c2re_t * aout_im + c2im_t * aout_re)).astype(BF16)


def _s5_prep(lam_re, lam_im, log_dt, b_re, b_im, c_re, c_im, d):
    G, P, C = S5_GROUPS, S5_STATE, S5_CH
    args = (lam_re, lam_im, log_dt.reshape(G, 1), c_re, c_im, d.reshape(G, C), b_re, b_im)
    spec = lambda a: pl.BlockSpec((GP,) + a.shape[1:], lambda g, nd=a.ndim: (g,) + (0,) * (nd - 1))
    out_shapes = (
        jax.ShapeDtypeStruct((G, 2 * P, R5), BF16),
        jax.ShapeDtypeStruct((G, R5, R5), BF16),
        jax.ShapeDtypeStruct((G, R5, 2 * P), BF16),
        jax.ShapeDtypeStruct((G, 8, 2 * P), F32),
        jax.ShapeDtypeStruct((G, 2, P, LANES), F32),
        jax.ShapeDtypeStruct((G, 2 * P, C), F32),
        jax.ShapeDtypeStruct((G, C, 2 * P), F32),
        jax.ShapeDtypeStruct((G, C, LANES), F32),
    )
    return pl.pallas_call(
        _s5_prep_kernel, grid=(G // GP,),
        in_specs=[spec(a) for a in args],
        out_specs=[pl.BlockSpec((GP,) + s.shape[1:], lambda g, nd=len(s.shape): (g,) + (0,) * (nd - 1))
                   for s in out_shapes],
        out_shape=out_shapes, compiler_params=_cp("arbitrary"), name="s5_prep",
    )(*args)


def _s5_inproj_kernel(x_ref, nw_ref, wut_ref, ut_ref):
    x = x_ref[...].reshape(M5, D_MODEL)
    hn = _rms(x, nw_ref[...]).astype(BF16)
    ut = _dot_nt(wut_ref[...], hn)
    ut_ref[...] = ut.astype(BF16).reshape(S5_GROUPS, S5_CH, M5)


def _s5_core_kernel(ut_ref, t_ref, ws_ref, wo_ref, aq_ref, y_ref, hfin_ref,
                    hs_scr, sp_scr, sq_scr, yin_scr):
    for i in range(GC):
        ut = ut_ref[i]
        sp = jnp.transpose(_dot(ws_ref[i], ut))
        sp_scr[i] = sp
        sq_scr[i] = pltpu.roll(sp, S5_STATE, axis=1)
        yin_scr[i] = _dot(t_ref[i], ut)
    for i in range(GC):
        a1 = jnp.broadcast_to(aq_ref[i, 0:1, :], (BATCH, LANES))
        a2 = jnp.broadcast_to(aq_ref[i, 1:2, :], (BATCH, LANES))
        a3 = jnp.broadcast_to(aq_ref[i, 2:3, :], (BATCH, LANES))
        hp = jnp.zeros((BATCH, LANES), F32)
        hq = jnp.zeros((BATCH, LANES), F32)
        for k in range(NC5):
            rows = pl.ds(k * BATCH, BATCH)
            hs_scr[i, rows, :] = hp
            hp, hq = (a1 * hp + a2 * hq + sp_scr[i, rows, :],
                      a1 * hq + a3 * hp + sq_scr[i, rows, :])
        hfin_ref[i] = hp
    for i in range(GC):
        y = yin_scr[i] + _dot_nt(wo_ref[i], hs_scr[i].astype(BF16))
        y_ref[i] = y.astype(BF16)


def _s5_out_kernel(g_ref, wglut_ref, bglu_ref, nw_ref, wout_ref, o_ref):
    g = jax.nn.gelu(g_ref[...].astype(F32).reshape(D_S5, M5))
    z = _dot(wglut_ref[...], g.astype(BF16)) + bglu_ref[...]
    out = g * jax.nn.sigmoid(z)
    ms = jnp.mean(out * out, axis=0, keepdims=True)
    y5 = out * lax.rsqrt(ms + EPS) * nw_ref[...]
    o = _dot_tn(y5.astype(BF16), wout_ref[...])
    o_ref[...] = o.reshape(NC5, BATCH, D_MODEL)


def _s5_prompt(x_prompt, norm_mix_w, wut, prep, wglut, b_glu, s5_norm_w, wout5):
    ws, tmat, wo, aq = prep[:4]
    x_v = jnp.transpose(x_prompt, (1, 0, 2)).reshape(NC5, Q5, BATCH, D_MODEL)
    step_blk = pl.BlockSpec((NC5, None, BATCH, D_MODEL), lambda s: (0, s, 0, 0))
    const2 = lambda s: (0, 0)
    ut = pl.pallas_call(
        _s5_inproj_kernel, grid=(Q5,),
        in_specs=[step_blk,
                  pl.BlockSpec((1, D_MODEL), const2),
                  pl.BlockSpec((D_S5, D_MODEL), const2)],
        out_specs=pl.BlockSpec((S5_GROUPS, S5_CH, M5), lambda s: (0, s, 0)),
        out_shape=jax.ShapeDtypeStruct((S5_GROUPS, R5, M5), BF16),
        compiler_params=_cp("arbitrary"), name="s5_inproj",
    )(x_v, norm_mix_w.reshape(1, D_MODEL), wut)

    per_g = lambda shape: pl.BlockSpec((GC,) + shape, lambda g: (g, 0, 0))
    gact, hfin = pl.pallas_call(
        _s5_core_kernel, grid=(S5_GROUPS // GC,),
        in_specs=[per_g((R5, M5)), per_g((R5, R5)), per_g((LANES, R5)), per_g((R5, LANES)),
                  per_g((8, LANES))],
        out_specs=[per_g((R5, M5)), per_g((BATCH, LANES))],
        out_shape=(jax.ShapeDtypeStruct((S5_GROUPS, R5, M5), BF16),
                   jax.ShapeDtypeStruct((S5_GROUPS, BATCH, LANES), F32)),
        scratch_shapes=[pltpu.VMEM((GC, M5, LANES), F32) for _ in range(3)]
        + [pltpu.VMEM((GC, R5, M5), F32)],
        compiler_params=_cp("arbitrary"), name="s5_core",
    )(ut, tmat, ws, wo, aq)

    o5 = pl.pallas_call(
        _s5_out_kernel, grid=(Q5,),
        in_specs=[pl.BlockSpec((S5_GROUPS, S5_CH, M5), lambda s: (0, s, 0)),
                  pl.BlockSpec((D_S5, D_S5), const2),
                  pl.BlockSpec((D_S5, 1), const2),
                  pl.BlockSpec((D_S5, 1), const2),
                  pl.BlockSpec((D_S5, D_MODEL), const2)],
        out_specs=step_blk,
        out_shape=jax.ShapeDtypeStruct((NC5, Q5, BATCH, D_MODEL), F32),
        compiler_params=_cp("arbitrary"), name="s5_out",
    )(gact, wglut, b_glu.reshape(D_S5, 1), s5_norm_w.reshape(D_S5, 1), wout5)
    return o5.reshape(SEQ, BATCH * D_MODEL), hfin


def _ffn_rows(x, nw_ref, w1_ref, w2_ref, fw_ref):
    hf = _rms(x, nw_ref[...]).astype(BF16)
    acc = x
    blk = D_FF // 4
    for c in range(4):
        h1 = jnp.square(jnp.maximum(_dot(hf, w1_ref[:, c * blk:(c + 1) * blk]), 0.0))
        acc = acc + _dot(h1.astype(BF16), w2_ref[c * blk:(c + 1) * blk, :])
    return _rms(acc, fw_ref[...])


def _ssd_ffn_kernel(x_ref, o5_ref, nw_ref, wz_ref, wxbc0_ref, wxbc1_ref, wxbc2_ref, wdt_ref,
                    cw_ref, cb_ref, dtb_ref, alog_ref, drep_ref, snw_ref, wout_ref,
                    fnw_ref, w1_ref, w2_ref, fw_ref,
                    y_ref, hout_ref, cout_ref,
                    xp_scr, xs_scr, b_scr, c_scr, z_scr, a_scr, dt_scr, y_scr, h_scr, x1_scr,
                    hf_scr, h1_scr, acc_scr):
    wxbc_refs = (wxbc0_ref, wxbc1_ref, wxbc2_ref)
    t = pl.program_id(0)
    n_tiles = pl.num_programs(0) - 1
    j = lax.rem(t, SEQ // TL)

    @pl.when(t == 0)
    def _():
        x1_scr[...] = jnp.zeros(x1_scr.shape, F32)

    @pl.when(j == 0)
    def _():
        xp_scr[0:SUBLANES, :] = jnp.zeros((SUBLANES, D_CONV), F32)
        h_scr[...] = jnp.zeros(h_scr.shape, F32)

    blk = D_FF // 4
    sl = blk // 4

    def ffn_up(c, q):
        cols = slice(c * blk + q * sl, c * blk + (q + 1) * sl)
        h1 = jnp.square(jnp.maximum(_dot(hf_scr[...], w1_ref[:, cols]), 0.0))
        h1_scr[:, q * sl:(q + 1) * sl] = h1.astype(BF16)

    def ffn_down(c, q):
        cols = slice(q * sl, (q + 1) * sl)
        acc_scr[:, cols] += _dot(h1_scr[...], w2_ref[c * blk:(c + 1) * blk, cols])

    ffn_slices = iter([functools.partial(f, c, q) for c in range(4)
                       for f in (ffn_up, ffn_down) for q in range(4)])

    def ffn_step(n=1):
        for _ in range(n):
            next(ffn_slices)()

    x1_prev = x1_scr[...]
    hf_scr[...] = _rms(x1_prev, fnw_ref[...]).astype(BF16)
    acc_scr[...] = x1_prev

    x = x_ref[...]
    hn = _rms(x, nw_ref[...]).astype(BF16)
    z_scr[...] = _dot_nt(hn, wz_ref[...])
    for i, w_ref in enumerate(wxbc_refs):
        xp_scr[SUBLANES:, i * XBC_BLK:(i + 1) * XBC_BLK] = _dot_nt(hn, w_ref[...])
    wdt = jnp.concatenate([wdt_ref[...], jnp.zeros((LANES - SSD_HEADS, D_MODEL), BF16)], axis=0)
    dt = jax.nn.softplus(_dot_nt(hn, wdt) + dtb_ref[...])
    dt_scr[...] = dt
    a_scr[...] = dt * (-jnp.exp(alog_ref[...]))
    ffn_step(4)

    conv = cb_ref[...] + sum(
        xp_scr[SUBLANES - (SSD_CONV - 1) + k:SUBLANES - (SSD_CONV - 1) + k + TL, :] * cw_ref[k:k + 1, :]
        for k in range(SSD_CONV))
    tail = xp_scr[TL:TL + SUBLANES, :]
    xp_scr[0:SUBLANES, :] = tail
    cout_ref[...] = tail
    conv = _silu(conv)
    xs_scr[...] = conv[:, :D_SSD]
    b_scr[...] = conv[:, D_SSD:D_SSD + 2 * SSD_STATE]
    c_scr[...] = conv[:, D_SSD + 2 * SSD_STATE:]
    ffn_step(4)

    li = _iota((QS, QS), 0)
    si = _iota((QS, QS), 1)
    causal = li >= si
    tril = causal.astype(F32)
    lo = si < SSD_HEAD_DIM

    def chunk(ci, carry):
        rows = pl.ds(ci * QS, QS)
        acum = _dot(tril, a_scr[rows, :], HI)
        acum_t = jnp.transpose(acum)
        dt_t = jnp.transpose(dt_scr[rows, :])
        alast = acum[QS - 1:QS, :]
        alast_t = acum_t[:, QS - 1:QS]
        for g in range(2):
            bg = b_scr[rows, g * SSD_STATE:(g + 1) * SSD_STATE]
            cg_b = c_scr[rows, g * SSD_STATE:(g + 1) * SSD_STATE].astype(BF16)
            bt = jnp.transpose(bg)
            cb = _dot(cg_b, bt.astype(BF16))
            h_grp = jnp.concatenate([h_scr[4 * g + i] for i in range(4)], axis=1)
            y_off = _dot(cg_b, h_grp.astype(BF16))
            for hp in range(4):
                pr = 4 * g + hp
                cols = slice(pr * LANES, (pr + 1) * LANES)
                xs_pair = xs_scr[rows, cols]
                xs_b = xs_pair.astype(BF16)
                zero = jnp.zeros_like(xs_b)
                xs_half = (jnp.where(lo, xs_b, zero), jnp.where(lo, zero, xs_b))
                acols, y, st = [], None, None
                for par, h in enumerate((2 * pr, 2 * pr + 1)):
                    acol = jnp.broadcast_to(acum[:, h:h + 1], (QS, QS))
                    arow = acum_t[h:h + 1, :]
                    dtrow = dt_t[h:h + 1, :]
                    lmat = jnp.exp(jnp.where(causal, acol - arow, -1e30))
                    m = (cb * lmat * dtrow).astype(BF16)
                    bts = (bt * (jnp.exp(alast_t[h:h + 1, :] - arow) * dtrow)).astype(BF16)
                    yd = _dot(m, xs_half[par])
                    sd = _dot(bts, xs_half[par])
                    y = yd if y is None else y + yd
                    st = sd if st is None else st + sd
                    acols.append(acol)
                h0, h1 = 2 * pr, 2 * pr + 1
                decay = jnp.exp(jnp.where(lo[0:1, :], alast[:, h0:h0 + 1], alast[:, h1:h1 + 1]))
                h_scr[pr] = decay * h_scr[pr] + st
                y = (y + jnp.exp(jnp.where(lo, acols[0], acols[1])) * y_off[:, hp * LANES:(hp + 1) * LANES]
                     + drep_ref[:, cols] * xs_pair)
                y_scr[rows, cols] = y * _silu(z_scr[rows, cols])
                if hp != 3:
                    ffn_step()
        half = D_SSD // 2
        for g in range(2):
            yg = y_scr[rows, g * half:(g + 1) * half]
            yg = yg * lax.rsqrt(jnp.mean(yg * yg, axis=-1, keepdims=True) + EPS)
            y_scr[rows, g * half:(g + 1) * half] = yg * snw_ref[:, g * half:(g + 1) * half]
        return carry

    for ci in range(TL // QS):
        chunk(ci, 0)
    assert next(ffn_slices, None) is None, "every FFN slice must have been issued"
    x1_scr[...] = x + o5_ref[...] + _dot(y_scr[...].astype(BF16), wout_ref[...])
    y_ref[...] = _rms(acc_scr[...], fw_ref[...])

    @pl.when(jnp.logical_and(j == SEQ // TL - 1, t < n_tiles))
    def _():
        for pr in range(SSD_HEADS // 2):
            hout_ref[2 * pr:2 * pr + 2] = jnp.transpose(h_scr[pr]).reshape(2, SSD_HEAD_DIM, SSD_STATE)


def _ssd_ffn_prompt(x_prompt, o5, norm_mix_w, w_int, conv_w, conv_b, dtb, alog, drep, snw,
                    w_out_b, norm_ffn_w, w1, w2, norm_final_w):
    nj = SEQ // TL
    n_tiles = BATCH * nj
    c2 = lambda t: (0, 0)
    once = pl.Buffered(1)
    row = lambda n: pl.BlockSpec((1, n), c2, pipeline_mode=once)
    mat = lambda r, c: pl.BlockSpec((r, c), c2, pipeline_mode=once)
    wrows = lambda n, i: pl.BlockSpec((n, D_MODEL), lambda t: (i, 0), pipeline_mode=once)
    xbc0 = (D_S5 + D_SSD) // XBC_BLK

    def tile(t):
        tt = jnp.minimum(t, n_tiles - 1)
        return tt // nj, lax.rem(tt, nj)

    def prev(t):
        tp = jnp.maximum(t - 1, 0)
        return tp // nj, lax.rem(tp, nj)

    return pl.pallas_call(
        _ssd_ffn_kernel, grid=(n_tiles + 1,),
        in_specs=[pl.BlockSpec((None, TL, D_MODEL), lambda t: (*tile(t), 0)),
                  pl.BlockSpec((TL, D_MODEL), lambda t: tile(t)[::-1]),
                  row(D_MODEL),
                  wrows(D_SSD, D_S5 // D_SSD),
                  wrows(XBC_BLK, xbc0), wrows(XBC_BLK, xbc0 + 1), wrows(XBC_BLK, xbc0 + 2),
                  wrows(SSD_HEADS, (D_S5 + D_SSD + D_CONV) // SSD_HEADS),
                  mat(SSD_CONV, D_CONV), row(D_CONV),
                  row(LANES), row(LANES), row(D_SSD), row(D_SSD),
                  wrows(D_SSD, D_S5 // D_SSD),
                  row(D_MODEL), mat(D_MODEL, D_FF), mat(D_FF, D_MODEL), row(D_MODEL)],
        out_specs=[pl.BlockSpec((None, TL, D_MODEL), lambda t: (*prev(t), 0)),
                   pl.BlockSpec((None, SSD_HEADS, SSD_HEAD_DIM, SSD_STATE), lambda t: (tile(t)[0], 0, 0, 0)),
                   pl.BlockSpec((None, SUBLANES, D_CONV), lambda t: (tile(t)[0], 0, 0))],
        out_shape=(jax.ShapeDtypeStruct((BATCH, SEQ, D_MODEL), F32),
                   jax.ShapeDtypeStruct((BATCH, SSD_HEADS, SSD_HEAD_DIM, SSD_STATE), F32),
                   jax.ShapeDtypeStruct((BATCH, SUBLANES, D_CONV), F32)),
        scratch_shapes=[pltpu.VMEM((SUBLANES + TL, D_CONV), F32),
                        pltpu.VMEM((TL, D_SSD), F32),
                        pltpu.VMEM((TL, 2 * SSD_STATE), F32),
                        pltpu.VMEM((TL, 2 * SSD_STATE), F32),
                        pltpu.VMEM((TL, D_SSD), F32),
                        pltpu.VMEM((TL, LANES), F32),
                        pltpu.VMEM((TL, LANES), F32),
                        pltpu.VMEM((TL, D_SSD), F32),
                        pltpu.VMEM((SSD_HEADS // 2, SSD_STATE, LANES), F32),
                        pltpu.VMEM((TL, D_MODEL), F32),
                        pltpu.VMEM((TL, D_MODEL), BF16),
                        pltpu.VMEM((TL, D_FF // 4), BF16),
                        pltpu.VMEM((TL, D_MODEL), F32)],
        compiler_params=pltpu.CompilerParams(dimension_semantics=("arbitrary",),
                                             vmem_limit_bytes=VMEM_LIMIT_FUSED),
        name="ssd_ffn",
    )(x_prompt, o5, norm_mix_w.reshape(1, D_MODEL), w_int, w_int, w_int, w_int, w_int,
      conv_w, conv_b, dtb, alog, drep, snw, w_out_b,
      norm_ffn_w.reshape(1, D_MODEL), w1, w2, norm_final_w.reshape(1, D_MODEL))


def _ffn_kernel(x_ref, nw_ref, w1_ref, w2_ref, fw_ref, o_ref):
    o_ref[...] = _ffn_rows(x_ref[...], nw_ref, w1_ref, w2_ref, fw_ref)


def _ffn(x1, norm_ffn_w, w1, w2, norm_final_w, tm):
    n = x1.shape[0]
    c2 = lambda i: (0, 0)
    return pl.pallas_call(
        _ffn_kernel, grid=(n // tm,),
        in_specs=[pl.BlockSpec((tm, D_MODEL), lambda i: (i, 0)),
                  pl.BlockSpec((1, D_MODEL), c2),
                  pl.BlockSpec((D_MODEL, D_FF), c2, pipeline_mode=pl.Buffered(1)),
                  pl.BlockSpec((D_FF, D_MODEL), c2, pipeline_mode=pl.Buffered(1)),
                  pl.BlockSpec((1, D_MODEL), c2)],
        out_specs=pl.BlockSpec((tm, D_MODEL), lambda i: (i, 0)),
        out_shape=jax.ShapeDtypeStruct((n, D_MODEL), F32),
        compiler_params=_cp("arbitrary"), name="ffn",
    )(x1, norm_ffn_w.reshape(1, D_MODEL), w1, w2, norm_final_w.reshape(1, D_MODEL))


def _sample_inproj_kernel(x_ref, nw_ref, win_ref, cw_ref, cb_ref, cbuf_ref, dtb_ref, alog_ref,
                          u_ref, z_ref, xs_ref, xdt_ref, b_ref, c_ref, dec_ref, nconv_ref):
    hn = _rms(x_ref[...], nw_ref[...]).astype(BF16)
    o_z, o_xbc, o_dt = D_S5, D_S5 + D_SSD, D_S5 + D_SSD + D_CONV
    u_ref[...] = _dot_nt(win_ref[0:o_z, :], hn)
    z_ref[...] = _dot_nt(hn, win_ref[o_z:o_xbc, :])
    xbc_t = _dot_nt(win_ref[o_xbc:o_dt, :], hn)
    wdt = jnp.concatenate([win_ref[o_dt:o_dt + SSD_HEADS, :],
                           jnp.zeros((LANES - SSD_HEADS, D_MODEL), BF16)], axis=0)
    dt = jax.nn.softplus(_dot_nt(hn, wdt) + dtb_ref[...])
    dec_ref[...] = jnp.exp(dt * (-jnp.exp(alog_ref[...])))
    head_of_col = lax.shift_right_logical(_iota((LANES, D_SSD), 1), 6)
    dt_rep = _dot(dt, (_iota((LANES, D_SSD), 0) == head_of_col).astype(F32), HI)
    cw_t = jnp.transpose(jnp.concatenate(
        [cw_ref[...], cb_ref[...], jnp.zeros((LANES - SSD_CONV - 1, D_CONV), F32)], axis=0))
    conv_t = cw_t[:, SSD_CONV:SSD_CONV + 1] + xbc_t * cw_t[:, SSD_CONV - 1:SSD_CONV]
    for k in range(SSD_CONV - 1):
        conv_t = conv_t + cbuf_ref[k] * cw_t[:, k:k + 1]
        if k > 0:
            nconv_ref[k - 1] = cbuf_ref[k]
    nconv_ref[SSD_CONV - 2] = xbc_t
    conv = jnp.transpose(_silu(conv_t))
    xs_ref[...] = conv[:, :D_SSD]
    xdt_ref[...] = conv[:, :D_SSD] * dt_rep
    b_ref[...] = conv[:, D_SSD:D_SSD + 2 * SSD_STATE]
    c_ref[...] = conv[:, D_SSD + 2 * SSD_STATE:]


def _sample_s5_kernel(u_ref, hre_ref, him_ref, a1c_ref, bst_ref, cneg_ref, dcb_ref,
                      nre_ref, nim_ref, y_ref):
    for i in range(GS):
        u = u_ref[i]
        bu = _dot(bst_ref[i], u, HI)
        a_re, a_im = a1c_ref[i, 0], a1c_ref[i, 1]
        h_re, h_im = hre_ref[i], him_ref[i]
        n_re = a_re * h_re - a_im * h_im + bu[0:S5_STATE]
        n_im = a_re * h_im + a_im * h_re + bu[S5_STATE:]
        nre_ref[i] = n_re
        nim_ref[i] = n_im
        y_ref[i] = _dot(cneg_ref[i], jnp.concatenate([n_re, n_im], axis=0), HI) + dcb_ref[i] * u


def _sample_ssd_kernel(dec_ref, h0_ref, xdt_ref, xs_ref, b_ref, c_ref, drep_ref, hn_ref, y_ref):
    blk = pl.program_id(0)
    hpg = SSD_HEADS // 2
    half = D_SSD // 2
    xt = jnp.transpose(jnp.concatenate(
        [xdt_ref[...], jnp.zeros((LANES - SB, D_SSD), F32)], axis=0))
    c_b = c_ref[...].astype(BF16)
    rowi = _iota((SB, half), 0)
    ys = [jnp.zeros((SB, half), F32) for _ in range(2)]
    for jj in range(SB):
        seq = blk * SB + jj
        for g in range(2):
            brow = b_ref[jj:jj + 1, g * SSD_STATE:(g + 1) * SSD_STATE]
            parts = []
            for h in range(g * hpg, (g + 1) * hpg):
                xcol = xt[h * SSD_HEAD_DIM:(h + 1) * SSD_HEAD_DIM, jj:jj + 1]
                hn = dec_ref[seq, h] * h0_ref[jj, h] + xcol * brow
                hn_ref[jj, h] = hn
                parts.append(hn.astype(BF16))
            y_all = _dot_nt(c_b[:, g * SSD_STATE:(g + 1) * SSD_STATE], jnp.concatenate(parts, axis=0))
            ys[g] = jnp.where(rowi == jj, y_all, ys[g])
    y_ref[...] = jnp.concatenate(ys, axis=1) + drep_ref[...] * xs_ref[...]


def _sample_mix_kernel(x_ref, y5_ref, ys_ref, z_ref, wglut_ref, bglu_ref, nw5_ref, snw_ref,
                       wout_ref, x1_ref):
    g = jax.nn.gelu(jnp.transpose(y5_ref[...]))
    out = g * jax.nn.sigmoid(_dot_nt(g.astype(BF16), wglut_ref[...]) + bglu_ref[...])
    y5 = _rms(out, nw5_ref[...])
    y = ys_ref[...] * _silu(z_ref[...])
    half = D_SSD // 2
    yn = jnp.concatenate(
        [_rms(y[:, i * half:(i + 1) * half], snw_ref[:, i * half:(i + 1) * half]) for i in range(2)],
        axis=1)
    x1_ref[...] = (x_ref[...] + _dot(y5.astype(BF16), wout_ref[0:D_S5, :])
                   + _dot(yn.astype(BF16), wout_ref[D_S5:, :]))


def _sample_layer(x_sample, st_re, st_im, st_ssd, st_conv, norm_mix_w, w_int,
                  conv_w, conv_b, dtb, alog, drep, snw, prep, s5_d, wglut, b_glu, s5_norm_w,
                  w_out_b):
    nb = DEC_BATCH
    a1c, bst, cneg, dcb = prep[4:8]
    xs2 = x_sample.reshape(nb, D_MODEL)
    sds = lambda *s: jax.ShapeDtypeStruct(s, F32)
    u_t, z, xs, xdt, bm, cm, dec, nconv_t = pl.pallas_call(
        _sample_inproj_kernel,
        out_shape=(sds(D_S5, nb), sds(nb, D_SSD), sds(nb, D_SSD), sds(nb, D_SSD),
                   sds(nb, 2 * SSD_STATE), sds(nb, 2 * SSD_STATE), sds(nb, LANES),
                   sds(SSD_CONV - 1, D_CONV, nb)),
        compiler_params=_cp(), name="sample_inproj",
    )(xs2, norm_mix_w.reshape(1, D_MODEL), w_int, conv_w, conv_b,
      jnp.transpose(st_conv, (1, 2, 0)), dtb, alog)

    def per_g(shape):
        nd = len(shape)
        return pl.BlockSpec((GS,) + shape, lambda g: (g,) + (0,) * nd)

    P = S5_STATE
    n_re_t, n_im_t, y5_t = pl.pallas_call(
        _sample_s5_kernel, grid=(S5_GROUPS // GS,),
        in_specs=[per_g((S5_CH, nb)), per_g((P, nb)), per_g((P, nb)), per_g((2, P, LANES)),
                  per_g((2 * P, S5_CH)), per_g((S5_CH, 2 * P)), per_g((S5_CH, LANES))],
        out_specs=[per_g((P, nb)), per_g((P, nb)), per_g((S5_CH, nb))],
        out_shape=(sds(S5_GROUPS, P, nb), sds(S5_GROUPS, P, nb), sds(S5_GROUPS, S5_CH, nb)),
        compiler_params=_cp("arbitrary"), name="sample_s5",
    )(u_t.reshape(S5_GROUPS, S5_CH, nb), jnp.transpose(st_re, (1, 2, 0)),
      jnp.transpose(st_im, (1, 2, 0)), a1c, bst, cneg, dcb)
    y5 = y5_t.reshape(D_S5, nb)

    smem = pl.BlockSpec(memory_space=pltpu.SMEM)
    blk2 = lambda n: pl.BlockSpec((SB, n), lambda i: (i, 0))
    st_spec = pl.BlockSpec((SB, SSD_HEADS, SSD_HEAD_DIM, SSD_STATE), lambda i: (i, 0, 0, 0))
    hn_ssd, ys = pl.pallas_call(
        _sample_ssd_kernel, grid=(nb // SB,),
        in_specs=[smem, st_spec, blk2(D_SSD), blk2(D_SSD), blk2(2 * SSD_STATE),
                  blk2(2 * SSD_STATE), pl.BlockSpec((1, D_SSD), lambda i: (0, 0))],
        out_specs=[st_spec, blk2(D_SSD)],
        out_shape=(sds(nb, SSD_HEADS, SSD_HEAD_DIM, SSD_STATE), sds(nb, D_SSD)),
        compiler_params=_cp("arbitrary"), name="sample_ssd",
    )(dec[:, :SSD_HEADS], st_ssd, xdt, xs, bm, cm, drep)

    x1 = pl.pallas_call(
        _sample_mix_kernel, out_shape=sds(nb, D_MODEL),
        compiler_params=_cp(), name="sample_mix",
    )(xs2, y5, ys, z, wglut, b_glu.reshape(1, D_S5), s5_norm_w.reshape(1, D_S5), snw, w_out_b)
    to_seq_major = lambda a: jnp.transpose(a, (2, 0, 1))
    return x1, to_seq_major(n_re_t), to_seq_major(n_im_t), hn_ssd, to_seq_major(nconv_t)


def kernel(x_prompt, x_sample, state_s5_re, state_s5_im, state_ssd, state_conv, norm_mix_w, w_in, s5_lam_re, s5_lam_im, s5_log_dt, s5_b_re, s5_b_im, s5_c_re, s5_c_im, s5_d, s5_w_glu, s5_b_glu, s5_norm_w, ssd_conv_w, ssd_conv_b, ssd_a_log, ssd_dt_bias, ssd_d, ssd_norm_w, w_out, norm_ffn_w, w_ff1, w_ff2, norm_final_w):
    P = S5_STATE
    w_int = jnp.transpose(w_in[0]).astype(BF16)
    wglut = jnp.transpose(s5_w_glu[0]).astype(BF16)
    w_out_b = w_out[0].astype(BF16)
    w1 = w_ff1[0].astype(BF16)
    w2 = w_ff2[0].astype(BF16)
    pad_h = lambda v: jnp.pad(v.reshape(1, SSD_HEADS), ((0, 0), (0, LANES - SSD_HEADS)))
    dtb, alog = pad_h(ssd_dt_bias[0]), pad_h(ssd_a_log[0])
    drep = jnp.repeat(ssd_d[0], SSD_HEAD_DIM).reshape(1, D_SSD)
    snw = ssd_norm_w[0].reshape(1, D_SSD)
    conv_w, conv_b = ssd_conv_w[0], ssd_conv_b[0].reshape(1, D_CONV)

    prep = _s5_prep(s5_lam_re[0], s5_lam_im[0], s5_log_dt[0], s5_b_re[0], s5_b_im[0],
                    s5_c_re[0], s5_c_im[0], s5_d[0])

    o5, hfin5 = _s5_prompt(x_prompt, norm_mix_w[0], w_int, prep, wglut, s5_b_glu[0], s5_norm_w[0],
                           w_out_b)
    y_prompt, h_ssd, ctail = _ssd_ffn_prompt(x_prompt, o5, norm_mix_w[0], w_int, conv_w, conv_b,
                                             dtb, alog, drep, snw, w_out_b, norm_ffn_w[0], w1, w2,
                                             norm_final_w)
    hfin5 = jnp.transpose(hfin5, (1, 0, 2))
    np_re, np_im = hfin5[None, :, :, :P], hfin5[None, :, :, P:]
    np_ssd = h_ssd[None]
    np_conv = ctail[None, :, SUBLANES - (SSD_CONV - 1):, :]

    x1s, ns_re, ns_im, hn_ssd, nconv = _sample_layer(
        x_sample, state_s5_re[0], state_s5_im[0], state_ssd[0], state_conv[0], norm_mix_w[0],
        w_int, conv_w, conv_b, dtb, alog, drep, snw, prep, s5_d[0], wglut,
        s5_b_glu[0], s5_norm_w[0], w_out_b)
    y_sample = _ffn(x1s, norm_ffn_w[0], w1, w2, norm_final_w, DEC_BATCH).reshape(DEC_BATCH, 1, D_MODEL)
    ns_re, ns_im = ns_re[None], ns_im[None]
    ns_ssd = hn_ssd[None]
    ns_conv = nconv[None]

    return (y_prompt, y_sample, np_re, np_im, np_ssd, np_conv, ns_re, ns_im, ns_ssd, ns_conv)
```

```python
import functools

import jax
import jax.numpy as jnp
from jax import lax
from jax.experimental import pallas as pl
from jax.experimental.pallas import tpu as pltpu

F32 = jnp.float32
BF16 = jnp.bfloat16
HI = lax.Precision.HIGHEST
EPS = 1e-5

D_MODEL = 1024
BATCH = 8
SEQ = 2048
DEC_BATCH = 128
D_S5 = 1024
S5_CH = 16
S5_GROUPS = 64
S5_STATE = 64
D_SSD = 1024
SSD_HEADS = 16
SSD_HEAD_DIM = 64
SSD_STATE = 128
SSD_CONV = 4
D_CONV = 1536
D_FF = 4096

LANES = 128
SUBLANES = 8
Q5 = 16
NC5 = SEQ // Q5
M5 = NC5 * BATCH
R5 = Q5 * S5_CH
QS = 128
TL = 512
XBC_BLK = 512
TM_FF = 512
SB = 16
GS = 8
GP = 8
GC = 4
VMEM_LIMIT = 56 * 1024 * 1024
VMEM_LIMIT_FUSED = 62 * 1024 * 1024


def _cp(*sem):
    return pltpu.CompilerParams(dimension_semantics=sem, vmem_limit_bytes=VMEM_LIMIT)


def _rms(x, w):
    return x * lax.rsqrt(jnp.mean(x * x, axis=-1, keepdims=True) + EPS) * w


def _dot(a, b, precision=None):
    return jnp.dot(a, b, preferred_element_type=F32, precision=precision)


def _dot_nt(a, b, precision=None):
    return lax.dot_general(a, b, (((1,), (1,)), ((), ())), preferred_element_type=F32,
                           precision=precision)


def _dot_tn(a, b, precision=None):
    return lax.dot_general(a, b, (((0,), (0,)), ((), ())), preferred_element_type=F32,
                           precision=precision)


def _iota(shape, dim):
    return lax.broadcasted_iota(jnp.int32, shape, dim)


def _silu(x):
    return x * jax.nn.sigmoid(x)


def _dot_split(a, b):
    a_hi = a.astype(BF16)
    b_hi = b.astype(BF16)
    a_lo = (a - a_hi.astype(F32)).astype(BF16)
    b_lo = (b - b_hi.astype(F32)).astype(BF16)
    return _dot(a_hi, b_hi) + _dot(a_lo, b_hi) + _dot(a_hi, b_lo)


def _rows_x16(x):
    return jnp.broadcast_to(x.reshape(Q5, 1, LANES), (Q5, S5_CH, LANES)).reshape(R5, LANES)


def _tile_x32(x):
    return jnp.broadcast_to(x[None], (Q5, S5_CH, LANES)).reshape(R5, LANES)


def _s5_prep_kernel(lre_ref, lim_ref, ldt_ref, cre_ref, cim_ref, d_ref, bre_ref, bim_ref,
                    ws_ref, t_ref, wo_ref, aq_ref, a1c_ref, bst_ref, cneg_ref, dcb_ref):
    dup = lambda a: jnp.concatenate([a, a], axis=1)
    for i in range(GP):
        c_re, c_im = cre_ref[i], cim_ref[i]
        drow = jnp.concatenate([jnp.zeros((1, R5 - S5_CH), F32), d_ref[i:i + 1, :]], axis=1)
        _s5_prep_group(dup(lre_ref[i:i + 1, :]), dup(lim_ref[i:i + 1, :]), ldt_ref[i:i + 1, :],
                       jnp.concatenate([c_re, c_im], axis=1), dup(c_re), dup(c_im), drow,
                       dup(jnp.transpose(bre_ref[i])), dup(jnp.transpose(bim_ref[i])),
                       ws_ref.at[i], t_ref.at[i], wo_ref.at[i], aq_ref.at[i], a1c_ref.at[i],
                       bst_ref.at[i], cneg_ref.at[i], dcb_ref.at[i])


def _col_bcast(row):
    return jnp.transpose(jnp.broadcast_to(row, (LANES, LANES)))


def _s5_prep_group(lr2, li2, ldt, ccat, c2re, c2im, drow, bt2re, bt2im,
                   ws_ref, t_ref, wo_ref, aq_ref, a1c_ref, bst_ref, cneg_ref, dcb_ref):
    dt = jnp.exp(ldt)

    def zoh(lr, li):
        mag = jnp.exp(lr * dt)
        ab_re = mag * jnp.cos(li * dt)
        ab_im = mag * jnp.sin(li * dt)
        den = lr * lr + li * li
        nr, ni = ab_re - 1.0, ab_im
        return (nr * lr + ni * li) / den, (ni * lr - nr * li) / den

    def powers(lr, li, e):
        mag = jnp.exp(e * (lr * dt))
        ang = e * (li * dt)
        return mag * jnp.cos(ang), mag * jnp.sin(ang)

    lane1 = _iota((1, LANES), 1) < S5_STATE
    lo16 = _iota((S5_CH, LANES), 1) < S5_STATE
    f_re, f_im = zoh(lr2, li2)
    bb_a = f_re * bt2re - f_im * bt2im
    bb_b = f_re * bt2im + f_im * bt2re
    bbt = jnp.where(lo16, bb_a, bb_b)
    bbt_sw = jnp.where(lo16, bb_b, bb_a)

    rev = SUBLANES * ((Q5 + SUBLANES) // SUBLANES)
    r = _iota((rev + Q5, LANES), 0)
    expo = jnp.where(r < rev, r, (rev + Q5 - 1) - r).astype(F32)
    tab_re, tab_im = powers(lr2, li2, expo)
    aq_re, aq_im = tab_re[Q5:Q5 + 1, :], tab_im[Q5:Q5 + 1, :]
    a1_re, a1_im = tab_re[1:2, :], tab_im[1:2, :]
    aq_ref[...] = jnp.concatenate(
        [aq_re, jnp.where(lane1, -aq_im, aq_im), jnp.where(lane1, aq_im, -aq_im),
         jnp.zeros((5, LANES), F32)], axis=0)

    a1c_ref[0] = _col_bcast(a1_re)[0:S5_STATE, :]
    a1c_ref[1] = _col_bcast(a1_im)[0:S5_STATE, :]
    bst_ref[...] = jnp.transpose(
        jnp.concatenate([bbt, jnp.zeros((LANES - S5_CH, LANES), F32)], axis=0))[:, 0:S5_CH]
    cneg_ref[...] = jnp.where(lo16, c2re, -c2im)
    dcb_ref[...] = _col_bcast(drow[:, R5 - LANES:])[LANES - S5_CH:, :]

    ain_re, ain_im = _rows_x16(tab_re[rev:rev + Q5, :]), _rows_x16(tab_im[rev:rev + Q5, :])
    sign = jnp.where(lane1, -1.0, 1.0)
    ws_t = ain_re * _tile_x32(bbt) + (sign * ain_im) * _tile_x32(bbt_sw)
    ws = jnp.transpose(ws_t)
    ws_ref[...] = ws.astype(BF16)

    upper = _iota((LANES, R5), 0) < S5_STATE
    vmat = _dot_split(ccat, jnp.where(upper, ws, -ws))
    vrow = _iota((S5_CH, R5), 0)
    vcol = _iota((S5_CH, R5), 1)
    vmat = vmat + jnp.where(vcol == (Q5 - 1) * S5_CH + vrow, drow, 0.0)
    for l in range(Q5):
        width = (l + 1) * S5_CH
        rolled = vmat if width == R5 else pltpu.roll(vmat, width, axis=1)
        t_ref[l * S5_CH:(l + 1) * S5_CH, :] = jnp.where(vcol < width, rolled, 0.0).astype(BF16)

    aout_re, aout_im = _rows_x16(tab_re[1:Q5 + 1, :]), _rows_x16(tab_im[1:Q5 + 1, :])
    c2re_t, c2im_t = _tile_x32(c2re), _tile_x32(c2im)
    wo_ref[...] = jnp.where(_iota((R5, LANES), 1) < S5_STATE,
                            c2re_t * aout_re - c2im_t * aout_im,
                            -(c2re_t * aout_im + c2im_t * aout_re)).astype(BF16)


def _s5_prep(lam_re, lam_im, log_dt, b_re, b_im, c_re, c_im, d):
    G, P, C = S5_GROUPS, S5_STATE, S5_CH
    args = (lam_re, lam_im, log_dt.reshape(G, 1), c_re, c_im, d.reshape(G, C), b_re, b_im)
    spec = lambda a: pl.BlockSpec((GP,) + a.shape[1:], lambda g, nd=a.ndim: (g,) + (0,) * (nd - 1))
    out_shapes = (
        jax.ShapeDtypeStruct((G, 2 * P, R5), BF16),
        jax.ShapeDtypeStruct((G, R5, R5), BF16),
        jax.ShapeDtypeStruct((G, R5, 2 * P), BF16),
        jax.ShapeDtypeStruct((G, 8, 2 * P), F32),
        jax.ShapeDtypeStruct((G, 2, P, LANES), F32),
        jax.ShapeDtypeStruct((G, 2 * P, C), F32),
        jax.ShapeDtypeStruct((G, C, 2 * P), F32),
        jax.ShapeDtypeStruct((G, C, LANES), F32),
    )
    return pl.pallas_call(
        _s5_prep_kernel, grid=(G // GP,),
        in_specs=[spec(a) for a in args],
        out_specs=[pl.BlockSpec((GP,) + s.shape[1:], lambda g, nd=len(s.shape): (g,) + (0,) * (nd - 1))
                   for s in out_shapes],
        out_shape=out_shapes, compiler_params=_cp("arbitrary"), name="s5_prep",
    )(*args)


def _s5_inproj_kernel(x_ref, nw_ref, wut_ref, ut_ref):
    x = x_ref[...].reshape(M5, D_MODEL)
    hn = _rms(x, nw_ref[...]).astype(BF16)
    ut = _dot_nt(wut_ref[...], hn)
    ut_ref[...] = ut.astype(BF16).reshape(S5_GROUPS, S5_CH, M5)


def _s5_core_kernel(ut_ref, t_ref, ws_ref, wo_ref, aq_ref, y_ref, hfin_ref,
                    hs_scr, sp_scr, sq_scr, yin_scr):
    for i in range(GC):
        ut = ut_ref[i]
        sp = jnp.transpose(_dot(ws_ref[i], ut))
        sp_scr[i] = sp
        sq_scr[i] = pltpu.roll(sp, S5_STATE, axis=1)
        yin_scr[i] = _dot(t_ref[i], ut)
    for i in range(GC):
        a1 = jnp.broadcast_to(aq_ref[i, 0:1, :], (BATCH, LANES))
        a2 = jnp.broadcast_to(aq_ref[i, 1:2, :], (BATCH, LANES))
        a3 = jnp.broadcast_to(aq_ref[i, 2:3, :], (BATCH, LANES))
        hp = jnp.zeros((BATCH, LANES), F32)
        hq = jnp.zeros((BATCH, LANES), F32)
        for k in range(NC5):
            rows = pl.ds(k * BATCH, BATCH)
            hs_scr[i, rows, :] = hp
            hp, hq = (a1 * hp + a2 * hq + sp_scr[i, rows, :],
                      a1 * hq + a3 * hp + sq_scr[i, rows, :])
        hfin_ref[i] = hp
    for i in range(GC):
        y = yin_scr[i] + _dot_nt(wo_ref[i], hs_scr[i].astype(BF16))
        y_ref[i] = y.astype(BF16)


def _s5_out_kernel(g_ref, wglut_ref, bglu_ref, nw_ref, wout_ref, o_ref):
    g = jax.nn.gelu(g_ref[...].astype(F32).reshape(D_S5, M5))
    z = _dot(wglut_ref[...], g.astype(BF16)) + bglu_ref[...]
    out = g * jax.nn.sigmoid(z)
    ms = jnp.mean(out * out, axis=0, keepdims=True)
    y5 = out * lax.rsqrt(ms + EPS) * nw_ref[...]
    o = _dot_tn(y5.astype(BF16), wout_ref[...])
    o_ref[...] = o.reshape(NC5, BATCH, D_MODEL)


def _s5_prompt(x_prompt, norm_mix_w, wut, prep, wglut, b_glu, s5_norm_w, wout5):
    ws, tmat, wo, aq = prep[:4]
    x_v = jnp.transpose(x_prompt, (1, 0, 2)).reshape(NC5, Q5, BATCH, D_MODEL)
    tok_blk = pl.BlockSpec((NC5, None, BATCH, D_MODEL), lambda s: (0, s, 0, 0))
    chan_blk = pl.BlockSpec((S5_GROUPS, S5_CH, M5), lambda s: (0, s, 0))
    const2 = lambda s: (0, 0)
    once = pl.Buffered(1)
    ut = pl.pallas_call(
        _s5_inproj_kernel, grid=(Q5,),
        in_specs=[tok_blk,
                  pl.BlockSpec((1, D_MODEL), const2),
                  pl.BlockSpec((D_S5, D_MODEL), const2, pipeline_mode=once)],
        out_specs=chan_blk,
        out_shape=jax.ShapeDtypeStruct((S5_GROUPS, R5, M5), BF16),
        compiler_params=_cp("arbitrary"), name="s5_inproj",
    )(x_v, norm_mix_w.reshape(1, D_MODEL), wut)

    per_g = lambda shape: pl.BlockSpec((GC,) + shape, lambda g: (g, 0, 0))
    gact, hfin = pl.pallas_call(
        _s5_core_kernel, grid=(S5_GROUPS // GC,),
        in_specs=[per_g((R5, M5)), per_g((R5, R5)), per_g((LANES, R5)), per_g((R5, LANES)),
                  per_g((8, LANES))],
        out_specs=[per_g((R5, M5)), per_g((BATCH, LANES))],
        out_shape=(jax.ShapeDtypeStruct((S5_GROUPS, R5, M5), BF16),
                   jax.ShapeDtypeStruct((S5_GROUPS, BATCH, LANES), F32)),
        scratch_shapes=[pltpu.VMEM((GC, M5, LANES), F32) for _ in range(3)]
        + [pltpu.VMEM((GC, R5, M5), F32)],
        compiler_params=_cp("arbitrary"), name="s5_core",
    )(ut, tmat, ws, wo, aq)

    o5 = pl.pallas_call(
        _s5_out_kernel, grid=(Q5,),
        in_specs=[chan_blk,
                  pl.BlockSpec((D_S5, D_S5), const2, pipeline_mode=once),
                  pl.BlockSpec((D_S5, 1), const2),
                  pl.BlockSpec((D_S5, 1), const2),
                  pl.BlockSpec((D_S5, D_MODEL), const2, pipeline_mode=once)],
        out_specs=tok_blk,
        out_shape=jax.ShapeDtypeStruct((NC5, Q5, BATCH, D_MODEL), F32),
        compiler_params=_cp("arbitrary"), name="s5_out",
    )(gact, wglut, b_glu.reshape(D_S5, 1), s5_norm_w.reshape(D_S5, 1), wout5)
    return o5.reshape(SEQ, BATCH * D_MODEL), hfin


def _ffn_rows(x, nw_ref, w1_ref, w2_ref, fw_ref):
    hf = _rms(x, nw_ref[...]).astype(BF16)
    acc = x
    blk = D_FF // 4
    for c in range(4):
        h1 = jnp.square(jnp.maximum(_dot(hf, w1_ref[:, c * blk:(c + 1) * blk]), 0.0))
        acc = acc + _dot(h1.astype(BF16), w2_ref[c * blk:(c + 1) * blk, :])
    return _rms(acc, fw_ref[...])


def _ssd_ffn_kernel(x_ref, o5_ref, nw_ref, wz_ref, wxbc0_ref, wxbc1_ref, wxbc2_ref, wdt_ref,
                    cw_ref, cb_ref, dtb_ref, alog_ref, drep_ref, snw_ref, wout_ref,
                    fnw_ref, w1_ref, w2_ref, fw_ref,
                    y_ref, hout_ref, cout_ref,
                    xp_scr, xs_scr, b_scr, c_scr, z_scr, a_scr, dt_scr, y_scr, h_scr, x1_scr,
                    hf_scr, h1_scr, acc_scr):
    wxbc_refs = (wxbc0_ref, wxbc1_ref, wxbc2_ref)
    t = pl.program_id(0)
    n_tiles = pl.num_programs(0) - 1
    j = lax.rem(t, SEQ // TL)

    @pl.when(t == 0)
    def _():
        x1_scr[...] = jnp.zeros(x1_scr.shape, F32)
        hf_scr[...] = jnp.zeros(hf_scr.shape, BF16)

    @pl.when(j == 0)
    def _():
        xp_scr[0:SUBLANES, :] = jnp.zeros((SUBLANES, D_CONV), F32)
        h_scr[...] = jnp.zeros(h_scr.shape, F32)

    blk = D_FF // 4
    sl = blk // 4

    def ffn_up(c, q):
        cols = slice(c * blk + q * sl, c * blk + (q + 1) * sl)
        h1 = jnp.square(jnp.maximum(_dot(hf_scr[...], w1_ref[:, cols]), 0.0))
        h1_scr[:, q * sl:(q + 1) * sl] = h1.astype(BF16)

    def ffn_down(c, q):
        cols = slice(q * sl, (q + 1) * sl)
        acc_scr[:, cols] += _dot(h1_scr[...], w2_ref[c * blk:(c + 1) * blk, cols])

    ffn_slices = iter([functools.partial(f, c, q) for c in range(4)
                       for f in (ffn_up, ffn_down) for q in range(4)])

    def ffn_step(n=1):
        for _ in range(n):
            next(ffn_slices)()

    ffn_step(2)

    x = x_ref[...]
    hn = _rms(x, nw_ref[...]).astype(BF16)
    acc_scr[...] = x1_scr[...]

    def conv_block(i):
        cols = slice(i * XBC_BLK, (i + 1) * XBC_BLK)
        xp_scr[SUBLANES:, cols] = _dot_nt(hn, wxbc_refs[i][...])
        conv = cb_ref[:, cols] + sum(
            xp_scr[SUBLANES - (SSD_CONV - 1) + k:SUBLANES - (SSD_CONV - 1) + k + TL, cols]
            * cw_ref[k:k + 1, cols] for k in range(SSD_CONV))
        tail = xp_scr[TL:TL + SUBLANES, cols]
        xp_scr[0:SUBLANES, cols] = tail
        cout_ref[:, cols] = tail
        return _silu(conv)

    xs_scr[:, 0:XBC_BLK] = conv_block(0)
    xs_scr[:, XBC_BLK:2 * XBC_BLK] = conv_block(1)
    bc = conv_block(2)
    b_scr[...] = bc[:, :2 * SSD_STATE]
    c_scr[...] = bc[:, 2 * SSD_STATE:]
    z_scr[...] = _dot_nt(hn, wz_ref[...])
    wdt = jnp.concatenate([wdt_ref[...], jnp.zeros((LANES - SSD_HEADS, D_MODEL), BF16)], axis=0)
    dt = jax.nn.softplus(_dot_nt(hn, wdt) + dtb_ref[...])
    dt_scr[...] = dt
    a_scr[...] = dt * (-jnp.exp(alog_ref[...]))
    ffn_step(5)

    li = _iota((QS, QS), 0)
    si = _iota((QS, QS), 1)
    causal = li >= si
    tril = causal.astype(F32)
    lo = si < SSD_HEAD_DIM

    def chunk(ci, carry):
        rows = pl.ds(ci * QS, QS)
        acum = _dot(tril, a_scr[rows, :], HI)
        acum_t = jnp.transpose(acum)
        dt_t = jnp.transpose(dt_scr[rows, :])
        alast = acum[QS - 1:QS, :]
        alast_t = acum_t[:, QS - 1:QS]
        for g in range(2):
            bg = b_scr[rows, g * SSD_STATE:(g + 1) * SSD_STATE]
            cg_b = c_scr[rows, g * SSD_STATE:(g + 1) * SSD_STATE].astype(BF16)
            bt = jnp.transpose(bg)
            cb = _dot(cg_b, bt.astype(BF16))
            h_grp = jnp.concatenate([h_scr[4 * g + i] for i in range(4)], axis=1)
            y_off = _dot(cg_b, h_grp.astype(BF16))
            for hp in range(4):
                pr = 4 * g + hp
                cols = slice(pr * LANES, (pr + 1) * LANES)
                xs_pair = xs_scr[rows, cols]
                xs_b = xs_pair.astype(BF16)
                zero = jnp.zeros_like(xs_b)
                xs_half = (jnp.where(lo, xs_b, zero), jnp.where(lo, zero, xs_b))
                acols, y, st = [], None, None
                for par, h in enumerate((2 * pr, 2 * pr + 1)):
                    acol = jnp.broadcast_to(acum[:, h:h + 1], (QS, QS))
                    arow = acum_t[h:h + 1, :]
                    dtrow = dt_t[h:h + 1, :]
                    lmat = jnp.exp(jnp.where(causal, acol - arow, -1e30))
                    m = (cb * lmat * dtrow).astype(BF16)
                    bts = (bt * (jnp.exp(alast_t[h:h + 1, :] - arow) * dtrow)).astype(BF16)
                    yd = _dot(m, xs_half[par])
                    sd = _dot(bts, xs_half[par])
                    y = yd if y is None else y + yd
                    st = sd if st is None else st + sd
                    acols.append(acol)
                h0, h1 = 2 * pr, 2 * pr + 1
                decay = jnp.exp(jnp.where(lo[0:1, :], alast[:, h0:h0 + 1], alast[:, h1:h1 + 1]))
                h_scr[pr] = decay * h_scr[pr] + st
                y = (y + jnp.exp(jnp.where(lo, acols[0], acols[1])) * y_off[:, hp * LANES:(hp + 1) * LANES]
                     + drep_ref[:, cols] * xs_pair)
                y_scr[rows, cols] = y * _silu(z_scr[rows, cols])
                if hp != 3 or (ci == 0 and g == 0):
                    ffn_step()
        half = D_SSD // 2
        for g in range(2):
            yg = y_scr[rows, g * half:(g + 1) * half]
            yg = yg * lax.rsqrt(jnp.mean(yg * yg, axis=-1, keepdims=True) + EPS)
            y_scr[rows, g * half:(g + 1) * half] = yg * snw_ref[:, g * half:(g + 1) * half]
        return carry

    for ci in range(TL // QS):
        chunk(ci, 0)
    assert next(ffn_slices, None) is None, "every FFN slice must have been issued"
    y_ref[...] = _rms(acc_scr[...], fw_ref[...])
    for hrows in (slice(0, TL // 2), slice(TL // 2, TL)):
        x1 = x[hrows] + o5_ref[hrows, :] + _dot(y_scr[hrows, :].astype(BF16), wout_ref[...])
        x1_scr[hrows, :] = x1
        hf_scr[hrows, :] = _rms(x1, fnw_ref[...]).astype(BF16)

    @pl.when(jnp.logical_and(j == SEQ // TL - 1, t < n_tiles))
    def _():
        for pr in range(SSD_HEADS // 2):
            hout_ref[2 * pr:2 * pr + 2] = jnp.transpose(h_scr[pr]).reshape(2, SSD_HEAD_DIM, SSD_STATE)


def _ssd_ffn_prompt(x_prompt, o5, norm_mix_w, w_int, conv_w, conv_b, dtb, alog, drep, snw,
                    w_out_b, norm_ffn_w, w1, w2, norm_final_w):
    nj = SEQ // TL
    n_tiles = BATCH * nj
    c2 = lambda t: (0, 0)
    once = pl.Buffered(1)
    row = lambda n: pl.BlockSpec((1, n), c2, pipeline_mode=once)
    mat = lambda r, c: pl.BlockSpec((r, c), c2, pipeline_mode=once)
    wrows = lambda n, i: pl.BlockSpec((n, D_MODEL), lambda t: (i, 0), pipeline_mode=once)
    xbc0 = (D_S5 + D_SSD) // XBC_BLK

    def tile(t):
        tt = jnp.minimum(t, n_tiles - 1)
        return tt // nj, lax.rem(tt, nj)

    def prev(t):
        tp = jnp.maximum(t - 1, 0)
        return tp // nj, lax.rem(tp, nj)

    return pl.pallas_call(
        _ssd_ffn_kernel, grid=(n_tiles + 1,),
        in_specs=[pl.BlockSpec((None, TL, D_MODEL), lambda t: (*tile(t), 0)),
                  pl.BlockSpec((TL, D_MODEL), lambda t: tile(t)[::-1]),
                  row(D_MODEL),
                  wrows(D_SSD, D_S5 // D_SSD),
                  wrows(XBC_BLK, xbc0), wrows(XBC_BLK, xbc0 + 1), wrows(XBC_BLK, xbc0 + 2),
                  wrows(SSD_HEADS, (D_S5 + D_SSD + D_CONV) // SSD_HEADS),
                  mat(SSD_CONV, D_CONV), row(D_CONV),
                  row(LANES), row(LANES), row(D_SSD), row(D_SSD),
                  wrows(D_SSD, D_S5 // D_SSD),
                  row(D_MODEL), mat(D_MODEL, D_FF), mat(D_FF, D_MODEL), row(D_MODEL)],
        out_specs=[pl.BlockSpec((None, TL, D_MODEL), lambda t: (*prev(t), 0)),
                   pl.BlockSpec((None, SSD_HEADS, SSD_HEAD_DIM, SSD_STATE), lambda t: (tile(t)[0], 0, 0, 0)),
                   pl.BlockSpec((None, SUBLANES, D_CONV), lambda t: (tile(t)[0], 0, 0))],
        out_shape=(jax.ShapeDtypeStruct((BATCH, SEQ, D_MODEL), F32),
                   jax.ShapeDtypeStruct((BATCH, SSD_HEADS, SSD_HEAD_DIM, SSD_STATE), F32),
                   jax.ShapeDtypeStruct((BATCH, SUBLANES, D_CONV), F32)),
        scratch_shapes=[pltpu.VMEM((SUBLANES + TL, D_CONV), F32),
                        pltpu.VMEM((TL, D_SSD), F32),
                        pltpu.VMEM((TL, 2 * SSD_STATE), F32),
                        pltpu.VMEM((TL, 2 * SSD_STATE), F32),
                        pltpu.VMEM((TL, D_SSD), F32),
                        pltpu.VMEM((TL, LANES), F32),
                        pltpu.VMEM((TL, LANES), F32),
                        pltpu.VMEM((TL, D_SSD), F32),
                        pltpu.VMEM((SSD_HEADS // 2, SSD_STATE, LANES), F32),
                        pltpu.VMEM((TL, D_MODEL), F32),
                        pltpu.VMEM((TL, D_MODEL), BF16),
                        pltpu.VMEM((TL, D_FF // 4), BF16),
                        pltpu.VMEM((TL, D_MODEL), F32)],
        compiler_params=pltpu.CompilerParams(dimension_semantics=("arbitrary",),
                                             vmem_limit_bytes=VMEM_LIMIT_FUSED),
        name="ssd_ffn",
    )(x_prompt, o5, norm_mix_w.reshape(1, D_MODEL), w_int, w_int, w_int, w_int, w_int,
      conv_w, conv_b, dtb, alog, drep, snw, w_out_b,
      norm_ffn_w.reshape(1, D_MODEL), w1, w2, norm_final_w.reshape(1, D_MODEL))


def _ffn_kernel(x_ref, nw_ref, w1_ref, w2_ref, fw_ref, o_ref):
    o_ref[...] = _ffn_rows(x_ref[...], nw_ref, w1_ref, w2_ref, fw_ref)


def _ffn(x1, norm_ffn_w, w1, w2, norm_final_w, tm):
    n = x1.shape[0]
    c2 = lambda i: (0, 0)
    return pl.pallas_call(
        _ffn_kernel, grid=(n // tm,),
        in_specs=[pl.BlockSpec((tm, D_MODEL), lambda i: (i, 0)),
                  pl.BlockSpec((1, D_MODEL), c2),
                  pl.BlockSpec((D_MODEL, D_FF), c2, pipeline_mode=pl.Buffered(1)),
                  pl.BlockSpec((D_FF, D_MODEL), c2, pipeline_mode=pl.Buffered(1)),
                  pl.BlockSpec((1, D_MODEL), c2)],
        out_specs=pl.BlockSpec((tm, D_MODEL), lambda i: (i, 0)),
        out_shape=jax.ShapeDtypeStruct((n, D_MODEL), F32),
        compiler_params=_cp("arbitrary"), name="ffn",
    )(x1, norm_ffn_w.reshape(1, D_MODEL), w1, w2, norm_final_w.reshape(1, D_MODEL))


def _sample_inproj_kernel(x_ref, nw_ref, win_ref, cw_ref, cb_ref, cbuf_ref, dtb_ref, alog_ref,
                          u_ref, z_ref, xs_ref, xdt_ref, b_ref, c_ref, dec_ref, nconv_ref):
    hn = _rms(x_ref[...], nw_ref[...]).astype(BF16)
    o_z, o_xbc, o_dt = D_S5, D_S5 + D_SSD, D_S5 + D_SSD + D_CONV
    u_ref[...] = _dot_nt(win_ref[0:o_z, :], hn)
    z_ref[...] = _dot_nt(hn, win_ref[o_z:o_xbc, :])
    xbc_t = _dot_nt(win_ref[o_xbc:o_dt, :], hn)
    wdt = jnp.concatenate([win_ref[o_dt:o_dt + SSD_HEADS, :],
                           jnp.zeros((LANES - SSD_HEADS, D_MODEL), BF16)], axis=0)
    dt = jax.nn.softplus(_dot_nt(hn, wdt) + dtb_ref[...])
    dec_ref[...] = jnp.exp(dt * (-jnp.exp(alog_ref[...])))
    head_of_col = lax.shift_right_logical(_iota((LANES, D_SSD), 1), 6)
    dt_rep = _dot(dt, (_iota((LANES, D_SSD), 0) == head_of_col).astype(F32), HI)
    cw_t = jnp.transpose(jnp.concatenate(
        [cw_ref[...], cb_ref[...], jnp.zeros((LANES - SSD_CONV - 1, D_CONV), F32)], axis=0))
    conv_t = cw_t[:, SSD_CONV:SSD_CONV + 1] + xbc_t * cw_t[:, SSD_CONV - 1:SSD_CONV]
    for k in range(SSD_CONV - 1):
        conv_t = conv_t + cbuf_ref[k] * cw_t[:, k:k + 1]
        if k > 0:
            nconv_ref[k - 1] = cbuf_ref[k]
    nconv_ref[SSD_CONV - 2] = xbc_t
    conv = jnp.transpose(_silu(conv_t))
    xs_ref[...] = conv[:, :D_SSD]
    xdt_ref[...] = conv[:, :D_SSD] * dt_rep
    b_ref[...] = conv[:, D_SSD:D_SSD + 2 * SSD_STATE]
    c_ref[...] = conv[:, D_SSD + 2 * SSD_STATE:]


def _sample_s5_kernel(u_ref, hre_ref, him_ref, a1c_ref, bst_ref, cneg_ref, dcb_ref,
                      nre_ref, nim_ref, y_ref):
    for i in range(GS):
        u = u_ref[i]
        bu = _dot(bst_ref[i], u, HI)
        a_re, a_im = a1c_ref[i, 0], a1c_ref[i, 1]
        h_re, h_im = hre_ref[i], him_ref[i]
        n_re = a_re * h_re - a_im * h_im + bu[0:S5_STATE]
        n_im = a_re * h_im + a_im * h_re + bu[S5_STATE:]
        nre_ref[i] = n_re
        nim_ref[i] = n_im
        y_ref[i] = _dot(cneg_ref[i], jnp.concatenate([n_re, n_im], axis=0), HI) + dcb_ref[i] * u


def _sample_ssd_kernel(dec_ref, h0_ref, xdt_ref, xs_ref, b_ref, c_ref, drep_ref, hn_ref, y_ref):
    blk = pl.program_id(0)
    hpg = SSD_HEADS // 2
    half = D_SSD // 2
    xt = jnp.transpose(jnp.concatenate(
        [xdt_ref[...], jnp.zeros((LANES - SB, D_SSD), F32)], axis=0))
    c_b = c_ref[...].astype(BF16)
    rowi = _iota((SB, half), 0)
    ys = [jnp.zeros((SB, half), F32) for _ in range(2)]
    for jj in range(SB):
        seq = blk * SB + jj
        for g in range(2):
            brow = b_ref[jj:jj + 1, g * SSD_STATE:(g + 1) * SSD_STATE]
            parts = []
            for h in range(g * hpg, (g + 1) * hpg):
                xcol = xt[h * SSD_HEAD_DIM:(h + 1) * SSD_HEAD_DIM, jj:jj + 1]
                hn = dec_ref[seq, h] * h0_ref[jj, h] + xcol * brow
                hn_ref[jj, h] = hn
                parts.append(hn.astype(BF16))
            y_all = _dot_nt(c_b[:, g * SSD_STATE:(g + 1) * SSD_STATE], jnp.concatenate(parts, axis=0))
            ys[g] = jnp.where(rowi == jj, y_all, ys[g])
    y_ref[...] = jnp.concatenate(ys, axis=1) + drep_ref[...] * xs_ref[...]


def _sample_mix_kernel(x_ref, y5_ref, ys_ref, z_ref, wglut_ref, bglu_ref, nw5_ref, snw_ref,
                       wout_ref, x1_ref):
    g = jax.nn.gelu(jnp.transpose(y5_ref[...]))
    out = g * jax.nn.sigmoid(_dot_nt(g.astype(BF16), wglut_ref[...]) + bglu_ref[...])
    y5 = _rms(out, nw5_ref[...])
    y = ys_ref[...] * _silu(z_ref[...])
    half = D_SSD // 2
    yn = jnp.concatenate(
        [_rms(y[:, i * half:(i + 1) * half], snw_ref[:, i * half:(i + 1) * half]) for i in range(2)],
        axis=1)
    x1_ref[...] = (x_ref[...] + _dot(y5.astype(BF16), wout_ref[0:D_S5, :])
                   + _dot(yn.astype(BF16), wout_ref[D_S5:, :]))


def _sample_layer(x_sample, st_re, st_im, st_ssd, st_conv, norm_mix_w, w_int,
                  conv_w, conv_b, dtb, alog, drep, snw, prep, s5_d, wglut, b_glu, s5_norm_w,
                  w_out_b):
    nb = DEC_BATCH
    a1c, bst, cneg, dcb = prep[4:8]
    xs2 = x_sample.reshape(nb, D_MODEL)
    sds = lambda *s: jax.ShapeDtypeStruct(s, F32)
    u_t, z, xs, xdt, bm, cm, dec, nconv_t = pl.pallas_call(
        _sample_inproj_kernel,
        out_shape=(sds(D_S5, nb), sds(nb, D_SSD), sds(nb, D_SSD), sds(nb, D_SSD),
                   sds(nb, 2 * SSD_STATE), sds(nb, 2 * SSD_STATE), sds(nb, LANES),
                   sds(SSD_CONV - 1, D_CONV, nb)),
        compiler_params=_cp(), name="sample_inproj",
    )(xs2, norm_mix_w.reshape(1, D_MODEL), w_int, conv_w, conv_b,
      jnp.transpose(st_conv, (1, 2, 0)), dtb, alog)

    def per_g(shape):
        nd = len(shape)
        return pl.BlockSpec((GS,) + shape, lambda g: (g,) + (0,) * nd)

    P = S5_STATE
    n_re_t, n_im_t, y5_t = pl.pallas_call(
        _sample_s5_kernel, grid=(S5_GROUPS // GS,),
        in_specs=[per_g((S5_CH, nb)), per_g((P, nb)), per_g((P, nb)), per_g((2, P, LANES)),
                  per_g((2 * P, S5_CH)), per_g((S5_CH, 2 * P)), per_g((S5_CH, LANES))],
        out_specs=[per_g((P, nb)), per_g((P, nb)), per_g((S5_CH, nb))],
        out_shape=(sds(S5_GROUPS, P, nb), sds(S5_GROUPS, P, nb), sds(S5_GROUPS, S5_CH, nb)),
        compiler_params=_cp("arbitrary"), name="sample_s5",
    )(u_t.reshape(S5_GROUPS, S5_CH, nb), jnp.transpose(st_re, (1, 2, 0)),
      jnp.transpose(st_im, (1, 2, 0)), a1c, bst, cneg, dcb)
    y5 = y5_t.reshape(D_S5, nb)

    smem = pl.BlockSpec(memory_space=pltpu.SMEM)
    blk2 = lambda n: pl.BlockSpec((SB, n), lambda i: (i, 0))
    st_spec = pl.BlockSpec((SB, SSD_HEADS, SSD_HEAD_DIM, SSD_STATE), lambda i: (i, 0, 0, 0))
    hn_ssd, ys = pl.pallas_call(
        _sample_ssd_kernel, grid=(nb // SB,),
        in_specs=[smem, st_spec, blk2(D_SSD), blk2(D_SSD), blk2(2 * SSD_STATE),
                  blk2(2 * SSD_STATE), pl.BlockSpec((1, D_SSD), lambda i: (0, 0))],
        out_specs=[st_spec, blk2(D_SSD)],
        out_shape=(sds(nb, SSD_HEADS, SSD_HEAD_DIM, SSD_STATE), sds(nb, D_SSD)),
        compiler_params=_cp("arbitrary"), name="sample_ssd",
    )(dec[:, :SSD_HEADS], st_ssd, xdt, xs, bm, cm, drep)

    x1 = pl.pallas_call(
        _sample_mix_kernel, out_shape=sds(nb, D_MODEL),
        compiler_params=_cp(), name="sample_mix",
    )(xs2, y5, ys, z, wglut, b_glu.reshape(1, D_S5), s5_norm_w.reshape(1, D_S5), snw, w_out_b)
    to_seq_major = lambda a: jnp.transpose(a, (2, 0, 1))
    return x1, to_seq_major(n_re_t), to_seq_major(n_im_t), hn_ssd, to_seq_major(nconv_t)


def kernel(x_prompt, x_sample, state_s5_re, state_s5_im, state_ssd, state_conv, norm_mix_w, w_in, s5_lam_re, s5_lam_im, s5_log_dt, s5_b_re, s5_b_im, s5_c_re, s5_c_im, s5_d, s5_w_glu, s5_b_glu, s5_norm_w, ssd_conv_w, ssd_conv_b, ssd_a_log, ssd_dt_bias, ssd_d, ssd_norm_w, w_out, norm_ffn_w, w_ff1, w_ff2, norm_final_w):
    P = S5_STATE
    w_int = jnp.transpose(w_in[0]).astype(BF16)
    wglut = jnp.transpose(s5_w_glu[0]).astype(BF16)
    w_out_b = w_out[0].astype(BF16)
    w1 = w_ff1[0].astype(BF16)
    w2 = w_ff2[0].astype(BF16)
    pad_h = lambda v: jnp.pad(v.reshape(1, SSD_HEADS), ((0, 0), (0, LANES - SSD_HEADS)))
    dtb, alog = pad_h(ssd_dt_bias[0]), pad_h(ssd_a_log[0])
    drep = jnp.repeat(ssd_d[0], SSD_HEAD_DIM).reshape(1, D_SSD)
    snw = ssd_norm_w[0].reshape(1, D_SSD)
    conv_w, conv_b = ssd_conv_w[0], ssd_conv_b[0].reshape(1, D_CONV)

    prep = _s5_prep(s5_lam_re[0], s5_lam_im[0], s5_log_dt[0], s5_b_re[0], s5_b_im[0],
                    s5_c_re[0], s5_c_im[0], s5_d[0])

    o5, hfin5 = _s5_prompt(x_prompt, norm_mix_w[0], w_int, prep, wglut, s5_b_glu[0], s5_norm_w[0],
                           w_out_b)
    y_prompt, h_ssd, ctail = _ssd_ffn_prompt(x_prompt, o5, norm_mix_w[0], w_int, conv_w, conv_b,
                                             dtb, alog, drep, snw, w_out_b, norm_ffn_w[0], w1, w2,
                                             norm_final_w)
    hfin5 = jnp.transpose(hfin5, (1, 0, 2))
    np_re, np_im = hfin5[None, :, :, :P], hfin5[None, :, :, P:]
    np_ssd = h_ssd[None]
    np_conv = ctail[None, :, SUBLANES - (SSD_CONV - 1):, :]

    x1s, ns_re, ns_im, hn_ssd, nconv = _sample_layer(
        x_sample, state_s5_re[0], state_s5_im[0], state_ssd[0], state_conv[0], norm_mix_w[0],
        w_int, conv_w, conv_b, dtb, alog, drep, snw, prep, s5_d[0], wglut,
        s5_b_glu[0], s5_norm_w[0], w_out_b)
    y_sample = _ffn(x1s, norm_ffn_w[0], w1, w2, norm_final_w, DEC_BATCH).reshape(DEC_BATCH, 1, D_MODEL)
    ns_re, ns_im = ns_re[None], ns_im[None]
    ns_ssd = hn_ssd[None]
    ns_conv = nconv[None]

    return (y_prompt, y_sample, np_re, np_im, np_ssd, np_conv, ns_re, ns_im, ns_ssd, ns_conv)
```

```python
import functools

import jax
import jax.numpy as jnp
from jax import lax
from jax.experimental import pallas as pl
from jax.experimental.pallas import tpu as pltpu

F32 = jnp.float32
BF16 = jnp.bfloat16
HI = lax.Precision.HIGHEST
EPS = 1e-5

D_MODEL = 1024
BATCH = 8
SEQ = 2048
DEC_BATCH = 128
D_S5 = 1024
S5_CH = 16
S5_GROUPS = 64
S5_STATE = 64
D_SSD = 1024
SSD_HEADS = 16
SSD_HEAD_DIM = 64
SSD_STATE = 128
SSD_CONV = 4
D_CONV = 1536
D_FF = 4096

LANES = 128
SUBLANES = 8
Q5 = 16
NC5 = SEQ // Q5
M5 = NC5 * BATCH
R5 = Q5 * S5_CH
QS = 128
TL = 512
XBC_BLK = 512
SB = 16
GS = 8
GP = 8
GC = 8
VMEM_LIMIT = 56 * 1024 * 1024
VMEM_LIMIT_FUSED = 62 * 1024 * 1024


def _cp(*sem):
    return pltpu.CompilerParams(dimension_semantics=sem, vmem_limit_bytes=VMEM_LIMIT)


def _rms(x, w):
    return x * lax.rsqrt(jnp.mean(x * x, axis=-1, keepdims=True) + EPS) * w


def _dot(a, b, precision=None):
    return jnp.dot(a, b, preferred_element_type=F32, precision=precision)


def _dot_nt(a, b, precision=None):
    return lax.dot_general(a, b, (((1,), (1,)), ((), ())), preferred_element_type=F32,
                           precision=precision)


def _dot_tn(a, b, precision=None):
    return lax.dot_general(a, b, (((0,), (0,)), ((), ())), preferred_element_type=F32,
                           precision=precision)


def _iota(shape, dim):
    return lax.broadcasted_iota(jnp.int32, shape, dim)


def _silu(x):
    return x * jax.nn.sigmoid(x)


def _dot_split(a, b):
    a_hi = a.astype(BF16)
    b_hi = b.astype(BF16)
    a_lo = (a - a_hi.astype(F32)).astype(BF16)
    b_lo = (b - b_hi.astype(F32)).astype(BF16)
    return _dot(a_hi, b_hi) + _dot(a_lo, b_hi) + _dot(a_hi, b_lo)


def _repeat_channels(x):
    return jnp.broadcast_to(x.reshape(Q5, 1, LANES), (Q5, S5_CH, LANES)).reshape(R5, LANES)


def _tile_steps(x):
    return jnp.broadcast_to(x[None], (Q5, S5_CH, LANES)).reshape(R5, LANES)


def _s5_prep_kernel(lre_ref, lim_ref, ldt_ref, cre_ref, cim_ref, d_ref, bre_ref, bim_ref,
                    ws_ref, t_ref, wo_ref, aq_ref, a1c_ref, bst_ref, cneg_ref, dcb_ref):
    dup = lambda a: jnp.concatenate([a, a], axis=1)
    for i in range(GP):
        c_re, c_im = cre_ref[i], cim_ref[i]
        drow = jnp.concatenate([jnp.zeros((1, R5 - S5_CH), F32), d_ref[i:i + 1, :]], axis=1)
        _s5_prep_group(dup(lre_ref[i:i + 1, :]), dup(lim_ref[i:i + 1, :]), ldt_ref[i:i + 1, :],
                       jnp.concatenate([c_re, c_im], axis=1), dup(c_re), dup(c_im), drow,
                       dup(jnp.transpose(bre_ref[i])), dup(jnp.transpose(bim_ref[i])),
                       ws_ref.at[i], t_ref.at[i], wo_ref.at[i], aq_ref.at[i], a1c_ref.at[i],
                       bst_ref.at[i], cneg_ref.at[i], dcb_ref.at[i])


def _col_bcast(row):
    return jnp.transpose(jnp.broadcast_to(row, (LANES, LANES)))


def _s5_prep_group(lr2, li2, ldt, ccat, c2re, c2im, drow, bt2re, bt2im,
                   ws_ref, t_ref, wo_ref, aq_ref, a1c_ref, bst_ref, cneg_ref, dcb_ref):
    dt = jnp.exp(ldt)

    def zoh(lr, li):
        mag = jnp.exp(lr * dt)
        ab_re = mag * jnp.cos(li * dt)
        ab_im = mag * jnp.sin(li * dt)
        den = lr * lr + li * li
        nr, ni = ab_re - 1.0, ab_im
        return (nr * lr + ni * li) / den, (ni * lr - nr * li) / den

    def powers(lr, li, e):
        mag = jnp.exp(e * (lr * dt))
        ang = e * (li * dt)
        return mag * jnp.cos(ang), mag * jnp.sin(ang)

    lane1 = _iota((1, LANES), 1) < S5_STATE
    lo16 = _iota((S5_CH, LANES), 1) < S5_STATE
    f_re, f_im = zoh(lr2, li2)
    bb_a = f_re * bt2re - f_im * bt2im
    bb_b = f_re * bt2im + f_im * bt2re
    bbt = jnp.where(lo16, bb_a, bb_b)
    bbt_sw = jnp.where(lo16, bb_b, bb_a)

    rev = SUBLANES * ((Q5 + SUBLANES) // SUBLANES)
    r = _iota((rev + Q5, LANES), 0)
    expo = jnp.where(r < rev, r, (rev + Q5 - 1) - r).astype(F32)
    tab_re, tab_im = powers(lr2, li2, expo)
    aq_re, aq_im = tab_re[Q5:Q5 + 1, :], tab_im[Q5:Q5 + 1, :]
    a1_re, a1_im = tab_re[1:2, :], tab_im[1:2, :]
    aq_ref[...] = jnp.concatenate(
        [aq_re, jnp.where(lane1, -aq_im, aq_im), jnp.where(lane1, aq_im, -aq_im),
         jnp.zeros((5, LANES), F32)], axis=0)

    a1c_ref[0] = _col_bcast(a1_re)[0:S5_STATE, :]
    a1c_ref[1] = _col_bcast(a1_im)[0:S5_STATE, :]
    bst_ref[...] = jnp.transpose(
        jnp.concatenate([bbt, jnp.zeros((LANES - S5_CH, LANES), F32)], axis=0))[:, 0:S5_CH]
    cneg_ref[...] = jnp.where(lo16, c2re, -c2im)
    dcb_ref[...] = _col_bcast(drow[:, R5 - LANES:])[LANES - S5_CH:, :]

    ain_re, ain_im = _repeat_channels(tab_re[rev:rev + Q5, :]), _repeat_channels(tab_im[rev:rev + Q5, :])
    sign = jnp.where(lane1, -1.0, 1.0)
    ws_t = ain_re * _tile_steps(bbt) + (sign * ain_im) * _tile_steps(bbt_sw)
    ws = jnp.transpose(ws_t)
    ws_ref[...] = ws.astype(BF16)

    upper = _iota((LANES, R5), 0) < S5_STATE
    vmat = _dot_split(ccat, jnp.where(upper, ws, -ws))
    vrow = _iota((S5_CH, R5), 0)
    vcol = _iota((S5_CH, R5), 1)
    vmat = vmat + jnp.where(vcol == (Q5 - 1) * S5_CH + vrow, drow, 0.0)
    for l in range(Q5):
        width = (l + 1) * S5_CH
        rolled = vmat if width == R5 else pltpu.roll(vmat, width, axis=1)
        t_ref[l * S5_CH:(l + 1) * S5_CH, :] = jnp.where(vcol < width, rolled, 0.0).astype(BF16)

    aout_re, aout_im = _repeat_channels(tab_re[1:Q5 + 1, :]), _repeat_channels(tab_im[1:Q5 + 1, :])
    c2re_t, c2im_t = _tile_steps(c2re), _tile_steps(c2im)
    wo_ref[...] = jnp.where(_iota((R5, LANES), 1) < S5_STATE,
                            c2re_t * aout_re - c2im_t * aout_im,
                            -(c2re_t * aout_im + c2im_t * aout_re)).astype(BF16)


def _s5_prep(lam_re, lam_im, log_dt, b_re, b_im, c_re, c_im, d):
    G, P, C = S5_GROUPS, S5_STATE, S5_CH
    args = (lam_re, lam_im, log_dt.reshape(G, 1), c_re, c_im, d.reshape(G, C), b_re, b_im)
    spec = lambda a: pl.BlockSpec((GP,) + a.shape[1:], lambda g, nd=a.ndim: (g,) + (0,) * (nd - 1))
    out_shapes = (
        jax.ShapeDtypeStruct((G, 2 * P, R5), BF16),
        jax.ShapeDtypeStruct((G, R5, R5), BF16),
        jax.ShapeDtypeStruct((G, R5, 2 * P), BF16),
        jax.ShapeDtypeStruct((G, 8, 2 * P), F32),
        jax.ShapeDtypeStruct((G, 2, P, LANES), F32),
        jax.ShapeDtypeStruct((G, 2 * P, C), F32),
        jax.ShapeDtypeStruct((G, C, 2 * P), F32),
        jax.ShapeDtypeStruct((G, C, LANES), F32),
    )
    return pl.pallas_call(
        _s5_prep_kernel, grid=(G // GP,),
        in_specs=[spec(a) for a in args],
        out_specs=[pl.BlockSpec((GP,) + s.shape[1:], lambda g, nd=len(s.shape): (g,) + (0,) * (nd - 1))
                   for s in out_shapes],
        out_shape=out_shapes, compiler_params=_cp("arbitrary"), name="s5_prep",
    )(*args)


def _s5_inproj_kernel(x_ref, nw_ref, wut_ref, ut_ref):
    x = x_ref[...].reshape(M5, D_MODEL)
    hn = _rms(x, nw_ref[...]).astype(BF16)
    ut = _dot_nt(wut_ref[...], hn)
    ut_ref[...] = ut.astype(BF16).reshape(S5_GROUPS, S5_CH, M5)


def _s5_core_kernel(ut_ref, t_ref, ws_ref, wo_ref, aq_ref, y_ref, hfin_ref,
                    hs_scr, sp_scr, sq_scr, yin_scr):
    for i in range(GC):
        ut = ut_ref[i]
        sp = jnp.transpose(_dot(ws_ref[i], ut))
        sp_scr[i] = sp
        sq_scr[i] = pltpu.roll(sp, S5_STATE, axis=1)
        yin_scr[i] = _dot(t_ref[i], ut)
    for i in range(GC):
        a1 = jnp.broadcast_to(aq_ref[i, 0:1, :], (BATCH, LANES))
        a2 = jnp.broadcast_to(aq_ref[i, 1:2, :], (BATCH, LANES))
        a3 = jnp.broadcast_to(aq_ref[i, 2:3, :], (BATCH, LANES))
        hp = jnp.zeros((BATCH, LANES), F32)
        hq = jnp.zeros((BATCH, LANES), F32)
        for k in range(NC5):
            rows = pl.ds(k * BATCH, BATCH)
            hs_scr[i, rows, :] = hp
            hp, hq = (a1 * hp + a2 * hq + sp_scr[i, rows, :],
                      a1 * hq + a3 * hp + sq_scr[i, rows, :])
        hfin_ref[i] = hp
    for i in range(GC):
        y = yin_scr[i] + _dot_nt(wo_ref[i], hs_scr[i].astype(BF16))
        y_ref[i] = y.astype(BF16)


def _s5_out_kernel(g_ref, wglut_ref, bglu_ref, nw_ref, wout_ref, o_ref):
    g = jax.nn.gelu(g_ref[...].astype(F32).reshape(D_S5, M5))
    z = _dot(wglut_ref[...], g.astype(BF16)) + bglu_ref[...]
    out = g * jax.nn.sigmoid(z)
    ms = jnp.mean(out * out, axis=0, keepdims=True)
    y5 = out * lax.rsqrt(ms + EPS) * nw_ref[...]
    o = _dot_tn(y5.astype(BF16), wout_ref[...])
    o_ref[...] = o.reshape(NC5, BATCH, D_MODEL)


def _s5_prompt(x_prompt, norm_mix_w, w_int, prep, wglut, b_glu, s5_norm_w, w_out_b):
    ws, tmat, wo, aq = prep[:4]
    x_v = jnp.transpose(x_prompt, (1, 0, 2)).reshape(NC5, Q5, BATCH, D_MODEL)
    tok_blk = pl.BlockSpec((NC5, None, BATCH, D_MODEL), lambda s: (0, s, 0, 0))
    chan_blk = pl.BlockSpec((S5_GROUPS, S5_CH, M5), lambda s: (0, s, 0))
    const2 = lambda s: (0, 0)
    once = pl.Buffered(1)
    ut = pl.pallas_call(
        _s5_inproj_kernel, grid=(Q5,),
        in_specs=[tok_blk,
                  pl.BlockSpec((1, D_MODEL), const2),
                  pl.BlockSpec((D_S5, D_MODEL), const2, pipeline_mode=once)],
        out_specs=chan_blk,
        out_shape=jax.ShapeDtypeStruct((S5_GROUPS, R5, M5), BF16),
        compiler_params=_cp("arbitrary"), name="s5_inproj",
    )(x_v, norm_mix_w.reshape(1, D_MODEL), w_int)

    per_g = lambda shape: pl.BlockSpec((GC,) + shape, lambda g: (g, 0, 0))
    y5_pre, hfin = pl.pallas_call(
        _s5_core_kernel, grid=(S5_GROUPS // GC,),
        in_specs=[per_g((R5, M5)), per_g((R5, R5)), per_g((LANES, R5)), per_g((R5, LANES)),
                  per_g((8, LANES))],
        out_specs=[per_g((R5, M5)), per_g((BATCH, LANES))],
        out_shape=(jax.ShapeDtypeStruct((S5_GROUPS, R5, M5), BF16),
                   jax.ShapeDtypeStruct((S5_GROUPS, BATCH, LANES), F32)),
        scratch_shapes=[pltpu.VMEM((GC, M5, LANES), F32) for _ in range(3)]
        + [pltpu.VMEM((GC, R5, M5), F32)],
        compiler_params=_cp("arbitrary"), name="s5_core",
    )(ut, tmat, ws, wo, aq)

    o5 = pl.pallas_call(
        _s5_out_kernel, grid=(Q5,),
        in_specs=[chan_blk,
                  pl.BlockSpec((D_S5, D_S5), const2, pipeline_mode=once),
                  pl.BlockSpec((D_S5, 1), const2),
                  pl.BlockSpec((D_S5, 1), const2),
                  pl.BlockSpec((D_S5, D_MODEL), const2, pipeline_mode=once)],
        out_specs=tok_blk,
        out_shape=jax.ShapeDtypeStruct((NC5, Q5, BATCH, D_MODEL), F32),
        compiler_params=_cp("arbitrary"), name="s5_out",
    )(y5_pre, wglut, b_glu.reshape(D_S5, 1), s5_norm_w.reshape(D_S5, 1), w_out_b)
    return o5.reshape(SEQ, BATCH * D_MODEL), hfin


assert D_SSD == 2 * XBC_BLK and D_CONV == 3 * XBC_BLK and 2 * SSD_STATE * 2 == XBC_BLK


def _ffn_rows(x, nw_ref, w1_ref, w2_ref, fw_ref):
    hf = _rms(x, nw_ref[...]).astype(BF16)
    acc = x
    blk = D_FF // 4
    for c in range(4):
        h1 = jnp.square(jnp.maximum(_dot(hf, w1_ref[:, c * blk:(c + 1) * blk]), 0.0))
        acc = acc + _dot(h1.astype(BF16), w2_ref[c * blk:(c + 1) * blk, :])
    return _rms(acc, fw_ref[...])


def _ssd_ffn_kernel(x_ref, o5_ref, nw_ref, wz_ref, wxbc0_ref, wxbc1_ref, wxbc2_ref, wdt_ref,
                    cw_ref, cb_ref, dtb_ref, alog_ref, drep_ref, snw_ref, wout_ref,
                    fnw_ref, w1_ref, w2_ref, fw_ref,
                    y_ref, hout_ref, cout_ref,
                    xp_scr, xs_scr, b_scr, c_scr, z_scr, a_scr, dt_scr, y_scr, h_scr, x1_scr,
                    hf_scr, h1_scr, acc_scr):
    wxbc_refs = (wxbc0_ref, wxbc1_ref, wxbc2_ref)
    t = pl.program_id(0)
    n_tiles = pl.num_programs(0) - 1
    j = lax.rem(t, SEQ // TL)

    @pl.when(t == 0)
    def _():
        x1_scr[...] = jnp.zeros(x1_scr.shape, F32)
        hf_scr[...] = jnp.zeros(hf_scr.shape, BF16)

    @pl.when(j == 0)
    def _():
        xp_scr[0:SUBLANES, :] = jnp.zeros((SUBLANES, D_CONV), F32)
        h_scr[...] = jnp.zeros(h_scr.shape, F32)

    blk = D_FF // 4
    sl = blk // 4

    def ffn_up(c, q):
        cols = slice(c * blk + q * sl, c * blk + (q + 1) * sl)
        h1 = jnp.square(jnp.maximum(_dot(hf_scr[...], w1_ref[:, cols]), 0.0))
        h1_scr[:, q * sl:(q + 1) * sl] = h1.astype(BF16)

    def ffn_down(c, q):
        cols = slice(q * sl, (q + 1) * sl)
        acc_scr[:, cols] += _dot(h1_scr[...], w2_ref[c * blk:(c + 1) * blk, cols])

    ffn_slices = iter([functools.partial(f, c, q) for c in range(4)
                       for f in (ffn_up, ffn_down) for q in range(4)])

    def ffn_step(n=1):
        for _ in range(n):
            next(ffn_slices)()

    ffn_step(2)

    x = x_ref[...]
    hn = _rms(x, nw_ref[...]).astype(BF16)
    acc_scr[...] = x1_scr[...]

    def conv_block(i):
        cols = slice(i * XBC_BLK, (i + 1) * XBC_BLK)
        xp_scr[SUBLANES:, cols] = _dot_nt(hn, wxbc_refs[i][...])
        conv = cb_ref[:, cols] + sum(
            xp_scr[SUBLANES - (SSD_CONV - 1) + k:SUBLANES - (SSD_CONV - 1) + k + TL, cols]
            * cw_ref[k:k + 1, cols] for k in range(SSD_CONV))
        tail = xp_scr[TL:TL + SUBLANES, cols]
        xp_scr[0:SUBLANES, cols] = tail
        cout_ref[:, cols] = tail
        return _silu(conv)

    xs_scr[:, 0:XBC_BLK] = conv_block(0)
    xs_scr[:, XBC_BLK:2 * XBC_BLK] = conv_block(1)
    bc = conv_block(2)
    b_scr[...] = bc[:, :2 * SSD_STATE]
    c_scr[...] = bc[:, 2 * SSD_STATE:]
    z_scr[...] = _dot_nt(hn, wz_ref[...])
    wdt = jnp.concatenate([wdt_ref[...], jnp.zeros((LANES - SSD_HEADS, D_MODEL), BF16)], axis=0)
    dt = jax.nn.softplus(_dot_nt(hn, wdt) + dtb_ref[...])
    dt_scr[...] = dt
    a_scr[...] = dt * (-jnp.exp(alog_ref[...]))
    ffn_step(5)

    li = _iota((QS, QS), 0)
    si = _iota((QS, QS), 1)
    causal = li >= si
    tril = causal.astype(F32)
    lo = si < SSD_HEAD_DIM

    def chunk(ci, carry):
        rows = pl.ds(ci * QS, QS)
        acum = _dot(tril, a_scr[rows, :], HI)
        acum_t = jnp.transpose(acum)
        dt_t = jnp.transpose(dt_scr[rows, :])
        alast = acum[QS - 1:QS, :]
        alast_t = acum_t[:, QS - 1:QS]
        for g in range(2):
            bg = b_scr[rows, g * SSD_STATE:(g + 1) * SSD_STATE]
            cg_b = c_scr[rows, g * SSD_STATE:(g + 1) * SSD_STATE].astype(BF16)
            bt = jnp.transpose(bg)
            cb = _dot(cg_b, bt.astype(BF16))
            h_grp = jnp.concatenate([h_scr[4 * g + i] for i in range(4)], axis=1)
            y_off = _dot(cg_b, h_grp.astype(BF16))
            for hp in range(4):
                pr = 4 * g + hp
                cols = slice(pr * LANES, (pr + 1) * LANES)
                xs_pair = xs_scr[rows, cols]
                xs_b = xs_pair.astype(BF16)
                zero = jnp.zeros_like(xs_b)
                xs_half = (jnp.where(lo, xs_b, zero), jnp.where(lo, zero, xs_b))
                acols, y, st = [], None, None
                for par, h in enumerate((2 * pr, 2 * pr + 1)):
                    acol = jnp.broadcast_to(acum[:, h:h + 1], (QS, QS))
                    arow = acum_t[h:h + 1, :]
                    dtrow = dt_t[h:h + 1, :]
                    lmat = jnp.exp(jnp.where(causal, acol - arow, -1e30))
                    m = (cb * lmat * dtrow).astype(BF16)
                    bts = (bt * (jnp.exp(alast_t[h:h + 1, :] - arow) * dtrow)).astype(BF16)
                    yd = _dot(m, xs_half[par])
                    sd = _dot(bts, xs_half[par])
                    y = yd if y is None else y + yd
                    st = sd if st is None else st + sd
                    acols.append(acol)
                h0, h1 = 2 * pr, 2 * pr + 1
                decay = jnp.exp(jnp.where(lo[0:1, :], alast[:, h0:h0 + 1], alast[:, h1:h1 + 1]))
                h_scr[pr] = decay * h_scr[pr] + st
                y = (y + jnp.exp(jnp.where(lo, acols[0], acols[1])) * y_off[:, hp * LANES:(hp + 1) * LANES]
                     + drep_ref[:, cols] * xs_pair)
                y_scr[rows, cols] = y * _silu(z_scr[rows, cols])
                if hp != 3 or (ci == 0 and g == 0):
                    ffn_step()
        half = D_SSD // 2
        for g in range(2):
            yg = y_scr[rows, g * half:(g + 1) * half]
            yg = yg * lax.rsqrt(jnp.mean(yg * yg, axis=-1, keepdims=True) + EPS)
            y_scr[rows, g * half:(g + 1) * half] = yg * snw_ref[:, g * half:(g + 1) * half]
        return carry

    for ci in range(TL // QS):
        chunk(ci, 0)
    assert next(ffn_slices, None) is None, "every FFN slice must have been issued"
    y_ref[...] = _rms(acc_scr[...], fw_ref[...])
    for hrows in (slice(0, TL // 2), slice(TL // 2, TL)):
        x1 = x[hrows] + o5_ref[hrows, :] + _dot(y_scr[hrows, :].astype(BF16), wout_ref[...])
        x1_scr[hrows, :] = x1
        hf_scr[hrows, :] = _rms(x1, fnw_ref[...]).astype(BF16)

    @pl.when(jnp.logical_and(j == SEQ // TL - 1, t < n_tiles))
    def _():
        for pr in range(SSD_HEADS // 2):
            hout_ref[2 * pr:2 * pr + 2] = jnp.transpose(h_scr[pr]).reshape(2, SSD_HEAD_DIM, SSD_STATE)


def _ssd_ffn_prompt(x_prompt, o5, norm_mix_w, w_int, conv_w, conv_b, dtb, alog, drep, snw,
                    w_out_b, norm_ffn_w, w1, w2, norm_final_w):
    nj = SEQ // TL
    n_tiles = BATCH * nj
    c2 = lambda t: (0, 0)
    once = pl.Buffered(1)
    row = lambda n: pl.BlockSpec((1, n), c2, pipeline_mode=once)
    mat = lambda r, c: pl.BlockSpec((r, c), c2, pipeline_mode=once)
    wrows = lambda n, i: pl.BlockSpec((n, D_MODEL), lambda t: (i, 0), pipeline_mode=once)
    xbc0 = (D_S5 + D_SSD) // XBC_BLK

    def tile(t):
        tt = jnp.minimum(t, n_tiles - 1)
        return tt // nj, lax.rem(tt, nj)

    def prev(t):
        tp = jnp.maximum(t - 1, 0)
        return tp // nj, lax.rem(tp, nj)

    return pl.pallas_call(
        _ssd_ffn_kernel, grid=(n_tiles + 1,),
        in_specs=[pl.BlockSpec((None, TL, D_MODEL), lambda t: (*tile(t), 0)),
                  pl.BlockSpec((TL, D_MODEL), lambda t: tile(t)[::-1]),
                  row(D_MODEL),
                  wrows(D_SSD, D_S5 // D_SSD),
                  wrows(XBC_BLK, xbc0), wrows(XBC_BLK, xbc0 + 1), wrows(XBC_BLK, xbc0 + 2),
                  wrows(SSD_HEADS, (D_S5 + D_SSD + D_CONV) // SSD_HEADS),
                  mat(SSD_CONV, D_CONV), row(D_CONV),
                  row(LANES), row(LANES), row(D_SSD), row(D_SSD),
                  wrows(D_SSD, D_S5 // D_SSD),
                  row(D_MODEL), mat(D_MODEL, D_FF), mat(D_FF, D_MODEL), row(D_MODEL)],
        out_specs=[pl.BlockSpec((None, TL, D_MODEL), lambda t: (*prev(t), 0)),
                   pl.BlockSpec((None, SSD_HEADS, SSD_HEAD_DIM, SSD_STATE), lambda t: (tile(t)[0], 0, 0, 0)),
                   pl.BlockSpec((None, SUBLANES, D_CONV), lambda t: (tile(t)[0], 0, 0))],
        out_shape=(jax.ShapeDtypeStruct((BATCH, SEQ, D_MODEL), F32),
                   jax.ShapeDtypeStruct((BATCH, SSD_HEADS, SSD_HEAD_DIM, SSD_STATE), F32),
                   jax.ShapeDtypeStruct((BATCH, SUBLANES, D_CONV), F32)),
        scratch_shapes=[pltpu.VMEM((SUBLANES + TL, D_CONV), F32),
                        pltpu.VMEM((TL, D_SSD), F32),
                        pltpu.VMEM((TL, 2 * SSD_STATE), F32),
                        pltpu.VMEM((TL, 2 * SSD_STATE), F32),
                        pltpu.VMEM((TL, D_SSD), F32),
                        pltpu.VMEM((TL, LANES), F32),
                        pltpu.VMEM((TL, LANES), F32),
                        pltpu.VMEM((TL, D_SSD), F32),
                        pltpu.VMEM((SSD_HEADS // 2, SSD_STATE, LANES), F32),
                        pltpu.VMEM((TL, D_MODEL), F32),
                        pltpu.VMEM((TL, D_MODEL), BF16),
                        pltpu.VMEM((TL, D_FF // 4), BF16),
                        pltpu.VMEM((TL, D_MODEL), F32)],
        compiler_params=pltpu.CompilerParams(dimension_semantics=("arbitrary",),
                                             vmem_limit_bytes=VMEM_LIMIT_FUSED),
        name="ssd_ffn",
    )(x_prompt, o5, norm_mix_w.reshape(1, D_MODEL), w_int, w_int, w_int, w_int, w_int,
      conv_w, conv_b, dtb, alog, drep, snw, w_out_b,
      norm_ffn_w.reshape(1, D_MODEL), w1, w2, norm_final_w.reshape(1, D_MODEL))


def _ffn_kernel(x_ref, nw_ref, w1_ref, w2_ref, fw_ref, o_ref):
    o_ref[...] = _ffn_rows(x_ref[...], nw_ref, w1_ref, w2_ref, fw_ref)


def _ffn(x1, norm_ffn_w, w1, w2, norm_final_w, tm):
    n = x1.shape[0]
    c2 = lambda i: (0, 0)
    return pl.pallas_call(
        _ffn_kernel, grid=(n // tm,),
        in_specs=[pl.BlockSpec((tm, D_MODEL), lambda i: (i, 0)),
                  pl.BlockSpec((1, D_MODEL), c2),
                  pl.BlockSpec((D_MODEL, D_FF), c2, pipeline_mode=pl.Buffered(1)),
                  pl.BlockSpec((D_FF, D_MODEL), c2, pipeline_mode=pl.Buffered(1)),
                  pl.BlockSpec((1, D_MODEL), c2)],
        out_specs=pl.BlockSpec((tm, D_MODEL), lambda i: (i, 0)),
        out_shape=jax.ShapeDtypeStruct((n, D_MODEL), F32),
        compiler_params=_cp("arbitrary"), name="ffn",
    )(x1, norm_ffn_w.reshape(1, D_MODEL), w1, w2, norm_final_w.reshape(1, D_MODEL))


def _sample_inproj_kernel(x_ref, nw_ref, win_ref, cw_ref, cb_ref, cbuf_ref, dtb_ref, alog_ref,
                          u_ref, z_ref, xs_ref, xdt_ref, b_ref, c_ref, dec_ref, nconv_ref):
    hn = _rms(x_ref[...], nw_ref[...]).astype(BF16)
    o_z, o_xbc, o_dt = D_S5, D_S5 + D_SSD, D_S5 + D_SSD + D_CONV
    u_ref[...] = _dot_nt(win_ref[0:o_z, :], hn)
    z_ref[...] = _dot_nt(hn, win_ref[o_z:o_xbc, :])
    xbc_t = _dot_nt(win_ref[o_xbc:o_dt, :], hn)
    wdt = jnp.concatenate([win_ref[o_dt:o_dt + SSD_HEADS, :],
                           jnp.zeros((LANES - SSD_HEADS, D_MODEL), BF16)], axis=0)
    dt = jax.nn.softplus(_dot_nt(hn, wdt) + dtb_ref[...])
    dec_ref[...] = jnp.exp(dt * (-jnp.exp(alog_ref[...])))
    head_of_col = lax.shift_right_logical(_iota((LANES, D_SSD), 1), 6)
    dt_rep = _dot(dt, (_iota((LANES, D_SSD), 0) == head_of_col).astype(F32), HI)
    cw_t = jnp.transpose(jnp.concatenate(
        [cw_ref[...], cb_ref[...], jnp.zeros((LANES - SSD_CONV - 1, D_CONV), F32)], axis=0))
    conv_t = cw_t[:, SSD_CONV:SSD_CONV + 1] + xbc_t * cw_t[:, SSD_CONV - 1:SSD_CONV]
    for k in range(SSD_CONV - 1):
        conv_t = conv_t + cbuf_ref[k] * cw_t[:, k:k + 1]
        if k > 0:
            nconv_ref[k - 1] = cbuf_ref[k]
    nconv_ref[SSD_CONV - 2] = xbc_t
    conv = jnp.transpose(_silu(conv_t))
    xs_ref[...] = conv[:, :D_SSD]
    xdt_ref[...] = conv[:, :D_SSD] * dt_rep
    b_ref[...] = conv[:, D_SSD:D_SSD + 2 * SSD_STATE]
    c_ref[...] = conv[:, D_SSD + 2 * SSD_STATE:]


def _sample_s5_kernel(u_ref, hre_ref, him_ref, a1c_ref, bst_ref, cneg_ref, dcb_ref,
                      nre_ref, nim_ref, y_ref):
    for i in range(GS):
        u = u_ref[i]
        bu = _dot(bst_ref[i], u, HI)
        a_re, a_im = a1c_ref[i, 0], a1c_ref[i, 1]
        h_re, h_im = hre_ref[i], him_ref[i]
        n_re = a_re * h_re - a_im * h_im + bu[0:S5_STATE]
        n_im = a_re * h_im + a_im * h_re + bu[S5_STATE:]
        nre_ref[i] = n_re
        nim_ref[i] = n_im
        y_ref[i] = _dot(cneg_ref[i], jnp.concatenate([n_re, n_im], axis=0), HI) + dcb_ref[i] * u


def _sample_ssd_kernel(dec_ref, h0_ref, xdt_ref, xs_ref, b_ref, c_ref, drep_ref, hn_ref, y_ref):
    blk = pl.program_id(0)
    hpg = SSD_HEADS // 2
    half = D_SSD // 2
    xt = jnp.transpose(jnp.concatenate(
        [xdt_ref[...], jnp.zeros((LANES - SB, D_SSD), F32)], axis=0))
    c_b = c_ref[...].astype(BF16)
    rowi = _iota((SB, half), 0)
    ys = [jnp.zeros((SB, half), F32) for _ in range(2)]
    for jj in range(SB):
        seq = blk * SB + jj
        for g in range(2):
            brow = b_ref[jj:jj + 1, g * SSD_STATE:(g + 1) * SSD_STATE]
            parts = []
            for h in range(g * hpg, (g + 1) * hpg):
                xcol = xt[h * SSD_HEAD_DIM:(h + 1) * SSD_HEAD_DIM, jj:jj + 1]
                hn = dec_ref[seq, h] * h0_ref[jj, h] + xcol * brow
                hn_ref[jj, h] = hn
                parts.append(hn.astype(BF16))
            y_all = _dot_nt(c_b[:, g * SSD_STATE:(g + 1) * SSD_STATE], jnp.concatenate(parts, axis=0))
            ys[g] = jnp.where(rowi == jj, y_all, ys[g])
    y_ref[...] = jnp.concatenate(ys, axis=1) + drep_ref[...] * xs_ref[...]


def _sample_mix_kernel(x_ref, y5_ref, ys_ref, z_ref, wglut_ref, bglu_ref, nw5_ref, snw_ref,
                       wout_ref, x1_ref):
    g = jax.nn.gelu(jnp.transpose(y5_ref[...]))
    out = g * jax.nn.sigmoid(_dot_nt(g.astype(BF16), wglut_ref[...]) + bglu_ref[...])
    y5 = _rms(out, nw5_ref[...])
    y = ys_ref[...] * _silu(z_ref[...])
    half = D_SSD // 2
    yn = jnp.concatenate(
        [_rms(y[:, i * half:(i + 1) * half], snw_ref[:, i * half:(i + 1) * half]) for i in range(2)],
        axis=1)
    x1_ref[...] = (x_ref[...] + _dot(y5.astype(BF16), wout_ref[0:D_S5, :])
                   + _dot(yn.astype(BF16), wout_ref[D_S5:, :]))


def _sample_layer(x_sample, st_re, st_im, st_ssd, st_conv, norm_mix_w, w_int,
                  conv_w, conv_b, dtb, alog, drep, snw, prep, wglut, b_glu, s5_norm_w,
                  w_out_b):
    nb = DEC_BATCH
    a1c, bst, cneg, dcb = prep[4:8]
    xs2 = x_sample.reshape(nb, D_MODEL)
    sds = lambda *s: jax.ShapeDtypeStruct(s, F32)
    u_t, z, xs, xdt, bm, cm, dec, nconv_t = pl.pallas_call(
        _sample_inproj_kernel,
        out_shape=(sds(D_S5, nb), sds(nb, D_SSD), sds(nb, D_SSD), sds(nb, D_SSD),
                   sds(nb, 2 * SSD_STATE), sds(nb, 2 * SSD_STATE), sds(nb, LANES),
                   sds(SSD_CONV - 1, D_CONV, nb)),
        compiler_params=_cp(), name="sample_inproj",
    )(xs2, norm_mix_w.reshape(1, D_MODEL), w_int, conv_w, conv_b,
      jnp.transpose(st_conv, (1, 2, 0)), dtb, alog)

    def per_g(shape):
        nd = len(shape)
        return pl.BlockSpec((GS,) + shape, lambda g: (g,) + (0,) * nd)

    P = S5_STATE
    n_re_t, n_im_t, y5_t = pl.pallas_call(
        _sample_s5_kernel, grid=(S5_GROUPS // GS,),
        in_specs=[per_g((S5_CH, nb)), per_g((P, nb)), per_g((P, nb)), per_g((2, P, LANES)),
                  per_g((2 * P, S5_CH)), per_g((S5_CH, 2 * P)), per_g((S5_CH, LANES))],
        out_specs=[per_g((P, nb)), per_g((P, nb)), per_g((S5_CH, nb))],
        out_shape=(sds(S5_GROUPS, P, nb), sds(S5_GROUPS, P, nb), sds(S5_GROUPS, S5_CH, nb)),
        compiler_params=_cp("arbitrary"), name="sample_s5",
    )(u_t.reshape(S5_GROUPS, S5_CH, nb), jnp.transpose(st_re, (1, 2, 0)),
      jnp.transpose(st_im, (1, 2, 0)), a1c, bst, cneg, dcb)
    y5 = y5_t.reshape(D_S5, nb)

    smem = pl.BlockSpec(memory_space=pltpu.SMEM)
    blk2 = lambda n: pl.BlockSpec((SB, n), lambda i: (i, 0))
    st_spec = pl.BlockSpec((SB, SSD_HEADS, SSD_HEAD_DIM, SSD_STATE), lambda i: (i, 0, 0, 0))
    hn_ssd, ys = pl.pallas_call(
        _sample_ssd_kernel, grid=(nb // SB,),
        in_specs=[smem, st_spec, blk2(D_SSD), blk2(D_SSD), blk2(2 * SSD_STATE),
                  blk2(2 * SSD_STATE), pl.BlockSpec((1, D_SSD), lambda i: (0, 0))],
        out_specs=[st_spec, blk2(D_SSD)],
        out_shape=(sds(nb, SSD_HEADS, SSD_HEAD_DIM, SSD_STATE), sds(nb, D_SSD)),
        compiler_params=_cp("arbitrary"), name="sample_ssd",
    )(dec[:, :SSD_HEADS], st_ssd, xdt, xs, bm, cm, drep)

    x1 = pl.pallas_call(
        _sample_mix_kernel, out_shape=sds(nb, D_MODEL),
        compiler_params=_cp(), name="sample_mix",
    )(xs2, y5, ys, z, wglut, b_glu.reshape(1, D_S5), s5_norm_w.reshape(1, D_S5), snw, w_out_b)
    to_seq_major = lambda a: jnp.transpose(a, (2, 0, 1))
    return x1, to_seq_major(n_re_t), to_seq_major(n_im_t), hn_ssd, to_seq_major(nconv_t)


def kernel(x_prompt, x_sample, state_s5_re, state_s5_im, state_ssd, state_conv, norm_mix_w, w_in, s5_lam_re, s5_lam_im, s5_log_dt, s5_b_re, s5_b_im, s5_c_re, s5_c_im, s5_d, s5_w_glu, s5_b_glu, s5_norm_w, ssd_conv_w, ssd_conv_b, ssd_a_log, ssd_dt_bias, ssd_d, ssd_norm_w, w_out, norm_ffn_w, w_ff1, w_ff2, norm_final_w):
    P = S5_STATE
    w_int = jnp.transpose(w_in[0]).astype(BF16)
    wglut = jnp.transpose(s5_w_glu[0]).astype(BF16)
    w_out_b = w_out[0].astype(BF16)
    w1 = w_ff1[0].astype(BF16)
    w2 = w_ff2[0].astype(BF16)
    pad_h = lambda v: jnp.pad(v.reshape(1, SSD_HEADS), ((0, 0), (0, LANES - SSD_HEADS)))
    dtb, alog = pad_h(ssd_dt_bias[0]), pad_h(ssd_a_log[0])
    drep = jnp.repeat(ssd_d[0], SSD_HEAD_DIM).reshape(1, D_SSD)
    snw = ssd_norm_w[0].reshape(1, D_SSD)
    conv_w, conv_b = ssd_conv_w[0], ssd_conv_b[0].reshape(1, D_CONV)

    prep = _s5_prep(s5_lam_re[0], s5_lam_im[0], s5_log_dt[0], s5_b_re[0], s5_b_im[0],
                    s5_c_re[0], s5_c_im[0], s5_d[0])

    o5, hfin5 = _s5_prompt(x_prompt, norm_mix_w[0], w_int, prep, wglut, s5_b_glu[0], s5_norm_w[0],
                           w_out_b)
    y_prompt, h_ssd, ctail = _ssd_ffn_prompt(x_prompt, o5, norm_mix_w[0], w_int, conv_w, conv_b,
                                             dtb, alog, drep, snw, w_out_b, norm_ffn_w[0], w1, w2,
                                             norm_final_w)
    hfin5 = jnp.transpose(hfin5, (1, 0, 2))
    np_re, np_im = hfin5[None, :, :, :P], hfin5[None, :, :, P:]
    np_ssd = h_ssd[None]
    np_conv = ctail[None, :, SUBLANES - (SSD_CONV - 1):, :]

    x1s, ns_re, ns_im, hn_ssd, nconv = _sample_layer(
        x_sample, state_s5_re[0], state_s5_im[0], state_ssd[0], state_conv[0], norm_mix_w[0],
        w_int, conv_w, conv_b, dtb, alog, drep, snw, prep, wglut,
        s5_b_glu[0], s5_norm_w[0], w_out_b)
    y_sample = _ffn(x1s, norm_ffn_w[0], w1, w2, norm_final_w, DEC_BATCH).reshape(DEC_BATCH, 1, D_MODEL)
    ns_re, ns_im = ns_re[None], ns_im[None]
    ns_ssd = hn_ssd[None]
    ns_conv = nconv[None]

    return (y_prompt, y_sample, np_re, np_im, np_ssd, np_conv, ns_re, ns_im, ns_ssd, ns_conv)
```

```python
import functools

import jax
import jax.numpy as jnp
from jax import lax
from jax.experimental import pallas as pl
from jax.experimental.pallas import tpu as pltpu

F32 = jnp.float32
BF16 = jnp.bfloat16
HI = lax.Precision.HIGHEST
EPS = 1e-5

D_MODEL = 1024
BATCH = 8
SEQ = 2048
DEC_BATCH = 128
D_S5 = 1024
S5_CH = 16
S5_GROUPS = 64
S5_STATE = 64
D_SSD = 1024
SSD_HEADS = 16
SSD_HEAD_DIM = 64
SSD_STATE = 128
SSD_CONV = 4
D_CONV = 1536
D_FF = 4096

LANES = 128
SUBLANES = 8
Q5 = 16
NC5 = SEQ // Q5
M5 = NC5 * BATCH
R5 = Q5 * S5_CH
QS = 128
TL = 512
XBC_BLK = 512
SB = 16
GS = 8
GP = 8
GC = 8
VMEM_LIMIT = 56 * 1024 * 1024
VMEM_LIMIT_FUSED = 62 * 1024 * 1024


def _cp(*sem):
    return pltpu.CompilerParams(dimension_semantics=sem, vmem_limit_bytes=VMEM_LIMIT)


def _rms(x, w):
    return x * lax.rsqrt(jnp.mean(x * x, axis=-1, keepdims=True) + EPS) * w


def _dot(a, b, precision=None):
    return jnp.dot(a, b, preferred_element_type=F32, precision=precision)


def _dot_nt(a, b, precision=None):
    return lax.dot_general(a, b, (((1,), (1,)), ((), ())), preferred_element_type=F32,
                           precision=precision)


def _dot_tn(a, b, precision=None):
    return lax.dot_general(a, b, (((0,), (0,)), ((), ())), preferred_element_type=F32,
                           precision=precision)


def _iota(shape, dim):
    return lax.broadcasted_iota(jnp.int32, shape, dim)


def _silu(x):
    return x * jax.nn.sigmoid(x)


def _dot_split(a, b):
    a_hi = a.astype(BF16)
    b_hi = b.astype(BF16)
    a_lo = (a - a_hi.astype(F32)).astype(BF16)
    b_lo = (b - b_hi.astype(F32)).astype(BF16)
    return _dot(a_hi, b_hi) + _dot(a_lo, b_hi) + _dot(a_hi, b_lo)


def _repeat_channels(x):
    return jnp.broadcast_to(x.reshape(Q5, 1, LANES), (Q5, S5_CH, LANES)).reshape(R5, LANES)


def _tile_steps(x):
    return jnp.broadcast_to(x[None], (Q5, S5_CH, LANES)).reshape(R5, LANES)


def _s5_prep_kernel(lre_ref, lim_ref, ldt_ref, cre_ref, cim_ref, d_ref, bre_ref, bim_ref,
                    ws_ref, t_ref, wo_ref, aq_ref, a1c_ref, bbt_ref, cneg_ref, dcb_ref):
    dup = lambda a: jnp.concatenate([a, a], axis=1)
    for i in range(GP):
        c_re, c_im = cre_ref[i], cim_ref[i]
        drow = jnp.concatenate([jnp.zeros((1, R5 - S5_CH), F32), d_ref[i:i + 1, :]], axis=1)
        _s5_prep_group(dup(lre_ref[i:i + 1, :]), dup(lim_ref[i:i + 1, :]), ldt_ref[i:i + 1, :],
                       jnp.concatenate([c_re, c_im], axis=1), dup(c_re), dup(c_im), drow,
                       dup(jnp.transpose(bre_ref[i])), dup(jnp.transpose(bim_ref[i])),
                       ws_ref.at[i], t_ref.at[i], wo_ref.at[i], aq_ref.at[i], a1c_ref.at[i],
                       bbt_ref.at[i], cneg_ref.at[i], dcb_ref.at[i])


def _col_bcast(row):
    return jnp.transpose(jnp.broadcast_to(row, (LANES, LANES)))


def _s5_prep_group(lr2, li2, ldt, ccat, c2re, c2im, drow, bt2re, bt2im,
                   ws_ref, t_ref, wo_ref, aq_ref, a1c_ref, bbt_ref, cneg_ref, dcb_ref):
    dt = jnp.exp(ldt)

    mag = jnp.exp(lr2 * dt)
    ab_re = mag * jnp.cos(li2 * dt)
    ab_im = mag * jnp.sin(li2 * dt)
    den = lr2 * lr2 + li2 * li2
    nr, ni = ab_re - 1.0, ab_im
    f_re, f_im = (nr * lr2 + ni * li2) / den, (ni * lr2 - nr * li2) / den

    def powers(expo):
        p_re, p_im = jnp.ones(expo.shape, F32), jnp.zeros(expo.shape, F32)
        s_re, s_im = ab_re, ab_im
        for k in range(Q5.bit_length()):
            take = (lax.shift_right_logical(expo, k) & 1) == 1
            p_re, p_im = (jnp.where(take, p_re * s_re - p_im * s_im, p_re),
                          jnp.where(take, p_re * s_im + p_im * s_re, p_im))
            s_re, s_im = s_re * s_re - s_im * s_im, 2.0 * s_re * s_im
        return p_re, p_im

    lane1 = _iota((1, LANES), 1) < S5_STATE
    lo16 = _iota((S5_CH, LANES), 1) < S5_STATE
    bb_a = f_re * bt2re - f_im * bt2im
    bb_b = f_re * bt2im + f_im * bt2re
    bbt = jnp.where(lo16, bb_a, bb_b)
    bbt_sw = jnp.where(lo16, bb_b, bb_a)

    rev = SUBLANES * ((Q5 + SUBLANES) // SUBLANES)
    r = _iota((rev + Q5, LANES), 0)
    tab_re, tab_im = powers(jnp.where(r < rev, r, (rev + Q5 - 1) - r))
    aq_re, aq_im = tab_re[Q5:Q5 + 1, :], tab_im[Q5:Q5 + 1, :]
    a1_re, a1_im = tab_re[1:2, :], tab_im[1:2, :]
    aq_ref[...] = jnp.concatenate(
        [aq_re, jnp.where(lane1, -aq_im, aq_im), jnp.where(lane1, aq_im, -aq_im),
         jnp.zeros((5, LANES), F32)], axis=0)

    a1c_ref[0] = _col_bcast(a1_re)[0:S5_STATE, :]
    a1c_ref[1] = _col_bcast(a1_im)[0:S5_STATE, :]
    bbt_ref[...] = bbt
    cneg_ref[...] = jnp.where(lo16, c2re, -c2im)
    dcb_ref[...] = _col_bcast(drow[:, R5 - LANES:])[LANES - S5_CH:, :]

    ain_re, ain_im = _repeat_channels(tab_re[rev:rev + Q5, :]), _repeat_channels(tab_im[rev:rev + Q5, :])
    sign = jnp.where(lane1, -1.0, 1.0)
    ws_t = ain_re * _tile_steps(bbt) + (sign * ain_im) * _tile_steps(bbt_sw)
    ws = jnp.transpose(ws_t)
    ws_ref[...] = ws.astype(BF16)

    upper = _iota((LANES, R5), 0) < S5_STATE
    vmat = _dot_split(ccat, jnp.where(upper, ws, -ws))
    vrow = _iota((S5_CH, R5), 0)
    vcol = _iota((S5_CH, R5), 1)
    vmat = vmat + jnp.where(vcol == (Q5 - 1) * S5_CH + vrow, drow, 0.0)
    for l in range(Q5):
        width = (l + 1) * S5_CH
        rolled = vmat if width == R5 else pltpu.roll(vmat, width, axis=1)
        t_ref[l * S5_CH:(l + 1) * S5_CH, :] = jnp.where(vcol < width, rolled, 0.0).astype(BF16)

    aout_re, aout_im = _repeat_channels(tab_re[1:Q5 + 1, :]), _repeat_channels(tab_im[1:Q5 + 1, :])
    c2re_t, c2im_t = _tile_steps(c2re), _tile_steps(c2im)
    wo_ref[...] = jnp.where(_iota((R5, LANES), 1) < S5_STATE,
                            c2re_t * aout_re - c2im_t * aout_im,
                            -(c2re_t * aout_im + c2im_t * aout_re)).astype(BF16)


def _s5_prep(lam_re, lam_im, log_dt, b_re, b_im, c_re, c_im, d):
    G, P, C = S5_GROUPS, S5_STATE, S5_CH
    args = (lam_re, lam_im, log_dt.reshape(G, 1), c_re, c_im, d.reshape(G, C), b_re, b_im)
    spec = lambda a: pl.BlockSpec((GP,) + a.shape[1:], lambda g, nd=a.ndim: (g,) + (0,) * (nd - 1))
    out_shapes = (
        jax.ShapeDtypeStruct((G, 2 * P, R5), BF16),
        jax.ShapeDtypeStruct((G, R5, R5), BF16),
        jax.ShapeDtypeStruct((G, R5, 2 * P), BF16),
        jax.ShapeDtypeStruct((G, 8, 2 * P), F32),
        jax.ShapeDtypeStruct((G, 2, P, LANES), F32),
        jax.ShapeDtypeStruct((G, C, 2 * P), F32),
        jax.ShapeDtypeStruct((G, C, 2 * P), F32),
        jax.ShapeDtypeStruct((G, C, LANES), F32),
    )
    return pl.pallas_call(
        _s5_prep_kernel, grid=(G // GP,),
        in_specs=[spec(a) for a in args],
        out_specs=[pl.BlockSpec((GP,) + s.shape[1:], lambda g, nd=len(s.shape): (g,) + (0,) * (nd - 1))
                   for s in out_shapes],
        out_shape=out_shapes, compiler_params=_cp("arbitrary"), name="s5_prep",
    )(*args)


def _s5_inproj_kernel(x_ref, nw_ref, wut_ref, ut_ref):
    x = x_ref[...].reshape(M5, D_MODEL)
    hn = _rms(x, nw_ref[...]).astype(BF16)
    ut = _dot_nt(wut_ref[...], hn)
    ut_ref[...] = ut.astype(BF16).reshape(S5_GROUPS, S5_CH, M5)


def _s5_core_kernel(ut_ref, t_ref, ws_ref, wo_ref, aq_ref, y_ref, hfin_ref,
                    hs_scr, sp_scr, sq_scr, yin_scr):
    for i in range(GC):
        ut = ut_ref[i]
        sp = jnp.transpose(_dot(ws_ref[i], ut))
        sp_scr[i] = sp
        sq_scr[i] = pltpu.roll(sp, S5_STATE, axis=1)
        yin_scr[i] = _dot(t_ref[i], ut)
    for i in range(GC):
        a1 = jnp.broadcast_to(aq_ref[i, 0:1, :], (BATCH, LANES))
        a2 = jnp.broadcast_to(aq_ref[i, 1:2, :], (BATCH, LANES))
        a3 = jnp.broadcast_to(aq_ref[i, 2:3, :], (BATCH, LANES))
        hp = jnp.zeros((BATCH, LANES), F32)
        hq = jnp.zeros((BATCH, LANES), F32)
        for k in range(NC5):
            rows = pl.ds(k * BATCH, BATCH)
            hs_scr[i, rows, :] = hp
            hp, hq = (a1 * hp + a2 * hq + sp_scr[i, rows, :],
                      a1 * hq + a3 * hp + sq_scr[i, rows, :])
        hfin_ref[i] = hp
    for i in range(GC):
        y = yin_scr[i] + _dot_nt(wo_ref[i], hs_scr[i].astype(BF16))
        y_ref[i] = y.astype(BF16)


def _s5_out_kernel(g_ref, wglut_ref, bglu_ref, nw_ref, wout_ref, o_ref):
    g = jax.nn.gelu(g_ref[...].astype(F32).reshape(D_S5, M5))
    z = _dot(wglut_ref[...], g.astype(BF16)) + bglu_ref[...]
    out = g * jax.nn.sigmoid(z)
    ms = jnp.mean(out * out, axis=0, keepdims=True)
    y5 = out * lax.rsqrt(ms + EPS) * nw_ref[...]
    o = _dot_tn(y5.astype(BF16), wout_ref[...])
    o_ref[...] = o.reshape(NC5, BATCH, D_MODEL)


def _s5_prompt(x_prompt, norm_mix_w, w_int, prep, wglut, b_glu, s5_norm_w, w_out_b):
    ws, tmat, wo, aq = prep[:4]
    x_v = jnp.transpose(x_prompt, (1, 0, 2)).reshape(NC5, Q5, BATCH, D_MODEL)
    tok_blk = pl.BlockSpec((NC5, None, BATCH, D_MODEL), lambda s: (0, s, 0, 0))
    chan_blk = pl.BlockSpec((S5_GROUPS, S5_CH, M5), lambda s: (0, s, 0))
    const2 = lambda s: (0, 0)
    once = pl.Buffered(1)
    ut = pl.pallas_call(
        _s5_inproj_kernel, grid=(Q5,),
        in_specs=[tok_blk,
                  pl.BlockSpec((1, D_MODEL), const2),
                  pl.BlockSpec((D_S5, D_MODEL), const2, pipeline_mode=once)],
        out_specs=chan_blk,
        out_shape=jax.ShapeDtypeStruct((S5_GROUPS, R5, M5), BF16),
        compiler_params=_cp("arbitrary"), name="s5_inproj",
    )(x_v, norm_mix_w.reshape(1, D_MODEL), w_int)

    per_g = lambda shape: pl.BlockSpec((GC,) + shape, lambda g: (g, 0, 0))
    y5_pre, hfin = pl.pallas_call(
        _s5_core_kernel, grid=(S5_GROUPS // GC,),
        in_specs=[per_g((R5, M5)), per_g((R5, R5)), per_g((LANES, R5)), per_g((R5, LANES)),
                  per_g((8, LANES))],
        out_specs=[per_g((R5, M5)), per_g((BATCH, LANES))],
        out_shape=(jax.ShapeDtypeStruct((S5_GROUPS, R5, M5), BF16),
                   jax.ShapeDtypeStruct((S5_GROUPS, BATCH, LANES), F32)),
        scratch_shapes=[pltpu.VMEM((GC, M5, LANES), F32) for _ in range(3)]
        + [pltpu.VMEM((GC, R5, M5), F32)],
        compiler_params=_cp("arbitrary"), name="s5_core",
    )(ut, tmat, ws, wo, aq)

    o5 = pl.pallas_call(
        _s5_out_kernel, grid=(Q5,),
        in_specs=[chan_blk,
                  pl.BlockSpec((D_S5, D_S5), const2, pipeline_mode=once),
                  pl.BlockSpec((D_S5, 1), const2),
                  pl.BlockSpec((D_S5, 1), const2),
                  pl.BlockSpec((D_S5, D_MODEL), const2, pipeline_mode=once)],
        out_specs=tok_blk,
        out_shape=jax.ShapeDtypeStruct((NC5, Q5, BATCH, D_MODEL), F32),
        compiler_params=_cp("arbitrary"), name="s5_out",
    )(y5_pre, wglut, b_glu.reshape(D_S5, 1), s5_norm_w.reshape(D_S5, 1), w_out_b)
    return o5.reshape(SEQ, BATCH * D_MODEL), hfin


assert D_SSD == 2 * XBC_BLK and D_CONV == 3 * XBC_BLK and 2 * SSD_STATE * 2 == XBC_BLK


def _ssd_ffn_kernel(x_ref, o5_ref, nw_ref, wz_ref, wxbc0_ref, wxbc1_ref, wxbc2_ref, wdt_ref,
                    cw_ref, cb_ref, dtb_ref, alog_ref, drep_ref, snw_ref, wout_ref,
                    fnw_ref, w1_ref, w2_ref, fw_ref, x1s_ref,
                    y_ref, hout_ref, cout_ref, ys_ref,
                    xp_scr, xs_scr, b_scr, c_scr, z_scr, a_scr, dt_scr, y_scr, h_scr, x1_scr,
                    hf_scr, h1_scr, acc_scr):
    wxbc_refs = (wxbc0_ref, wxbc1_ref, wxbc2_ref)
    t = pl.program_id(0)
    n_tiles = pl.num_programs(0) - 1
    j = lax.rem(t, SEQ // TL)

    @pl.when(t == 0)
    def _():
        x1s = x1s_ref[...]
        x1_scr[0:DEC_BATCH, :] = x1s
        x1_scr[DEC_BATCH:, :] = jnp.zeros((TL - DEC_BATCH, D_MODEL), F32)
        hf_scr[0:DEC_BATCH, :] = _rms(x1s, fnw_ref[...]).astype(BF16)
        hf_scr[DEC_BATCH:, :] = jnp.zeros((TL - DEC_BATCH, D_MODEL), BF16)

    @pl.when(j == 0)
    def _():
        xp_scr[0:SUBLANES, :] = jnp.zeros((SUBLANES, D_CONV), F32)
        h_scr[...] = jnp.zeros(h_scr.shape, F32)

    blk = D_FF // 4
    sl = blk // 4

    def ffn_up(c, q):
        cols = slice(c * blk + q * sl, c * blk + (q + 1) * sl)
        h1 = jnp.square(jnp.maximum(_dot(hf_scr[...], w1_ref[:, cols]), 0.0))
        h1_scr[:, q * sl:(q + 1) * sl] = h1.astype(BF16)

    def ffn_down(c, q):
        cols = slice(q * sl, (q + 1) * sl)
        acc_scr[:, cols] += _dot(h1_scr[...], w2_ref[c * blk:(c + 1) * blk, cols])

    ffn_slices = iter([functools.partial(f, c, q) for c in range(4)
                       for f in (ffn_up, ffn_down) for q in range(4)])

    def ffn_step(n=1):
        for _ in range(n):
            next(ffn_slices)()

    ffn_step(2)

    x = x_ref[...]
    hn = _rms(x, nw_ref[...]).astype(BF16)
    acc_scr[...] = x1_scr[...]

    def conv_block(i):
        cols = slice(i * XBC_BLK, (i + 1) * XBC_BLK)
        xp_scr[SUBLANES:, cols] = _dot_nt(hn, wxbc_refs[i][...])
        conv = cb_ref[:, cols] + sum(
            xp_scr[SUBLANES - (SSD_CONV - 1) + k:SUBLANES - (SSD_CONV - 1) + k + TL, cols]
            * cw_ref[k:k + 1, cols] for k in range(SSD_CONV))
        tail = xp_scr[TL:TL + SUBLANES, cols]
        xp_scr[0:SUBLANES, cols] = tail
        cout_ref[:, cols] = tail
        return _silu(conv)

    xs_scr[:, 0:XBC_BLK] = conv_block(0)
    xs_scr[:, XBC_BLK:2 * XBC_BLK] = conv_block(1)
    bc = conv_block(2)
    b_scr[...] = bc[:, :2 * SSD_STATE]
    c_scr[...] = bc[:, 2 * SSD_STATE:]
    z_scr[...] = _dot_nt(hn, wz_ref[...])
    wdt = jnp.concatenate([wdt_ref[...], jnp.zeros((LANES - SSD_HEADS, D_MODEL), BF16)], axis=0)
    dt = jax.nn.softplus(_dot_nt(hn, wdt) + dtb_ref[...])
    dt_scr[...] = dt
    a_scr[...] = dt * (-jnp.exp(alog_ref[...]))
    ffn_step(5)

    li = _iota((QS, QS), 0)
    si = _iota((QS, QS), 1)
    causal = li >= si
    tril = causal.astype(F32)
    lo = si < SSD_HEAD_DIM

    def chunk(ci, carry):
        rows = pl.ds(ci * QS, QS)
        acum = _dot(tril, a_scr[rows, :], HI)
        acum_t = jnp.transpose(acum)
        dt_t = jnp.transpose(dt_scr[rows, :])
        alast = acum[QS - 1:QS, :]
        alast_t = acum_t[:, QS - 1:QS]
        for g in range(2):
            bg = b_scr[rows, g * SSD_STATE:(g + 1) * SSD_STATE]
            cg_b = c_scr[rows, g * SSD_STATE:(g + 1) * SSD_STATE].astype(BF16)
            bt = jnp.transpose(bg)
            cb = _dot(cg_b, bt.astype(BF16))
            h_grp = jnp.concatenate([h_scr[4 * g + i] for i in range(4)], axis=1)
            y_off = _dot(cg_b, h_grp.astype(BF16))
            for hp in range(4):
                pr = 4 * g + hp
                cols = slice(pr * LANES, (pr + 1) * LANES)
                xs_pair = xs_scr[rows, cols]
                xs_b = xs_pair.astype(BF16)
                zero = jnp.zeros_like(xs_b)
                xs_half = (jnp.where(lo, xs_b, zero), jnp.where(lo, zero, xs_b))
                acols, y, st = [], None, None
                for par, h in enumerate((2 * pr, 2 * pr + 1)):
                    acol = jnp.broadcast_to(acum[:, h:h + 1], (QS, QS))
                    arow = acum_t[h:h + 1, :]
                    dtrow = dt_t[h:h + 1, :]
                    lmat = jnp.exp(jnp.where(causal, acol - arow, -1e30))
                    m = (cb * lmat * dtrow).astype(BF16)
                    bts = (bt * (jnp.exp(alast_t[h:h + 1, :] - arow) * dtrow)).astype(BF16)
                    yd = _dot(m, xs_half[par])
                    sd = _dot(bts, xs_half[par])
                    y = yd if y is None else y + yd
                    st = sd if st is None else st + sd
                    acols.append(acol)
                h0, h1 = 2 * pr, 2 * pr + 1
                decay = jnp.exp(jnp.where(lo[0:1, :], alast[:, h0:h0 + 1], alast[:, h1:h1 + 1]))
                h_scr[pr] = decay * h_scr[pr] + st
                y = (y + jnp.exp(jnp.where(lo, acols[0], acols[1])) * y_off[:, hp * LANES:(hp + 1) * LANES]
                     + drep_ref[:, cols] * xs_pair)
                y_scr[rows, cols] = y * _silu(z_scr[rows, cols])
                if hp != 3 or (ci == 0 and g == 0):
                    ffn_step()
        half = D_SSD // 2
        for g in range(2):
            yg = y_scr[rows, g * half:(g + 1) * half]
            yg = yg * lax.rsqrt(jnp.mean(yg * yg, axis=-1, keepdims=True) + EPS)
            y_scr[rows, g * half:(g + 1) * half] = yg * snw_ref[:, g * half:(g + 1) * half]
        return carry

    for ci in range(TL // QS):
        chunk(ci, 0)
    assert next(ffn_slices, None) is None, "every FFN slice must have been issued"
    y_ref[...] = _rms(acc_scr[...], fw_ref[...])
    for hrows in (slice(0, TL // 2), slice(TL // 2, TL)):
        x1 = x[hrows] + o5_ref[hrows, :] + _dot(y_scr[hrows, :].astype(BF16), wout_ref[...])
        x1_scr[hrows, :] = x1
        hf_scr[hrows, :] = _rms(x1, fnw_ref[...]).astype(BF16)

    @pl.when(t == 0)
    def _():
        ys_ref[...] = y_ref[0:DEC_BATCH, :]

    @pl.when(jnp.logical_and(j == SEQ // TL - 1, t < n_tiles))
    def _():
        for pr in range(SSD_HEADS // 2):
            hout_ref[2 * pr:2 * pr + 2] = jnp.transpose(h_scr[pr]).reshape(2, SSD_HEAD_DIM, SSD_STATE)


def _ssd_ffn_prompt(x_prompt, o5, norm_mix_w, w_int, conv_w, conv_b, dtb, alog, drep, snw,
                    w_out_b, norm_ffn_w, w1, w2, norm_final_w, x1_sample):
    nj = SEQ // TL
    n_tiles = BATCH * nj
    c2 = lambda t: (0, 0)
    once = pl.Buffered(1)
    row = lambda n: pl.BlockSpec((1, n), c2, pipeline_mode=once)
    mat = lambda r, c: pl.BlockSpec((r, c), c2, pipeline_mode=once)
    wrows = lambda n, i: pl.BlockSpec((n, D_MODEL), lambda t: (i, 0), pipeline_mode=once)
    xbc0 = (D_S5 + D_SSD) // XBC_BLK

    def tile(t):
        tt = jnp.minimum(t, n_tiles - 1)
        return tt // nj, lax.rem(tt, nj)

    def prev(t):
        tp = jnp.maximum(t - 1, 0)
        return tp // nj, lax.rem(tp, nj)

    return pl.pallas_call(
        _ssd_ffn_kernel, grid=(n_tiles + 1,),
        in_specs=[pl.BlockSpec((None, TL, D_MODEL), lambda t: (*tile(t), 0)),
                  pl.BlockSpec((TL, D_MODEL), lambda t: tile(t)[::-1]),
                  row(D_MODEL),
                  wrows(D_SSD, D_S5 // D_SSD),
                  wrows(XBC_BLK, xbc0), wrows(XBC_BLK, xbc0 + 1), wrows(XBC_BLK, xbc0 + 2),
                  wrows(SSD_HEADS, (D_S5 + D_SSD + D_CONV) // SSD_HEADS),
                  mat(SSD_CONV, D_CONV), row(D_CONV),
                  row(LANES), row(LANES), row(D_SSD), row(D_SSD),
                  wrows(D_SSD, D_S5 // D_SSD),
                  row(D_MODEL), mat(D_MODEL, D_FF), mat(D_FF, D_MODEL), row(D_MODEL),
                  mat(DEC_BATCH, D_MODEL)],
        out_specs=[pl.BlockSpec((None, TL, D_MODEL), lambda t: (*prev(t), 0)),
                   pl.BlockSpec((None, SSD_HEADS, SSD_HEAD_DIM, SSD_STATE), lambda t: (tile(t)[0], 0, 0, 0)),
                   pl.BlockSpec((None, SUBLANES, D_CONV), lambda t: (tile(t)[0], 0, 0)),
                   pl.BlockSpec((DEC_BATCH, D_MODEL), c2)],
        out_shape=(jax.ShapeDtypeStruct((BATCH, SEQ, D_MODEL), F32),
                   jax.ShapeDtypeStruct((BATCH, SSD_HEADS, SSD_HEAD_DIM, SSD_STATE), F32),
                   jax.ShapeDtypeStruct((BATCH, SUBLANES, D_CONV), F32),
                   jax.ShapeDtypeStruct((DEC_BATCH, D_MODEL), F32)),
        scratch_shapes=[pltpu.VMEM((SUBLANES + TL, D_CONV), F32),
                        pltpu.VMEM((TL, D_SSD), F32),
                        pltpu.VMEM((TL, 2 * SSD_STATE), F32),
                        pltpu.VMEM((TL, 2 * SSD_STATE), F32),
                        pltpu.VMEM((TL, D_SSD), F32),
                        pltpu.VMEM((TL, LANES), F32),
                        pltpu.VMEM((TL, LANES), F32),
                        pltpu.VMEM((TL, D_SSD), F32),
                        pltpu.VMEM((SSD_HEADS // 2, SSD_STATE, LANES), F32),
                        pltpu.VMEM((TL, D_MODEL), F32),
                        pltpu.VMEM((TL, D_MODEL), BF16),
                        pltpu.VMEM((TL, D_FF // 4), BF16),
                        pltpu.VMEM((TL, D_MODEL), F32)],
        compiler_params=pltpu.CompilerParams(dimension_semantics=("arbitrary",),
                                             vmem_limit_bytes=VMEM_LIMIT_FUSED),
        name="ssd_ffn",
    )(x_prompt, o5, norm_mix_w.reshape(1, D_MODEL), w_int, w_int, w_int, w_int, w_int,
      conv_w, conv_b, dtb, alog, drep, snw, w_out_b,
      norm_ffn_w.reshape(1, D_MODEL), w1, w2, norm_final_w.reshape(1, D_MODEL), x1_sample)


def _sample_inproj_kernel(x_ref, nw_ref, win_ref, cw_ref, cb_ref, cbuf_ref, dtb_ref, alog_ref,
                          u_ref, z_ref, xs_ref, xdt_ref, b_ref, c_ref, dec_ref, nconv_ref):
    hn = _rms(x_ref[...], nw_ref[...]).astype(BF16)
    o_z, o_xbc, o_dt = D_S5, D_S5 + D_SSD, D_S5 + D_SSD + D_CONV
    u_ref[...] = _dot_nt(win_ref[0:o_z, :], hn)
    z_ref[...] = _dot_nt(hn, win_ref[o_z:o_xbc, :])
    xbc_t = _dot_nt(win_ref[o_xbc:o_dt, :], hn)
    wdt = jnp.concatenate([win_ref[o_dt:o_dt + SSD_HEADS, :],
                           jnp.zeros((LANES - SSD_HEADS, D_MODEL), BF16)], axis=0)
    dt = jax.nn.softplus(_dot_nt(hn, wdt) + dtb_ref[...])
    dec_ref[...] = jnp.exp(dt * (-jnp.exp(alog_ref[...])))
    head_of_col = lax.shift_right_logical(_iota((LANES, D_SSD), 1), 6)
    dt_rep = _dot(dt, (_iota((LANES, D_SSD), 0) == head_of_col).astype(F32), HI)
    cw_t = jnp.transpose(jnp.concatenate(
        [cw_ref[...], cb_ref[...], jnp.zeros((LANES - SSD_CONV - 1, D_CONV), F32)], axis=0))
    conv_t = cw_t[:, SSD_CONV:SSD_CONV + 1] + xbc_t * cw_t[:, SSD_CONV - 1:SSD_CONV]
    for k in range(SSD_CONV - 1):
        conv_t = conv_t + cbuf_ref[k] * cw_t[:, k:k + 1]
        if k > 0:
            nconv_ref[k - 1] = cbuf_ref[k]
    nconv_ref[SSD_CONV - 2] = xbc_t
    conv = jnp.transpose(_silu(conv_t))
    xs_ref[...] = conv[:, :D_SSD]
    xdt_ref[...] = conv[:, :D_SSD] * dt_rep
    b_ref[...] = conv[:, D_SSD:D_SSD + 2 * SSD_STATE]
    c_ref[...] = conv[:, D_SSD + 2 * SSD_STATE:]


def _sample_s5_kernel(u_ref, hre_ref, him_ref, a1c_ref, bbt_ref, cneg_ref, dcb_ref,
                      nre_ref, nim_ref, y_ref):
    for i in range(GS):
        u = u_ref[i]
        bu = _dot_tn(bbt_ref[i], u, HI)
        a_re, a_im = a1c_ref[i, 0], a1c_ref[i, 1]
        h_re, h_im = hre_ref[i], him_ref[i]
        n_re = a_re * h_re - a_im * h_im + bu[0:S5_STATE]
        n_im = a_re * h_im + a_im * h_re + bu[S5_STATE:]
        nre_ref[i] = n_re
        nim_ref[i] = n_im
        y_ref[i] = _dot(cneg_ref[i], jnp.concatenate([n_re, n_im], axis=0), HI) + dcb_ref[i] * u


def _sample_ssd_kernel(dec_ref, h0_ref, xdt_ref, xs_ref, b_ref, c_ref, drep_ref, hn_ref, y_ref):
    blk = pl.program_id(0)
    hpg = SSD_HEADS // 2
    half = D_SSD // 2
    xt = jnp.transpose(jnp.concatenate(
        [xdt_ref[...], jnp.zeros((LANES - SB, D_SSD), F32)], axis=0))
    c_b = c_ref[...].astype(BF16)
    rowi = _iota((SB, half), 0)
    ys = [jnp.zeros((SB, half), F32) for _ in range(2)]
    for jj in range(SB):
        seq = blk * SB + jj
        for g in range(2):
            brow = b_ref[jj:jj + 1, g * SSD_STATE:(g + 1) * SSD_STATE]
            parts = []
            for h in range(g * hpg, (g + 1) * hpg):
                xcol = xt[h * SSD_HEAD_DIM:(h + 1) * SSD_HEAD_DIM, jj:jj + 1]
                hn = dec_ref[seq, h] * h0_ref[jj, h] + xcol * brow
                hn_ref[jj, h] = hn
                parts.append(hn.astype(BF16))
            y_all = _dot_nt(c_b[:, g * SSD_STATE:(g + 1) * SSD_STATE], jnp.concatenate(parts, axis=0))
            ys[g] = jnp.where(rowi == jj, y_all, ys[g])
    y_ref[...] = jnp.concatenate(ys, axis=1) + drep_ref[...] * xs_ref[...]


def _sample_mix_kernel(x_ref, y5_ref, ys_ref, z_ref, wglut_ref, bglu_ref, nw5_ref, snw_ref,
                       wout_ref, x1_ref):
    g = jax.nn.gelu(jnp.transpose(y5_ref[...]))
    out = g * jax.nn.sigmoid(_dot_nt(g.astype(BF16), wglut_ref[...]) + bglu_ref[...])
    y5 = _rms(out, nw5_ref[...])
    y = ys_ref[...] * _silu(z_ref[...])
    half = D_SSD // 2
    yn = jnp.concatenate(
        [_rms(y[:, i * half:(i + 1) * half], snw_ref[:, i * half:(i + 1) * half]) for i in range(2)],
        axis=1)
    x1_ref[...] = (x_ref[...] + _dot(y5.astype(BF16), wout_ref[0:D_S5, :])
                   + _dot(yn.astype(BF16), wout_ref[D_S5:, :]))


def _sample_layer(x_sample, st_re, st_im, st_ssd, st_conv, norm_mix_w, w_int,
                  conv_w, conv_b, dtb, alog, drep, snw, prep, wglut, b_glu, s5_norm_w,
                  w_out_b):
    nb = DEC_BATCH
    a1c, bbt, cneg, dcb = prep[4:8]
    xs2 = x_sample.reshape(nb, D_MODEL)
    sds = lambda *s: jax.ShapeDtypeStruct(s, F32)
    u_t, z, xs, xdt, bm, cm, dec, nconv_t = pl.pallas_call(
        _sample_inproj_kernel,
        out_shape=(sds(D_S5, nb), sds(nb, D_SSD), sds(nb, D_SSD), sds(nb, D_SSD),
                   sds(nb, 2 * SSD_STATE), sds(nb, 2 * SSD_STATE), sds(nb, LANES),
                   sds(SSD_CONV - 1, D_CONV, nb)),
        compiler_params=_cp(), name="sample_inproj",
    )(xs2, norm_mix_w.reshape(1, D_MODEL), w_int, conv_w, conv_b,
      jnp.transpose(st_conv, (1, 2, 0)), dtb, alog)

    def per_g(shape):
        nd = len(shape)
        return pl.BlockSpec((GS,) + shape, lambda g: (g,) + (0,) * nd)

    P = S5_STATE
    n_re_t, n_im_t, y5_t = pl.pallas_call(
        _sample_s5_kernel, grid=(S5_GROUPS // GS,),
        in_specs=[per_g((S5_CH, nb)), per_g((P, nb)), per_g((P, nb)), per_g((2, P, LANES)),
                  per_g((S5_CH, 2 * P)), per_g((S5_CH, 2 * P)), per_g((S5_CH, LANES))],
        out_specs=[per_g((P, nb)), per_g((P, nb)), per_g((S5_CH, nb))],
        out_shape=(sds(S5_GROUPS, P, nb), sds(S5_GROUPS, P, nb), sds(S5_GROUPS, S5_CH, nb)),
        compiler_params=_cp("arbitrary"), name="sample_s5",
    )(u_t.reshape(S5_GROUPS, S5_CH, nb), jnp.transpose(st_re, (1, 2, 0)),
      jnp.transpose(st_im, (1, 2, 0)), a1c, bbt, cneg, dcb)
    y5 = y5_t.reshape(D_S5, nb)

    smem = pl.BlockSpec(memory_space=pltpu.SMEM)
    blk2 = lambda n: pl.BlockSpec((SB, n), lambda i: (i, 0))
    st_spec = pl.BlockSpec((SB, SSD_HEADS, SSD_HEAD_DIM, SSD_STATE), lambda i: (i, 0, 0, 0))
    hn_ssd, ys = pl.pallas_call(
        _sample_ssd_kernel, grid=(nb // SB,),
        in_specs=[smem, st_spec, blk2(D_SSD), blk2(D_SSD), blk2(2 * SSD_STATE),
                  blk2(2 * SSD_STATE), pl.BlockSpec((1, D_SSD), lambda i: (0, 0))],
        out_specs=[st_spec, blk2(D_SSD)],
        out_shape=(sds(nb, SSD_HEADS, SSD_HEAD_DIM, SSD_STATE), sds(nb, D_SSD)),
        compiler_params=_cp("arbitrary"), name="sample_ssd",
    )(dec[:, :SSD_HEADS], st_ssd, xdt, xs, bm, cm, drep)

    x1 = pl.pallas_call(
        _sample_mix_kernel, out_shape=sds(nb, D_MODEL),
        compiler_params=_cp(), name="sample_mix",
    )(xs2, y5, ys, z, wglut, b_glu.reshape(1, D_S5), s5_norm_w.reshape(1, D_S5), snw, w_out_b)
    to_seq_major = lambda a: jnp.transpose(a, (2, 0, 1))
    return x1, to_seq_major(n_re_t), to_seq_major(n_im_t), hn_ssd, to_seq_major(nconv_t)


def kernel(x_prompt, x_sample, state_s5_re, state_s5_im, state_ssd, state_conv, norm_mix_w, w_in, s5_lam_re, s5_lam_im, s5_log_dt, s5_b_re, s5_b_im, s5_c_re, s5_c_im, s5_d, s5_w_glu, s5_b_glu, s5_norm_w, ssd_conv_w, ssd_conv_b, ssd_a_log, ssd_dt_bias, ssd_d, ssd_norm_w, w_out, norm_ffn_w, w_ff1, w_ff2, norm_final_w):
    P = S5_STATE
    w_int = jnp.transpose(w_in[0]).astype(BF16)
    wglut = jnp.transpose(s5_w_glu[0]).astype(BF16)
    w_out_b = w_out[0].astype(BF16)
    w1 = w_ff1[0].astype(BF16)
    w2 = w_ff2[0].astype(BF16)
    pad_h = lambda v: jnp.pad(v.reshape(1, SSD_HEADS), ((0, 0), (0, LANES - SSD_HEADS)))
    dtb, alog = pad_h(ssd_dt_bias[0]), pad_h(ssd_a_log[0])
    drep = jnp.repeat(ssd_d[0], SSD_HEAD_DIM).reshape(1, D_SSD)
    snw = ssd_norm_w[0].reshape(1, D_SSD)
    conv_w, conv_b = ssd_conv_w[0], ssd_conv_b[0].reshape(1, D_CONV)

    prep = _s5_prep(s5_lam_re[0], s5_lam_im[0], s5_log_dt[0], s5_b_re[0], s5_b_im[0],
                    s5_c_re[0], s5_c_im[0], s5_d[0])

    x1s, ns_re, ns_im, hn_ssd, nconv = _sample_layer(
        x_sample, state_s5_re[0], state_s5_im[0], state_ssd[0], state_conv[0], norm_mix_w[0],
        w_int, conv_w, conv_b, dtb, alog, drep, snw, prep, wglut,
        s5_b_glu[0], s5_norm_w[0], w_out_b)

    o5, hfin5 = _s5_prompt(x_prompt, norm_mix_w[0], w_int, prep, wglut, s5_b_glu[0], s5_norm_w[0],
                           w_out_b)
    y_prompt, h_ssd, ctail, y_s = _ssd_ffn_prompt(
        x_prompt, o5, norm_mix_w[0], w_int, conv_w, conv_b, dtb, alog, drep, snw, w_out_b,
        norm_ffn_w[0], w1, w2, norm_final_w, x1s)
    hfin5 = jnp.transpose(hfin5, (1, 0, 2))
    np_re, np_im = hfin5[None, :, :, :P], hfin5[None, :, :, P:]
    np_ssd = h_ssd[None]
    np_conv = ctail[None, :, SUBLANES - (SSD_CONV - 1):, :]

    y_sample = y_s.reshape(DEC_BATCH, 1, D_MODEL)
    ns_re, ns_im = ns_re[None], ns_im[None]
    ns_ssd = hn_ssd[None]
    ns_conv = nconv[None]

    return (y_prompt, y_sample, np_re, np_im, np_ssd, np_conv, ns_re, ns_im, ns_ssd, ns_conv)
```

```python
import functools

import jax
import jax.numpy as jnp
from jax import lax
from jax.experimental import pallas as pl
from jax.experimental.pallas import tpu as pltpu

F32 = jnp.float32
BF16 = jnp.bfloat16
HI = lax.Precision.HIGHEST
EPS = 1e-5
LOG2_E = 1.4426950408889634

D_MODEL = 1024
BATCH = 8
SEQ = 2048
DEC_BATCH = 128
D_S5 = 1024
S5_CH = 16
S5_GROUPS = 64
S5_STATE = 64
D_SSD = 1024
SSD_HEADS = 16
SSD_HEAD_DIM = 64
SSD_STATE = 128
SSD_CONV = 4
D_CONV = 1536
D_FF = 4096

LANES = 128
SUBLANES = 8
Q5 = 16
NC5 = SEQ // Q5
M5 = NC5 * BATCH
R5 = Q5 * S5_CH
QS = 128
TL = 512
XBC_BLK = 512
SB = 16
GS = 8
GP = 8
GC = 8
VMEM_LIMIT = 56 * 1024 * 1024
VMEM_LIMIT_FUSED = 62 * 1024 * 1024


def _cp(*sem):
    return pltpu.CompilerParams(dimension_semantics=sem, vmem_limit_bytes=VMEM_LIMIT)


def _rms(x, w):
    return x * lax.rsqrt(jnp.mean(x * x, axis=-1, keepdims=True) + EPS) * w


def _dot(a, b, precision=None):
    return jnp.dot(a, b, preferred_element_type=F32, precision=precision)


def _dot_nt(a, b, precision=None):
    return lax.dot_general(a, b, (((1,), (1,)), ((), ())), preferred_element_type=F32,
                           precision=precision)


def _dot_tn(a, b, precision=None):
    return lax.dot_general(a, b, (((0,), (0,)), ((), ())), preferred_element_type=F32,
                           precision=precision)


def _iota(shape, dim):
    return lax.broadcasted_iota(jnp.int32, shape, dim)


def _silu(x):
    return x * jax.nn.sigmoid(x)


def _dot_split(a, b):
    a_hi = a.astype(BF16)
    b_hi = b.astype(BF16)
    a_lo = (a - a_hi.astype(F32)).astype(BF16)
    b_lo = (b - b_hi.astype(F32)).astype(BF16)
    return _dot(a_hi, b_hi) + _dot(a_lo, b_hi) + _dot(a_hi, b_lo)


def _repeat_channels(x):
    return jnp.broadcast_to(x.reshape(Q5, 1, LANES), (Q5, S5_CH, LANES)).reshape(R5, LANES)


def _tile_steps(x):
    return jnp.broadcast_to(x[None], (Q5, S5_CH, LANES)).reshape(R5, LANES)


def _s5_prep_kernel(lre_ref, lim_ref, ldt_ref, cre_ref, cim_ref, d_ref, bre_ref, bim_ref,
                    ws_ref, t_ref, wo_ref, aq_ref, a1c_ref, bbt_ref, cneg_ref, dcb_ref):
    dup = lambda a: jnp.concatenate([a, a], axis=1)
    for i in range(GP):
        c_re, c_im = cre_ref[i], cim_ref[i]
        drow = jnp.concatenate([jnp.zeros((1, R5 - S5_CH), F32), d_ref[i:i + 1, :]], axis=1)
        _s5_prep_group(dup(lre_ref[i:i + 1, :]), dup(lim_ref[i:i + 1, :]), ldt_ref[i:i + 1, :],
                       jnp.concatenate([c_re, c_im], axis=1), dup(c_re), dup(c_im), drow,
                       dup(jnp.transpose(bre_ref[i])), dup(jnp.transpose(bim_ref[i])),
                       ws_ref.at[i], t_ref.at[i], wo_ref.at[i], aq_ref.at[i], a1c_ref.at[i],
                       bbt_ref.at[i], cneg_ref.at[i], dcb_ref.at[i])


def _col_bcast(row):
    return jnp.transpose(jnp.broadcast_to(row, (LANES, LANES)))


def _s5_prep_group(lr2, li2, ldt, ccat, c2re, c2im, drow, bt2re, bt2im,
                   ws_ref, t_ref, wo_ref, aq_ref, a1c_ref, bbt_ref, cneg_ref, dcb_ref):
    dt = jnp.exp(ldt)

    mag = jnp.exp(lr2 * dt)
    ab_re = mag * jnp.cos(li2 * dt)
    ab_im = mag * jnp.sin(li2 * dt)
    den = lr2 * lr2 + li2 * li2
    nr, ni = ab_re - 1.0, ab_im
    f_re, f_im = (nr * lr2 + ni * li2) / den, (ni * lr2 - nr * li2) / den

    def powers(expo):
        p_re, p_im = jnp.ones(expo.shape, F32), jnp.zeros(expo.shape, F32)
        s_re, s_im = ab_re, ab_im
        for k in range(Q5.bit_length()):
            take = (lax.shift_right_logical(expo, k) & 1) == 1
            p_re, p_im = (jnp.where(take, p_re * s_re - p_im * s_im, p_re),
                          jnp.where(take, p_re * s_im + p_im * s_re, p_im))
            s_re, s_im = s_re * s_re - s_im * s_im, 2.0 * s_re * s_im
        return p_re, p_im

    lane1 = _iota((1, LANES), 1) < S5_STATE
    lo16 = _iota((S5_CH, LANES), 1) < S5_STATE
    bb_a = f_re * bt2re - f_im * bt2im
    bb_b = f_re * bt2im + f_im * bt2re
    bbt = jnp.where(lo16, bb_a, bb_b)
    bbt_sw = jnp.where(lo16, bb_b, bb_a)

    rev = SUBLANES * ((Q5 + SUBLANES) // SUBLANES)
    r = _iota((rev + Q5, LANES), 0)
    tab_re, tab_im = powers(jnp.where(r < rev, r, (rev + Q5 - 1) - r))
    aq_re, aq_im = tab_re[Q5:Q5 + 1, :], tab_im[Q5:Q5 + 1, :]
    a1_re, a1_im = tab_re[1:2, :], tab_im[1:2, :]
    aq_ref[...] = jnp.concatenate(
        [aq_re, jnp.where(lane1, -aq_im, aq_im), jnp.where(lane1, aq_im, -aq_im),
         jnp.zeros((5, LANES), F32)], axis=0)

    a1c_ref[0] = _col_bcast(a1_re)[0:S5_STATE, :]
    a1c_ref[1] = _col_bcast(a1_im)[0:S5_STATE, :]
    bbt_ref[...] = bbt
    cneg_ref[...] = jnp.where(lo16, c2re, -c2im)
    dcb_ref[...] = _col_bcast(drow[:, R5 - LANES:])[LANES - S5_CH:, :]

    ain_re, ain_im = _repeat_channels(tab_re[rev:rev + Q5, :]), _repeat_channels(tab_im[rev:rev + Q5, :])
    sign = jnp.where(lane1, -1.0, 1.0)
    ws_t = ain_re * _tile_steps(bbt) + (sign * ain_im) * _tile_steps(bbt_sw)
    ws = jnp.transpose(ws_t)
    ws_ref[...] = ws.astype(BF16)

    upper = _iota((LANES, R5), 0) < S5_STATE
    vmat = _dot_split(ccat, jnp.where(upper, ws, -ws))
    vrow = _iota((S5_CH, R5), 0)
    vcol = _iota((S5_CH, R5), 1)
    vmat = vmat + jnp.where(vcol == (Q5 - 1) * S5_CH + vrow, drow, 0.0)
    for l in range(Q5):
        width = (l + 1) * S5_CH
        rolled = vmat if width == R5 else pltpu.roll(vmat, width, axis=1)
        t_ref[l * S5_CH:(l + 1) * S5_CH, :] = jnp.where(vcol < width, rolled, 0.0).astype(BF16)

    aout_re, aout_im = _repeat_channels(tab_re[1:Q5 + 1, :]), _repeat_channels(tab_im[1:Q5 + 1, :])
    c2re_t, c2im_t = _tile_steps(c2re), _tile_steps(c2im)
    wo_ref[...] = jnp.where(_iota((R5, LANES), 1) < S5_STATE,
                            c2re_t * aout_re - c2im_t * aout_im,
                            -(c2re_t * aout_im + c2im_t * aout_re)).astype(BF16)


def _s5_prep(lam_re, lam_im, log_dt, b_re, b_im, c_re, c_im, d):
    G, P, C = S5_GROUPS, S5_STATE, S5_CH
    args = (lam_re, lam_im, log_dt.reshape(G, 1), c_re, c_im, d.reshape(G, C), b_re, b_im)
    spec = lambda a: pl.BlockSpec((GP,) + a.shape[1:], lambda g, nd=a.ndim: (g,) + (0,) * (nd - 1))
    out_shapes = (
        jax.ShapeDtypeStruct((G, 2 * P, R5), BF16),
        jax.ShapeDtypeStruct((G, R5, R5), BF16),
        jax.ShapeDtypeStruct((G, R5, 2 * P), BF16),
        jax.ShapeDtypeStruct((G, 8, 2 * P), F32),
        jax.ShapeDtypeStruct((G, 2, P, LANES), F32),
        jax.ShapeDtypeStruct((G, C, 2 * P), F32),
        jax.ShapeDtypeStruct((G, C, 2 * P), F32),
        jax.ShapeDtypeStruct((G, C, LANES), F32),
    )
    return pl.pallas_call(
        _s5_prep_kernel, grid=(G // GP,),
        in_specs=[spec(a) for a in args],
        out_specs=[pl.BlockSpec((GP,) + s.shape[1:], lambda g, nd=len(s.shape): (g,) + (0,) * (nd - 1))
                   for s in out_shapes],
        out_shape=out_shapes, compiler_params=_cp("arbitrary"), name="s5_prep",
    )(*args)


def _s5_inproj_kernel(x_ref, nw_ref, wut_ref, ut_ref):
    x = x_ref[...].reshape(M5, D_MODEL)
    hn = _rms(x, nw_ref[...]).astype(BF16)
    ut = _dot_nt(wut_ref[...], hn)
    ut_ref[...] = ut.astype(BF16).reshape(S5_GROUPS, S5_CH, M5)


def _s5_core_kernel(ut_ref, t_ref, ws_ref, wo_ref, aq_ref, y_ref, hfin_ref,
                    hs_scr, sp_scr, sq_scr, yin_scr):
    for i in range(GC):
        ut = ut_ref[i]
        sp = jnp.transpose(_dot(ws_ref[i], ut))
        sp_scr[i] = sp
        sq_scr[i] = pltpu.roll(sp, S5_STATE, axis=1)
        yin_scr[i] = _dot(t_ref[i], ut)
    for i in range(GC):
        a1 = jnp.broadcast_to(aq_ref[i, 0:1, :], (BATCH, LANES))
        a2 = jnp.broadcast_to(aq_ref[i, 1:2, :], (BATCH, LANES))
        a3 = jnp.broadcast_to(aq_ref[i, 2:3, :], (BATCH, LANES))
        hp = jnp.zeros((BATCH, LANES), F32)
        hq = jnp.zeros((BATCH, LANES), F32)
        for k in range(NC5):
            rows = pl.ds(k * BATCH, BATCH)
            hs_scr[i, rows, :] = hp
            hp, hq = (a1 * hp + a2 * hq + sp_scr[i, rows, :],
                      a1 * hq + a3 * hp + sq_scr[i, rows, :])
        hfin_ref[i] = hp
    for i in range(GC):
        y = yin_scr[i] + _dot_nt(wo_ref[i], hs_scr[i].astype(BF16))
        y_ref[i] = y.astype(BF16)


def _s5_out_kernel(g_ref, wglut_ref, bglu_ref, nw_ref, wout_ref, o_ref):
    g = jax.nn.gelu(g_ref[...].astype(F32).reshape(D_S5, M5))
    z = _dot(wglut_ref[...], g.astype(BF16)) + bglu_ref[...]
    out = g * jax.nn.sigmoid(z)
    ms = jnp.mean(out * out, axis=0, keepdims=True)
    y5 = out * lax.rsqrt(ms + EPS) * nw_ref[...]
    o = _dot_tn(y5.astype(BF16), wout_ref[...])
    o_ref[...] = o.reshape(NC5, BATCH, D_MODEL)


def _s5_prompt(x_prompt, norm_mix_w, w_int, prep, wglut, b_glu, s5_norm_w, w_out_b):
    ws, tmat, wo, aq = prep[:4]
    x_v = jnp.transpose(x_prompt, (1, 0, 2)).reshape(NC5, Q5, BATCH, D_MODEL)
    tok_blk = pl.BlockSpec((NC5, None, BATCH, D_MODEL), lambda s: (0, s, 0, 0))
    chan_blk = pl.BlockSpec((S5_GROUPS, S5_CH, M5), lambda s: (0, s, 0))
    const2 = lambda s: (0, 0)
    once = pl.Buffered(1)
    ut = pl.pallas_call(
        _s5_inproj_kernel, grid=(Q5,),
        in_specs=[tok_blk,
                  pl.BlockSpec((1, D_MODEL), const2),
                  pl.BlockSpec((D_S5, D_MODEL), const2, pipeline_mode=once)],
        out_specs=chan_blk,
        out_shape=jax.ShapeDtypeStruct((S5_GROUPS, R5, M5), BF16),
        compiler_params=_cp("arbitrary"), name="s5_inproj",
    )(x_v, norm_mix_w.reshape(1, D_MODEL), w_int)

    per_g = lambda shape: pl.BlockSpec((GC,) + shape, lambda g: (g, 0, 0))
    y5_pre, hfin = pl.pallas_call(
        _s5_core_kernel, grid=(S5_GROUPS // GC,),
        in_specs=[per_g((R5, M5)), per_g((R5, R5)), per_g((LANES, R5)), per_g((R5, LANES)),
                  per_g((8, LANES))],
        out_specs=[per_g((R5, M5)), per_g((BATCH, LANES))],
        out_shape=(jax.ShapeDtypeStruct((S5_GROUPS, R5, M5), BF16),
                   jax.ShapeDtypeStruct((S5_GROUPS, BATCH, LANES), F32)),
        scratch_shapes=[pltpu.VMEM((GC, M5, LANES), F32) for _ in range(3)]
        + [pltpu.VMEM((GC, R5, M5), F32)],
        compiler_params=_cp("arbitrary"), name="s5_core",
    )(ut, tmat, ws, wo, aq)

    o5 = pl.pallas_call(
        _s5_out_kernel, grid=(Q5,),
        in_specs=[chan_blk,
                  pl.BlockSpec((D_S5, D_S5), const2, pipeline_mode=once),
                  pl.BlockSpec((D_S5, 1), const2),
                  pl.BlockSpec((D_S5, 1), const2),
                  pl.BlockSpec((D_S5, D_MODEL), const2, pipeline_mode=once)],
        out_specs=tok_blk,
        out_shape=jax.ShapeDtypeStruct((NC5, Q5, BATCH, D_MODEL), F32),
        compiler_params=_cp("arbitrary"), name="s5_out",
    )(y5_pre, wglut, b_glu.reshape(D_S5, 1), s5_norm_w.reshape(D_S5, 1), w_out_b)
    return o5.reshape(SEQ, BATCH * D_MODEL), hfin


assert D_SSD == 2 * XBC_BLK and D_CONV == 3 * XBC_BLK and 2 * SSD_STATE * 2 == XBC_BLK


def _ssd_ffn_kernel(x_ref, o5_ref, nw_ref, wz_ref, wxbc0_ref, wxbc1_ref, wxbc2_ref, wdt_ref,
                    cw_ref, cb_ref, dtb_ref, alog_ref, drep_ref, snw_ref, wout_ref,
                    fnw_ref, w1_ref, w2_ref, fw_ref, x1s_ref,
                    y_ref, hout_ref, cout_ref, ys_ref,
                    xp_scr, xs_scr, b_scr, c_scr, z_scr, a_scr, dt_scr, y_scr, h_scr, x1_scr,
                    hf_scr, h1_scr, acc_scr):
    wxbc_refs = (wxbc0_ref, wxbc1_ref, wxbc2_ref)
    t = pl.program_id(0)
    n_tiles = pl.num_programs(0) - 1
    j = lax.rem(t, SEQ // TL)

    @pl.when(t == 0)
    def _():
        x1s = x1s_ref[...]
        x1_scr[0:DEC_BATCH, :] = x1s
        x1_scr[DEC_BATCH:, :] = jnp.zeros((TL - DEC_BATCH, D_MODEL), F32)
        hf_scr[0:DEC_BATCH, :] = _rms(x1s, fnw_ref[...]).astype(BF16)
        hf_scr[DEC_BATCH:, :] = jnp.zeros((TL - DEC_BATCH, D_MODEL), BF16)

    @pl.when(j == 0)
    def _():
        xp_scr[0:SUBLANES, :] = jnp.zeros((SUBLANES, D_CONV), F32)
        h_scr[...] = jnp.zeros(h_scr.shape, F32)

    blk = D_FF // 4
    sl = blk // 4

    def ffn_up(c, q):
        cols = slice(c * blk + q * sl, c * blk + (q + 1) * sl)
        h1 = jnp.square(jnp.maximum(_dot(hf_scr[...], w1_ref[:, cols]), 0.0))
        h1_scr[:, q * sl:(q + 1) * sl] = h1.astype(BF16)

    def ffn_down(c, q):
        cols = slice(q * sl, (q + 1) * sl)
        acc_scr[:, cols] += _dot(h1_scr[...], w2_ref[c * blk:(c + 1) * blk, cols])

    ffn_slices = iter([functools.partial(f, c, q) for c in range(4)
                       for f in (ffn_up, ffn_down) for q in range(4)])

    def ffn_step(n=1):
        for _ in range(n):
            next(ffn_slices)()

    ffn_step(2)

    x = x_ref[...]
    hn = _rms(x, nw_ref[...]).astype(BF16)
    acc_scr[...] = x1_scr[...]

    def conv_block(i):
        cols = slice(i * XBC_BLK, (i + 1) * XBC_BLK)
        xp_scr[SUBLANES:, cols] = _dot_nt(hn, wxbc_refs[i][...])
        conv = cb_ref[:, cols] + sum(
            xp_scr[SUBLANES - (SSD_CONV - 1) + k:SUBLANES - (SSD_CONV - 1) + k + TL, cols]
            * cw_ref[k:k + 1, cols] for k in range(SSD_CONV))
        tail = xp_scr[TL:TL + SUBLANES, cols]
        xp_scr[0:SUBLANES, cols] = tail
        cout_ref[:, cols] = tail
        return _silu(conv)

    xs_scr[:, 0:XBC_BLK] = conv_block(0)
    xs_scr[:, XBC_BLK:2 * XBC_BLK] = conv_block(1)
    bc = conv_block(2)
    b_scr[...] = bc[:, :2 * SSD_STATE]
    c_scr[...] = bc[:, 2 * SSD_STATE:]
    z_scr[...] = _dot_nt(hn, wz_ref[...])
    wdt = jnp.concatenate([wdt_ref[...], jnp.zeros((LANES - SSD_HEADS, D_MODEL), BF16)], axis=0)
    dt = jax.nn.softplus(_dot_nt(hn, wdt) + dtb_ref[...])
    dt_scr[...] = dt
    a_scr[...] = dt * (-jnp.exp(alog_ref[...]) * LOG2_E)
    ffn_step(5)

    li = _iota((QS, QS), 0)
    si = _iota((QS, QS), 1)
    causal = li >= si
    tril = causal.astype(F32)
    lo = si < SSD_HEAD_DIM

    def chunk(ci, carry):
        rows = pl.ds(ci * QS, QS)
        acum = _dot(tril, a_scr[rows, :], HI)
        acum_t = jnp.transpose(acum)
        asrc_t = acum_t - jnp.log2(jnp.transpose(dt_scr[rows, :]))
        alast = acum[QS - 1:QS, :]
        alast_t = acum_t[:, QS - 1:QS]
        for g in range(2):
            bg = b_scr[rows, g * SSD_STATE:(g + 1) * SSD_STATE]
            cg_b = c_scr[rows, g * SSD_STATE:(g + 1) * SSD_STATE].astype(BF16)
            bt = jnp.transpose(bg)
            cb = _dot(cg_b, bt.astype(BF16))
            h_grp = jnp.concatenate([h_scr[4 * g + i] for i in range(4)], axis=1)
            y_off = _dot(cg_b, h_grp.astype(BF16))
            for hp in range(4):
                pr = 4 * g + hp
                cols = slice(pr * LANES, (pr + 1) * LANES)
                xs_pair = xs_scr[rows, cols]
                xs_b = xs_pair.astype(BF16)
                zero = jnp.zeros_like(xs_b)
                xs_cat = jnp.concatenate([jnp.where(lo, xs_b, zero), jnp.where(lo, zero, xs_b)], axis=0)
                acols, ms, btss = [], [], []
                for h in (2 * pr, 2 * pr + 1):
                    acol = jnp.broadcast_to(acum[:, h:h + 1], (QS, QS))
                    asrc = asrc_t[h:h + 1, :]
                    lmat = jnp.exp2(jnp.where(causal, acol - asrc, -1e30))
                    ms.append((cb * lmat).astype(BF16))
                    btss.append((bt * jnp.exp2(alast_t[h:h + 1, :] - asrc)).astype(BF16))
                    acols.append(acol)
                y = _dot(jnp.concatenate(ms, axis=1), xs_cat)
                st = _dot(jnp.concatenate(btss, axis=1), xs_cat)
                h0, h1 = 2 * pr, 2 * pr + 1
                decay = jnp.exp2(jnp.where(lo[0:1, :], alast[:, h0:h0 + 1], alast[:, h1:h1 + 1]))
                h_scr[pr] = decay * h_scr[pr] + st
                y = (y + jnp.exp2(jnp.where(lo, acols[0], acols[1])) * y_off[:, hp * LANES:(hp + 1) * LANES]
                     + drep_ref[:, cols] * xs_pair)
                y_scr[rows, cols] = y * _silu(z_scr[rows, cols])
                if hp != 3 or (ci == 0 and g == 0):
                    ffn_step()
        half = D_SSD // 2
        for g in range(2):
            yg = y_scr[rows, g * half:(g + 1) * half]
            yg = yg * lax.rsqrt(jnp.mean(yg * yg, axis=-1, keepdims=True) + EPS)
            y_scr[rows, g * half:(g + 1) * half] = yg * snw_ref[:, g * half:(g + 1) * half]
        return carry

    for ci in range(TL // QS):
        chunk(ci, 0)
    assert next(ffn_slices, None) is None, "every FFN slice must have been issued"
    y_ref[...] = _rms(acc_scr[...], fw_ref[...])
    for hrows in (slice(0, TL // 2), slice(TL // 2, TL)):
        x1 = x[hrows] + o5_ref[hrows, :] + _dot(y_scr[hrows, :].astype(BF16), wout_ref[...])
        x1_scr[hrows, :] = x1
        hf_scr[hrows, :] = _rms(x1, fnw_ref[...]).astype(BF16)

    @pl.when(t == 0)
    def _():
        ys_ref[...] = y_ref[0:DEC_BATCH, :]

    @pl.when(jnp.logical_and(j == SEQ // TL - 1, t < n_tiles))
    def _():
        for pr in range(SSD_HEADS // 2):
            hout_ref[2 * pr:2 * pr + 2] = jnp.transpose(h_scr[pr]).reshape(2, SSD_HEAD_DIM, SSD_STATE)


def _ssd_ffn_prompt(x_prompt, o5, norm_mix_w, w_int, conv_w, conv_b, dtb, alog, drep, snw,
                    w_out_b, norm_ffn_w, w1, w2, norm_final_w, x1_sample):
    nj = SEQ // TL
    n_tiles = BATCH * nj
    c2 = lambda t: (0, 0)
    once = pl.Buffered(1)
    row = lambda n: pl.BlockSpec((1, n), c2, pipeline_mode=once)
    mat = lambda r, c: pl.BlockSpec((r, c), c2, pipeline_mode=once)
    wrows = lambda n, i: pl.BlockSpec((n, D_MODEL), lambda t: (i, 0), pipeline_mode=once)
    xbc0 = (D_S5 + D_SSD) // XBC_BLK

    def tile(t):
        tt = jnp.minimum(t, n_tiles - 1)
        return tt // nj, lax.rem(tt, nj)

    def prev(t):
        tp = jnp.maximum(t - 1, 0)
        return tp // nj, lax.rem(tp, nj)

    return pl.pallas_call(
        _ssd_ffn_kernel, grid=(n_tiles + 1,),
        in_specs=[pl.BlockSpec((None, TL, D_MODEL), lambda t: (*tile(t), 0)),
                  pl.BlockSpec((TL, D_MODEL), lambda t: tile(t)[::-1]),
                  row(D_MODEL),
                  wrows(D_SSD, D_S5 // D_SSD),
                  wrows(XBC_BLK, xbc0), wrows(XBC_BLK, xbc0 + 1), wrows(XBC_BLK, xbc0 + 2),
                  wrows(SSD_HEADS, (D_S5 + D_SSD + D_CONV) // SSD_HEADS),
                  mat(SSD_CONV, D_CONV), row(D_CONV),
                  row(LANES), row(LANES), row(D_SSD), row(D_SSD),
                  wrows(D_SSD, D_S5 // D_SSD),
                  row(D_MODEL), mat(D_MODEL, D_FF), mat(D_FF, D_MODEL), row(D_MODEL),
                  mat(DEC_BATCH, D_MODEL)],
        out_specs=[pl.BlockSpec((None, TL, D_MODEL), lambda t: (*prev(t), 0)),
                   pl.BlockSpec((None, SSD_HEADS, SSD_HEAD_DIM, SSD_STATE), lambda t: (tile(t)[0], 0, 0, 0)),
                   pl.BlockSpec((None, SUBLANES, D_CONV), lambda t: (tile(t)[0], 0, 0)),
                   pl.BlockSpec((DEC_BATCH, D_MODEL), c2)],
        out_shape=(jax.ShapeDtypeStruct((BATCH, SEQ, D_MODEL), F32),
                   jax.ShapeDtypeStruct((BATCH, SSD_HEADS, SSD_HEAD_DIM, SSD_STATE), F32),
                   jax.ShapeDtypeStruct((BATCH, SUBLANES, D_CONV), F32),
                   jax.ShapeDtypeStruct((DEC_BATCH, D_MODEL), F32)),
        scratch_shapes=[pltpu.VMEM((SUBLANES + TL, D_CONV), F32),
                        pltpu.VMEM((TL, D_SSD), F32),
                        pltpu.VMEM((TL, 2 * SSD_STATE), F32),
                        pltpu.VMEM((TL, 2 * SSD_STATE), F32),
                        pltpu.VMEM((TL, D_SSD), F32),
                        pltpu.VMEM((TL, LANES), F32),
                        pltpu.VMEM((TL, LANES), F32),
                        pltpu.VMEM((TL, D_SSD), F32),
                        pltpu.VMEM((SSD_HEADS // 2, SSD_STATE, LANES), F32),
                        pltpu.VMEM((TL, D_MODEL), F32),
                        pltpu.VMEM((TL, D_MODEL), BF16),
                        pltpu.VMEM((TL, D_FF // 4), BF16),
                        pltpu.VMEM((TL, D_MODEL), F32)],
        compiler_params=pltpu.CompilerParams(dimension_semantics=("arbitrary",),
                                             vmem_limit_bytes=VMEM_LIMIT_FUSED),
        name="ssd_ffn",
    )(x_prompt, o5, norm_mix_w.reshape(1, D_MODEL), w_int, w_int, w_int, w_int, w_int,
      conv_w, conv_b, dtb, alog, drep, snw, w_out_b,
      norm_ffn_w.reshape(1, D_MODEL), w1, w2, norm_final_w.reshape(1, D_MODEL), x1_sample)


def _sample_inproj_kernel(x_ref, nw_ref, win_ref, cw_ref, cb_ref, cbuf_ref, dtb_ref, alog_ref,
                          u_ref, z_ref, xs_ref, xdt_ref, b_ref, c_ref, dec_ref, nconv_ref):
    hn = _rms(x_ref[...], nw_ref[...]).astype(BF16)
    o_z, o_xbc, o_dt = D_S5, D_S5 + D_SSD, D_S5 + D_SSD + D_CONV
    u_ref[...] = _dot_nt(win_ref[0:o_z, :], hn)
    z_ref[...] = _dot_nt(hn, win_ref[o_z:o_xbc, :])
    xbc_t = _dot_nt(win_ref[o_xbc:o_dt, :], hn)
    wdt = jnp.concatenate([win_ref[o_dt:o_dt + SSD_HEADS, :],
                           jnp.zeros((LANES - SSD_HEADS, D_MODEL), BF16)], axis=0)
    dt = jax.nn.softplus(_dot_nt(hn, wdt) + dtb_ref[...])
    dec_ref[...] = jnp.exp(dt * (-jnp.exp(alog_ref[...])))
    head_of_col = lax.shift_right_logical(_iota((LANES, D_SSD), 1), 6)
    dt_rep = _dot(dt, (_iota((LANES, D_SSD), 0) == head_of_col).astype(F32), HI)
    cw_t = jnp.transpose(jnp.concatenate(
        [cw_ref[...], cb_ref[...], jnp.zeros((LANES - SSD_CONV - 1, D_CONV), F32)], axis=0))
    conv_t = cw_t[:, SSD_CONV:SSD_CONV + 1] + xbc_t * cw_t[:, SSD_CONV - 1:SSD_CONV]
    for k in range(SSD_CONV - 1):
        conv_t = conv_t + cbuf_ref[k] * cw_t[:, k:k + 1]
        if k > 0:
            nconv_ref[k - 1] = cbuf_ref[k]
    nconv_ref[SSD_CONV - 2] = xbc_t
    conv = jnp.transpose(_silu(conv_t))
    xs_ref[...] = conv[:, :D_SSD]
    xdt_ref[...] = conv[:, :D_SSD] * dt_rep
    b_ref[...] = conv[:, D_SSD:D_SSD + 2 * SSD_STATE]
    c_ref[...] = conv[:, D_SSD + 2 * SSD_STATE:]


def _sample_s5_kernel(u_ref, hre_ref, him_ref, a1c_ref, bbt_ref, cneg_ref, dcb_ref,
                      nre_ref, nim_ref, y_ref):
    for i in range(GS):
        u = u_ref[i]
        bu = _dot_tn(bbt_ref[i], u, HI)
        a_re, a_im = a1c_ref[i, 0], a1c_ref[i, 1]
        h_re, h_im = hre_ref[i], him_ref[i]
        n_re = a_re * h_re - a_im * h_im + bu[0:S5_STATE]
        n_im = a_re * h_im + a_im * h_re + bu[S5_STATE:]
        nre_ref[i] = n_re
        nim_ref[i] = n_im
        y_ref[i] = _dot(cneg_ref[i], jnp.concatenate([n_re, n_im], axis=0), HI) + dcb_ref[i] * u


def _sample_ssd_kernel(dec_ref, h0_ref, xdt_ref, xs_ref, b_ref, c_ref, drep_ref, hn_ref, y_ref):
    blk = pl.program_id(0)
    hpg = SSD_HEADS // 2
    half = D_SSD // 2
    xt = jnp.transpose(jnp.concatenate(
        [xdt_ref[...], jnp.zeros((LANES - SB, D_SSD), F32)], axis=0))
    c_b = c_ref[...].astype(BF16)
    rowi = _iota((SB, half), 0)
    ys = [jnp.zeros((SB, half), F32) for _ in range(2)]
    for jj in range(SB):
        seq = blk * SB + jj
        for g in range(2):
            brow = b_ref[jj:jj + 1, g * SSD_STATE:(g + 1) * SSD_STATE]
            parts = []
            for h in range(g * hpg, (g + 1) * hpg):
                xcol = xt[h * SSD_HEAD_DIM:(h + 1) * SSD_HEAD_DIM, jj:jj + 1]
                hn = dec_ref[seq, h] * h0_ref[jj, h] + xcol * brow
                hn_ref[jj, h] = hn
                parts.append(hn.astype(BF16))
            y_all = _dot_nt(c_b[:, g * SSD_STATE:(g + 1) * SSD_STATE], jnp.concatenate(parts, axis=0))
            ys[g] = jnp.where(rowi == jj, y_all, ys[g])
    y_ref[...] = jnp.concatenate(ys, axis=1) + drep_ref[...] * xs_ref[...]


def _sample_mix_kernel(x_ref, y5_ref, ys_ref, z_ref, wglut_ref, bglu_ref, nw5_ref, snw_ref,
                       wout_ref, x1_ref):
    g = jax.nn.gelu(jnp.transpose(y5_ref[...]))
    out = g * jax.nn.sigmoid(_dot_nt(g.astype(BF16), wglut_ref[...]) + bglu_ref[...])
    y5 = _rms(out, nw5_ref[...])
    y = ys_ref[...] * _silu(z_ref[...])
    half = D_SSD // 2
    yn = jnp.concatenate(
        [_rms(y[:, i * half:(i + 1) * half], snw_ref[:, i * half:(i + 1) * half]) for i in range(2)],
        axis=1)
    x1_ref[...] = (x_ref[...] + _dot(y5.astype(BF16), wout_ref[0:D_S5, :])
                   + _dot(yn.astype(BF16), wout_ref[D_S5:, :]))


def _sample_layer(x_sample, st_re, st_im, st_ssd, st_conv, norm_mix_w, w_int,
                  conv_w, conv_b, dtb, alog, drep, snw, prep, wglut, b_glu, s5_norm_w,
                  w_out_b):
    nb = DEC_BATCH
    a1c, bbt, cneg, dcb = prep[4:8]
    xs2 = x_sample.reshape(nb, D_MODEL)
    sds = lambda *s: jax.ShapeDtypeStruct(s, F32)
    u_t, z, xs, xdt, bm, cm, dec, nconv_t = pl.pallas_call(
        _sample_inproj_kernel,
        out_shape=(sds(D_S5, nb), sds(nb, D_SSD), sds(nb, D_SSD), sds(nb, D_SSD),
                   sds(nb, 2 * SSD_STATE), sds(nb, 2 * SSD_STATE), sds(nb, LANES),
                   sds(SSD_CONV - 1, D_CONV, nb)),
        compiler_params=_cp(), name="sample_inproj",
    )(xs2, norm_mix_w.reshape(1, D_MODEL), w_int, conv_w, conv_b,
      jnp.transpose(st_conv, (1, 2, 0)), dtb, alog)

    def per_g(shape):
        nd = len(shape)
        return pl.BlockSpec((GS,) + shape, lambda g: (g,) + (0,) * nd)

    P = S5_STATE
    n_re_t, n_im_t, y5_t = pl.pallas_call(
        _sample_s5_kernel, grid=(S5_GROUPS // GS,),
        in_specs=[per_g((S5_CH, nb)), per_g((P, nb)), per_g((P, nb)), per_g((2, P, LANES)),
                  per_g((S5_CH, 2 * P)), per_g((S5_CH, 2 * P)), per_g((S5_CH, LANES))],
        out_specs=[per_g((P, nb)), per_g((P, nb)), per_g((S5_CH, nb))],
        out_shape=(sds(S5_GROUPS, P, nb), sds(S5_GROUPS, P, nb), sds(S5_GROUPS, S5_CH, nb)),
        compiler_params=_cp("arbitrary"), name="sample_s5",
    )(u_t.reshape(S5_GROUPS, S5_CH, nb), jnp.transpose(st_re, (1, 2, 0)),
      jnp.transpose(st_im, (1, 2, 0)), a1c, bbt, cneg, dcb)
    y5 = y5_t.reshape(D_S5, nb)

    smem = pl.BlockSpec(memory_space=pltpu.SMEM)
    blk2 = lambda n: pl.BlockSpec((SB, n), lambda i: (i, 0))
    st_spec = pl.BlockSpec((SB, SSD_HEADS, SSD_HEAD_DIM, SSD_STATE), lambda i: (i, 0, 0, 0))
    hn_ssd, ys = pl.pallas_call(
        _sample_ssd_kernel, grid=(nb // SB,),
        in_specs=[smem, st_spec, blk2(D_SSD), blk2(D_SSD), blk2(2 * SSD_STATE),
                  blk2(2 * SSD_STATE), pl.BlockSpec((1, D_SSD), lambda i: (0, 0))],
        out_specs=[st_spec, blk2(D_SSD)],
        out_shape=(sds(nb, SSD_HEADS, SSD_HEAD_DIM, SSD_STATE), sds(nb, D_SSD)),
        compiler_params=_cp("arbitrary"), name="sample_ssd",
    )(dec[:, :SSD_HEADS], st_ssd, xdt, xs, bm, cm, drep)

    x1 = pl.pallas_call(
        _sample_mix_kernel, out_shape=sds(nb, D_MODEL),
        compiler_params=_cp(), name="sample_mix",
    )(xs2, y5, ys, z, wglut, b_glu.reshape(1, D_S5), s5_norm_w.reshape(1, D_S5), snw, w_out_b)
    to_seq_major = lambda a: jnp.transpose(a, (2, 0, 1))
    return x1, to_seq_major(n_re_t), to_seq_major(n_im_t), hn_ssd, to_seq_major(nconv_t)


def kernel(x_prompt, x_sample, state_s5_re, state_s5_im, state_ssd, state_conv, norm_mix_w, w_in, s5_lam_re, s5_lam_im, s5_log_dt, s5_b_re, s5_b_im, s5_c_re, s5_c_im, s5_d, s5_w_glu, s5_b_glu, s5_norm_w, ssd_conv_w, ssd_conv_b, ssd_a_log, ssd_dt_bias, ssd_d, ssd_norm_w, w_out, norm_ffn_w, w_ff1, w_ff2, norm_final_w):
    P = S5_STATE
    w_int = jnp.transpose(w_in[0]).astype(BF16)
    wglut = jnp.transpose(s5_w_glu[0]).astype(BF16)
    w_out_b = w_out[0].astype(BF16)
    w1 = w_ff1[0].astype(BF16)
    w2 = w_ff2[0].astype(BF16)
    pad_h = lambda v: jnp.pad(v.reshape(1, SSD_HEADS), ((0, 0), (0, LANES - SSD_HEADS)))
    dtb, alog = pad_h(ssd_dt_bias[0]), pad_h(ssd_a_log[0])
    drep = jnp.repeat(ssd_d[0], SSD_HEAD_DIM).reshape(1, D_SSD)
    snw = ssd_norm_w[0].reshape(1, D_SSD)
    conv_w, conv_b = ssd_conv_w[0], ssd_conv_b[0].reshape(1, D_CONV)

    prep = _s5_prep(s5_lam_re[0], s5_lam_im[0], s5_log_dt[0], s5_b_re[0], s5_b_im[0],
                    s5_c_re[0], s5_c_im[0], s5_d[0])

    x1s, ns_re, ns_im, hn_ssd, nconv = _sample_layer(
        x_sample, state_s5_re[0], state_s5_im[0], state_ssd[0], state_conv[0], norm_mix_w[0],
        w_int, conv_w, conv_b, dtb, alog, drep, snw, prep, wglut,
        s5_b_glu[0], s5_norm_w[0], w_out_b)

    o5, hfin5 = _s5_prompt(x_prompt, norm_mix_w[0], w_int, prep, wglut, s5_b_glu[0], s5_norm_w[0],
                           w_out_b)
    y_prompt, h_ssd, ctail, y_s = _ssd_ffn_prompt(
        x_prompt, o5, norm_mix_w[0], w_int, conv_w, conv_b, dtb, alog, drep, snw, w_out_b,
        norm_ffn_w[0], w1, w2, norm_final_w, x1s)
    hfin5 = jnp.transpose(hfin5, (1, 0, 2))
    np_re, np_im = hfin5[None, :, :, :P], hfin5[None, :, :, P:]
    np_ssd = h_ssd[None]
    np_conv = ctail[None, :, SUBLANES - (SSD_CONV - 1):, :]

    y_sample = y_s.reshape(DEC_BATCH, 1, D_MODEL)
    ns_re, ns_im = ns_re[None], ns_im[None]
    ns_ssd = hn_ssd[None]
    ns_conv = nconv[None]

    return (y_prompt, y_sample, np_re, np_im, np_ssd, np_conv, ns_re, ns_im, ns_ssd, ns_conv)
```

```python
import functools

import jax
import jax.numpy as jnp
from jax import lax
from jax.experimental import pallas as pl
from jax.experimental.pallas import tpu as pltpu

F32 = jnp.float32
BF16 = jnp.bfloat16
HI = lax.Precision.HIGHEST
EPS = 1e-5
LOG2_E = 1.4426950408889634

D_MODEL = 1024
BATCH = 8
SEQ = 2048
DEC_BATCH = 128
D_S5 = 1024
S5_CH = 16
S5_GROUPS = 64
S5_STATE = 64
D_SSD = 1024
SSD_HEADS = 16
SSD_HEAD_DIM = 64
SSD_STATE = 128
SSD_CONV = 4
D_CONV = 1536
D_FF = 4096

LANES = 128
SUBLANES = 8
Q5 = 16
NC5 = SEQ // Q5
M5 = NC5 * BATCH
R5 = Q5 * S5_CH
QS = 128
TL = 512
XBC_BLK = 512
SB = 16
GS = 8
GP = 8
GC = 8
VMEM_LIMIT = 56 * 1024 * 1024
VMEM_LIMIT_FUSED = 62 * 1024 * 1024


def _cp(*sem):
    return pltpu.CompilerParams(dimension_semantics=sem, vmem_limit_bytes=VMEM_LIMIT)


def _rms(x, w):
    return x * lax.rsqrt(jnp.mean(x * x, axis=-1, keepdims=True) + EPS) * w


def _dot(a, b, precision=None):
    return jnp.dot(a, b, preferred_element_type=F32, precision=precision)


def _dot_nt(a, b, precision=None):
    return lax.dot_general(a, b, (((1,), (1,)), ((), ())), preferred_element_type=F32,
                           precision=precision)


def _dot_tn(a, b, precision=None):
    return lax.dot_general(a, b, (((0,), (0,)), ((), ())), preferred_element_type=F32,
                           precision=precision)


def _iota(shape, dim):
    return lax.broadcasted_iota(jnp.int32, shape, dim)


def _silu(x):
    return x * jax.nn.sigmoid(x)


def _dot_split(a, b):
    a_hi = a.astype(BF16)
    b_hi = b.astype(BF16)
    a_lo = (a - a_hi.astype(F32)).astype(BF16)
    b_lo = (b - b_hi.astype(F32)).astype(BF16)
    return _dot(a_hi, b_hi) + _dot(a_lo, b_hi) + _dot(a_hi, b_lo)


def _repeat_channels(x):
    return jnp.broadcast_to(x.reshape(Q5, 1, LANES), (Q5, S5_CH, LANES)).reshape(R5, LANES)


def _tile_steps(x):
    return jnp.broadcast_to(x[None], (Q5, S5_CH, LANES)).reshape(R5, LANES)


def _s5_prep_kernel(lre_ref, lim_ref, ldt_ref, cre_ref, cim_ref, d_ref, bre_ref, bim_ref,
                    ws_ref, t_ref, wo_ref, aq_ref, a1c_ref, bbt_ref, cneg_ref, dcb_ref):
    dup = lambda a: jnp.concatenate([a, a], axis=1)
    for i in range(GP):
        c_re, c_im = cre_ref[i], cim_ref[i]
        drow = jnp.concatenate([jnp.zeros((1, R5 - S5_CH), F32), d_ref[i:i + 1, :]], axis=1)
        _s5_prep_group(dup(lre_ref[i:i + 1, :]), dup(lim_ref[i:i + 1, :]), ldt_ref[i:i + 1, :],
                       jnp.concatenate([c_re, c_im], axis=1), dup(c_re), dup(c_im), drow,
                       dup(jnp.transpose(bre_ref[i])), dup(jnp.transpose(bim_ref[i])),
                       ws_ref.at[i], t_ref.at[i], wo_ref.at[i], aq_ref.at[i], a1c_ref.at[i],
                       bbt_ref.at[i], cneg_ref.at[i], dcb_ref.at[i])


def _col_bcast(row):
    return jnp.transpose(jnp.broadcast_to(row, (LANES, LANES)))


def _s5_prep_group(lr2, li2, ldt, ccat, c2re, c2im, drow, bt2re, bt2im,
                   ws_ref, t_ref, wo_ref, aq_ref, a1c_ref, bbt_ref, cneg_ref, dcb_ref):
    dt = jnp.exp(ldt)

    mag = jnp.exp(lr2 * dt)
    ab_re = mag * jnp.cos(li2 * dt)
    ab_im = mag * jnp.sin(li2 * dt)
    den = lr2 * lr2 + li2 * li2
    nr, ni = ab_re - 1.0, ab_im
    f_re, f_im = (nr * lr2 + ni * li2) / den, (ni * lr2 - nr * li2) / den

    def powers(expo):
        p_re, p_im = jnp.ones(expo.shape, F32), jnp.zeros(expo.shape, F32)
        s_re, s_im = ab_re, ab_im
        for k in range(Q5.bit_length()):
            take = (lax.shift_right_logical(expo, k) & 1) == 1
            p_re, p_im = (jnp.where(take, p_re * s_re - p_im * s_im, p_re),
                          jnp.where(take, p_re * s_im + p_im * s_re, p_im))
            s_re, s_im = s_re * s_re - s_im * s_im, 2.0 * s_re * s_im
        return p_re, p_im

    lane1 = _iota((1, LANES), 1) < S5_STATE
    lo16 = _iota((S5_CH, LANES), 1) < S5_STATE
    bb_a = f_re * bt2re - f_im * bt2im
    bb_b = f_re * bt2im + f_im * bt2re
    bbt = jnp.where(lo16, bb_a, bb_b)
    bbt_sw = jnp.where(lo16, bb_b, bb_a)

    rev = SUBLANES * ((Q5 + SUBLANES) // SUBLANES)
    r = _iota((rev + Q5, LANES), 0)
    tab_re, tab_im = powers(jnp.where(r < rev, r, (rev + Q5 - 1) - r))
    aq_re, aq_im = tab_re[Q5:Q5 + 1, :], tab_im[Q5:Q5 + 1, :]
    a1_re, a1_im = tab_re[1:2, :], tab_im[1:2, :]
    aq_ref[...] = jnp.concatenate(
        [aq_re, jnp.where(lane1, -aq_im, aq_im), jnp.where(lane1, aq_im, -aq_im),
         jnp.zeros((5, LANES), F32)], axis=0)

    a1c_ref[0] = _col_bcast(a1_re)[0:S5_STATE, :]
    a1c_ref[1] = _col_bcast(a1_im)[0:S5_STATE, :]
    bbt_ref[...] = bbt
    cneg_ref[...] = jnp.where(lo16, c2re, -c2im)
    dcb_ref[...] = _col_bcast(drow[:, R5 - LANES:])[LANES - S5_CH:, :]

    ain_re, ain_im = _repeat_channels(tab_re[rev:rev + Q5, :]), _repeat_channels(tab_im[rev:rev + Q5, :])
    sign = jnp.where(lane1, -1.0, 1.0)
    ws_t = ain_re * _tile_steps(bbt) + (sign * ain_im) * _tile_steps(bbt_sw)
    ws = jnp.transpose(ws_t)
    ws_ref[...] = ws.astype(BF16)

    upper = _iota((LANES, R5), 0) < S5_STATE
    vmat = _dot_split(ccat, jnp.where(upper, ws, -ws))
    vrow = _iota((S5_CH, R5), 0)
    vcol = _iota((S5_CH, R5), 1)
    vmat = vmat + jnp.where(vcol == (Q5 - 1) * S5_CH + vrow, drow, 0.0)
    for l in range(Q5):
        width = (l + 1) * S5_CH
        rolled = vmat if width == R5 else pltpu.roll(vmat, width, axis=1)
        t_ref[l * S5_CH:(l + 1) * S5_CH, :] = jnp.where(vcol < width, rolled, 0.0).astype(BF16)

    aout_re, aout_im = _repeat_channels(tab_re[1:Q5 + 1, :]), _repeat_channels(tab_im[1:Q5 + 1, :])
    c2re_t, c2im_t = _tile_steps(c2re), _tile_steps(c2im)
    wo_ref[...] = jnp.where(_iota((R5, LANES), 1) < S5_STATE,
                            c2re_t * aout_re - c2im_t * aout_im,
                            -(c2re_t * aout_im + c2im_t * aout_re)).astype(BF16)


def _s5_prep(lam_re, lam_im, log_dt, b_re, b_im, c_re, c_im, d):
    G, P, C = S5_GROUPS, S5_STATE, S5_CH
    args = (lam_re, lam_im, log_dt.reshape(G, 1), c_re, c_im, d.reshape(G, C), b_re, b_im)
    spec = lambda a: pl.BlockSpec((GP,) + a.shape[1:], lambda g, nd=a.ndim: (g,) + (0,) * (nd - 1))
    out_shapes = (
        jax.ShapeDtypeStruct((G, 2 * P, R5), BF16),
        jax.ShapeDtypeStruct((G, R5, R5), BF16),
        jax.ShapeDtypeStruct((G, R5, 2 * P), BF16),
        jax.ShapeDtypeStruct((G, 8, 2 * P), F32),
        jax.ShapeDtypeStruct((G, 2, P, LANES), F32),
        jax.ShapeDtypeStruct((G, C, 2 * P), F32),
        jax.ShapeDtypeStruct((G, C, 2 * P), F32),
        jax.ShapeDtypeStruct((G, C, LANES), F32),
    )
    return pl.pallas_call(
        _s5_prep_kernel, grid=(G // GP,),
        in_specs=[spec(a) for a in args],
        out_specs=[pl.BlockSpec((GP,) + s.shape[1:], lambda g, nd=len(s.shape): (g,) + (0,) * (nd - 1))
                   for s in out_shapes],
        out_shape=out_shapes, compiler_params=_cp("arbitrary"), name="s5_prep",
    )(*args)


def _s5_inproj_kernel(x_ref, nw_ref, wut_ref, ut_ref):
    x = x_ref[...].reshape(M5, D_MODEL)
    hn = _rms(x, nw_ref[...]).astype(BF16)
    ut = _dot_nt(wut_ref[...], hn)
    ut_ref[...] = ut.astype(BF16).reshape(S5_GROUPS, S5_CH, M5)


def _s5_core_kernel(ut_ref, t_ref, ws_ref, wo_ref, aq_ref, y_ref, hfin_ref,
                    hs_scr, sp_scr, sq_scr, yin_scr):
    for i in range(GC):
        ut = ut_ref[i]
        sp = jnp.transpose(_dot(ws_ref[i], ut))
        sp_scr[i] = sp
        sq_scr[i] = pltpu.roll(sp, S5_STATE, axis=1)
        yin_scr[i] = _dot(t_ref[i], ut)
    for i in range(GC):
        a1 = jnp.broadcast_to(aq_ref[i, 0:1, :], (BATCH, LANES))
        a2 = jnp.broadcast_to(aq_ref[i, 1:2, :], (BATCH, LANES))
        a3 = jnp.broadcast_to(aq_ref[i, 2:3, :], (BATCH, LANES))
        hp = jnp.zeros((BATCH, LANES), F32)
        hq = jnp.zeros((BATCH, LANES), F32)
        for k in range(NC5):
            rows = pl.ds(k * BATCH, BATCH)
            hs_scr[i, rows, :] = hp
            hp, hq = (a1 * hp + a2 * hq + sp_scr[i, rows, :],
                      a1 * hq + a3 * hp + sq_scr[i, rows, :])
        hfin_ref[i] = hp
    for i in range(GC):
        y = yin_scr[i] + _dot_nt(wo_ref[i], hs_scr[i].astype(BF16))
        y_ref[i] = y.astype(BF16)


def _s5_out_kernel(g_ref, wglut_ref, bglu_ref, nw_ref, wout_ref, o_ref):
    g = jax.nn.gelu(g_ref[...].astype(F32).reshape(D_S5, M5))
    z = _dot(wglut_ref[...], g.astype(BF16)) + bglu_ref[...]
    out = g * jax.nn.sigmoid(z)
    ms = jnp.mean(out * out, axis=0, keepdims=True)
    y5 = out * lax.rsqrt(ms + EPS) * nw_ref[...]
    o = _dot_tn(y5.astype(BF16), wout_ref[...])
    o_ref[...] = o.reshape(NC5, BATCH, D_MODEL)


def _s5_prompt(x_prompt, norm_mix_w, w_int, prep, wglut, b_glu, s5_norm_w, w_out_b):
    ws, tmat, wo, aq = prep[:4]
    x_v = jnp.transpose(x_prompt, (1, 0, 2)).reshape(NC5, Q5, BATCH, D_MODEL)
    tok_blk = pl.BlockSpec((NC5, None, BATCH, D_MODEL), lambda s: (0, s, 0, 0))
    chan_blk = pl.BlockSpec((S5_GROUPS, S5_CH, M5), lambda s: (0, s, 0))
    const2 = lambda s: (0, 0)
    once = pl.Buffered(1)
    ut = pl.pallas_call(
        _s5_inproj_kernel, grid=(Q5,),
        in_specs=[tok_blk,
                  pl.BlockSpec((1, D_MODEL), const2),
                  pl.BlockSpec((D_S5, D_MODEL), const2, pipeline_mode=once)],
        out_specs=chan_blk,
        out_shape=jax.ShapeDtypeStruct((S5_GROUPS, R5, M5), BF16),
        compiler_params=_cp("arbitrary"), name="s5_inproj",
    )(x_v, norm_mix_w.reshape(1, D_MODEL), w_int)

    per_g = lambda shape: pl.BlockSpec((GC,) + shape, lambda g: (g, 0, 0))
    y5_pre, hfin = pl.pallas_call(
        _s5_core_kernel, grid=(S5_GROUPS // GC,),
        in_specs=[per_g((R5, M5)), per_g((R5, R5)), per_g((LANES, R5)), per_g((R5, LANES)),
                  per_g((8, LANES))],
        out_specs=[per_g((R5, M5)), per_g((BATCH, LANES))],
        out_shape=(jax.ShapeDtypeStruct((S5_GROUPS, R5, M5), BF16),
                   jax.ShapeDtypeStruct((S5_GROUPS, BATCH, LANES), F32)),
        scratch_shapes=[pltpu.VMEM((GC, M5, LANES), F32) for _ in range(3)]
        + [pltpu.VMEM((GC, R5, M5), F32)],
        compiler_params=_cp("arbitrary"), name="s5_core",
    )(ut, tmat, ws, wo, aq)

    o5 = pl.pallas_call(
        _s5_out_kernel, grid=(Q5,),
        in_specs=[chan_blk,
                  pl.BlockSpec((D_S5, D_S5), const2, pipeline_mode=once),
                  pl.BlockSpec((D_S5, 1), const2),
                  pl.BlockSpec((D_S5, 1), const2),
                  pl.BlockSpec((D_S5, D_MODEL), const2, pipeline_mode=once)],
        out_specs=tok_blk,
        out_shape=jax.ShapeDtypeStruct((NC5, Q5, BATCH, D_MODEL), F32),
        compiler_params=_cp("arbitrary"), name="s5_out",
    )(y5_pre, wglut, b_glu.reshape(D_S5, 1), s5_norm_w.reshape(D_S5, 1), w_out_b)
    return o5.reshape(SEQ, BATCH * D_MODEL), hfin


assert D_SSD == 2 * XBC_BLK and D_CONV == 3 * XBC_BLK and 2 * SSD_STATE * 2 == XBC_BLK


def _ssd_ffn_kernel(x_ref, o5_ref, nw_ref, wz_ref, wxbc0_ref, wxbc1_ref, wxbc2_ref, wdt_ref,
                    cw_ref, cb_ref, dtb_ref, alog_ref, drep_ref, snw_ref, wout_ref,
                    fnw_ref, w1a_ref, w1b_ref, w1c_ref, w1d_ref, w2a_ref, w2b_ref, w2c_ref, w2d_ref,
                    fw_ref, x1s_ref,
                    y_ref, hout_ref, cout_ref, ys_ref,
                    xp_scr, xs_scr, b_scr, c_scr, z_scr, a_scr, dt_scr, y_scr, h_scr, x1_scr,
                    hf_scr, h1_scr, acc_scr):
    wxbc_refs = (wxbc0_ref, wxbc1_ref, wxbc2_ref)
    t = pl.program_id(0)
    n_tiles = pl.num_programs(0) - 1
    j = lax.rem(t, SEQ // TL)

    @pl.when(t == 0)
    def _():
        x1s = x1s_ref[...]
        x1_scr[0:DEC_BATCH, :] = x1s
        x1_scr[DEC_BATCH:, :] = jnp.zeros((TL - DEC_BATCH, D_MODEL), F32)
        hf_scr[0:DEC_BATCH, :] = _rms(x1s, fnw_ref[...]).astype(BF16)
        hf_scr[DEC_BATCH:, :] = jnp.zeros((TL - DEC_BATCH, D_MODEL), BF16)

    @pl.when(j == 0)
    def _():
        xp_scr[0:SUBLANES, :] = jnp.zeros((SUBLANES, D_CONV), F32)
        h_scr[...] = jnp.zeros(h_scr.shape, F32)

    w1_refs = (w1a_ref, w1b_ref, w1c_ref, w1d_ref)
    w2_refs = (w2a_ref, w2b_ref, w2c_ref, w2d_ref)
    sl = D_FF // 16

    def ffn_up(c, q):
        cols = slice(q * sl, (q + 1) * sl)
        h1 = jnp.square(jnp.maximum(_dot(hf_scr[...], w1_refs[c][:, cols]), 0.0))
        h1_scr[:, cols] = h1.astype(BF16)

    def ffn_down(c, q):
        cols = slice(q * sl, (q + 1) * sl)
        acc_scr[:, cols] += _dot(h1_scr[...], w2_refs[c][:, cols])

    ffn_slices = iter([functools.partial(f, c, q) for c in range(4)
                       for f in (ffn_up, ffn_down) for q in range(4)])

    def ffn_step(n=1):
        for _ in range(n):
            next(ffn_slices)()

    ffn_step(2)

    x = x_ref[...]
    hn = _rms(x, nw_ref[...]).astype(BF16)
    acc_scr[...] = x1_scr[...]

    def conv_block(i):
        cols = slice(i * XBC_BLK, (i + 1) * XBC_BLK)
        xp_scr[SUBLANES:, cols] = _dot_nt(hn, wxbc_refs[i][...])
        conv = cb_ref[:, cols] + sum(
            xp_scr[SUBLANES - (SSD_CONV - 1) + k:SUBLANES - (SSD_CONV - 1) + k + TL, cols]
            * cw_ref[k:k + 1, cols] for k in range(SSD_CONV))
        tail = xp_scr[TL:TL + SUBLANES, cols]
        xp_scr[0:SUBLANES, cols] = tail
        cout_ref[:, cols] = tail
        return _silu(conv)

    xs_scr[:, 0:XBC_BLK] = conv_block(0)
    xs_scr[:, XBC_BLK:2 * XBC_BLK] = conv_block(1)
    bc = conv_block(2)
    b_scr[...] = bc[:, :2 * SSD_STATE]
    c_scr[...] = bc[:, 2 * SSD_STATE:]
    z_scr[...] = _dot_nt(hn, wz_ref[...])
    wdt = jnp.concatenate([wdt_ref[...], jnp.zeros((LANES - SSD_HEADS, D_MODEL), BF16)], axis=0)
    dt = jax.nn.softplus(_dot_nt(hn, wdt) + dtb_ref[...])
    dt_scr[...] = dt
    a_scr[...] = dt * (-jnp.exp(alog_ref[...]) * LOG2_E)
    ffn_step(5)

    li = _iota((QS, QS), 0)
    si = _iota((QS, QS), 1)
    causal = li >= si
    tril = causal.astype(F32)
    lo = si < SSD_HEAD_DIM

    def chunk(ci, carry):
        rows = pl.ds(ci * QS, QS)
        acum = _dot(tril, a_scr[rows, :], HI)
        acum_t = jnp.transpose(acum)
        asrc_t = acum_t - jnp.log2(jnp.transpose(dt_scr[rows, :]))
        alast = acum[QS - 1:QS, :]
        alast_t = acum_t[:, QS - 1:QS]
        for g in range(2):
            bg = b_scr[rows, g * SSD_STATE:(g + 1) * SSD_STATE]
            cg_b = c_scr[rows, g * SSD_STATE:(g + 1) * SSD_STATE].astype(BF16)
            bt = jnp.transpose(bg)
            cb = _dot(cg_b, bt.astype(BF16))
            h_grp = jnp.concatenate([h_scr[4 * g + i] for i in range(4)], axis=1)
            y_off = _dot(cg_b, h_grp.astype(BF16))
            for hp in range(4):
                pr = 4 * g + hp
                cols = slice(pr * LANES, (pr + 1) * LANES)
                xs_pair = xs_scr[rows, cols]
                xs_b = xs_pair.astype(BF16)
                zero = jnp.zeros_like(xs_b)
                xs_cat = jnp.concatenate([jnp.where(lo, xs_b, zero), jnp.where(lo, zero, xs_b)], axis=0)
                acols, ms, btss = [], [], []
                for h in (2 * pr, 2 * pr + 1):
                    acol = jnp.broadcast_to(acum[:, h:h + 1], (QS, QS))
                    asrc = asrc_t[h:h + 1, :]
                    lmat = jnp.exp2(jnp.where(causal, acol - asrc, -1e30))
                    ms.append((cb * lmat).astype(BF16))
                    btss.append((bt * jnp.exp2(alast_t[h:h + 1, :] - asrc)).astype(BF16))
                    acols.append(acol)
                y = _dot(jnp.concatenate(ms, axis=1), xs_cat)
                st = _dot(jnp.concatenate(btss, axis=1), xs_cat)
                h0, h1 = 2 * pr, 2 * pr + 1
                decay = jnp.exp2(jnp.where(lo[0:1, :], alast[:, h0:h0 + 1], alast[:, h1:h1 + 1]))
                h_scr[pr] = decay * h_scr[pr] + st
                y = (y + jnp.exp2(jnp.where(lo, acols[0], acols[1])) * y_off[:, hp * LANES:(hp + 1) * LANES]
                     + drep_ref[:, cols] * xs_pair)
                y_scr[rows, cols] = y * _silu(z_scr[rows, cols])
                if hp != 3 or (ci == 0 and g == 0):
                    ffn_step()
        half = D_SSD // 2
        for g in range(2):
            yg = y_scr[rows, g * half:(g + 1) * half]
            yg = yg * lax.rsqrt(jnp.mean(yg * yg, axis=-1, keepdims=True) + EPS)
            y_scr[rows, g * half:(g + 1) * half] = yg * snw_ref[:, g * half:(g + 1) * half]
        return carry

    for ci in range(TL // QS):
        chunk(ci, 0)
    assert next(ffn_slices, None) is None, "every FFN slice must have been issued"
    y_ref[...] = _rms(acc_scr[...], fw_ref[...])
    for hrows in (slice(0, TL // 2), slice(TL // 2, TL)):
        x1 = x[hrows] + o5_ref[hrows, :] + _dot(y_scr[hrows, :].astype(BF16), wout_ref[...])
        x1_scr[hrows, :] = x1
        hf_scr[hrows, :] = _rms(x1, fnw_ref[...]).astype(BF16)

    @pl.when(t == 0)
    def _():
        ys_ref[...] = y_ref[0:DEC_BATCH, :]

    @pl.when(jnp.logical_and(j == SEQ // TL - 1, t < n_tiles))
    def _():
        for pr in range(SSD_HEADS // 2):
            hout_ref[2 * pr:2 * pr + 2] = jnp.transpose(h_scr[pr]).reshape(2, SSD_HEAD_DIM, SSD_STATE)


def _ssd_ffn_prompt(x_prompt, o5, norm_mix_w, w_int, conv_w, conv_b, dtb, alog, drep, snw,
                    w_out_b, norm_ffn_w, w1, w2, norm_final_w, x1_sample):
    nj = SEQ // TL
    n_tiles = BATCH * nj
    c2 = lambda t: (0, 0)
    once = pl.Buffered(1)
    row = lambda n: pl.BlockSpec((1, n), c2, pipeline_mode=once)
    mat = lambda r, c: pl.BlockSpec((r, c), c2, pipeline_mode=once)
    wrows = lambda n, i: pl.BlockSpec((n, D_MODEL), lambda t: (i, 0), pipeline_mode=once)
    xbc0 = (D_S5 + D_SSD) // XBC_BLK

    def tile(t):
        tt = jnp.minimum(t, n_tiles - 1)
        return tt // nj, lax.rem(tt, nj)

    def prev(t):
        tp = jnp.maximum(t - 1, 0)
        return tp // nj, lax.rem(tp, nj)

    return pl.pallas_call(
        _ssd_ffn_kernel, grid=(n_tiles + 1,),
        in_specs=[pl.BlockSpec((None, TL, D_MODEL), lambda t: (*tile(t), 0)),
                  pl.BlockSpec((TL, D_MODEL), lambda t: tile(t)[::-1]),
                  row(D_MODEL),
                  wrows(D_SSD, D_S5 // D_SSD),
                  wrows(XBC_BLK, xbc0), wrows(XBC_BLK, xbc0 + 1), wrows(XBC_BLK, xbc0 + 2),
                  wrows(SSD_HEADS, (D_S5 + D_SSD + D_CONV) // SSD_HEADS),
                  mat(SSD_CONV, D_CONV), row(D_CONV),
                  row(LANES), row(LANES), row(D_SSD), row(D_SSD),
                  wrows(D_SSD, D_S5 // D_SSD),
                  row(D_MODEL),
                  *[pl.BlockSpec((D_MODEL, D_FF // 4), lambda t, c=c: (0, c), pipeline_mode=once)
                    for c in range(4)],
                  *[pl.BlockSpec((D_FF // 4, D_MODEL), lambda t, c=c: (c, 0), pipeline_mode=once)
                    for c in range(4)],
                  row(D_MODEL),
                  mat(DEC_BATCH, D_MODEL)],
        out_specs=[pl.BlockSpec((None, TL, D_MODEL), lambda t: (*prev(t), 0)),
                   pl.BlockSpec((None, SSD_HEADS, SSD_HEAD_DIM, SSD_STATE), lambda t: (tile(t)[0], 0, 0, 0)),
                   pl.BlockSpec((None, SUBLANES, D_CONV), lambda t: (tile(t)[0], 0, 0)),
                   pl.BlockSpec((DEC_BATCH, D_MODEL), c2)],
        out_shape=(jax.ShapeDtypeStruct((BATCH, SEQ, D_MODEL), F32),
                   jax.ShapeDtypeStruct((BATCH, SSD_HEADS, SSD_HEAD_DIM, SSD_STATE), F32),
                   jax.ShapeDtypeStruct((BATCH, SUBLANES, D_CONV), F32),
                   jax.ShapeDtypeStruct((DEC_BATCH, D_MODEL), F32)),
        scratch_shapes=[pltpu.VMEM((SUBLANES + TL, D_CONV), F32),
                        pltpu.VMEM((TL, D_SSD), F32),
                        pltpu.VMEM((TL, 2 * SSD_STATE), F32),
                        pltpu.VMEM((TL, 2 * SSD_STATE), F32),
                        pltpu.VMEM((TL, D_SSD), F32),
                        pltpu.VMEM((TL, LANES), F32),
                        pltpu.VMEM((TL, LANES), F32),
                        pltpu.VMEM((TL, D_SSD), F32),
                        pltpu.VMEM((SSD_HEADS // 2, SSD_STATE, LANES), F32),
                        pltpu.VMEM((TL, D_MODEL), F32),
                        pltpu.VMEM((TL, D_MODEL), BF16),
                        pltpu.VMEM((TL, D_FF // 4), BF16),
                        pltpu.VMEM((TL, D_MODEL), F32)],
        compiler_params=pltpu.CompilerParams(dimension_semantics=("arbitrary",),
                                             vmem_limit_bytes=VMEM_LIMIT_FUSED),
        name="ssd_ffn",
    )(x_prompt, o5, norm_mix_w.reshape(1, D_MODEL), w_int, w_int, w_int, w_int, w_int,
      conv_w, conv_b, dtb, alog, drep, snw, w_out_b,
      norm_ffn_w.reshape(1, D_MODEL), w1, w1, w1, w1, w2, w2, w2, w2,
      norm_final_w.reshape(1, D_MODEL), x1_sample)


def _sample_inproj_kernel(x_ref, nw_ref, wu_ref, wz_ref, wxbc0_ref, wxbc1_ref, wxbc2_ref, wdt_ref,
                          cw_ref, cb_ref, cbuf_ref, dtb_ref, alog_ref,
                          u_ref, z_ref, xs_ref, xdt_ref, b_ref, c_ref, dec_ref, nconv_ref):
    hn = _rms(x_ref[...], nw_ref[...]).astype(BF16)
    u_ref[...] = _dot_nt(wu_ref[...], hn)
    z_ref[...] = _dot_nt(hn, wz_ref[...])
    xbc_t = jnp.concatenate([_dot_nt(w_ref[...], hn) for w_ref in (wxbc0_ref, wxbc1_ref, wxbc2_ref)],
                            axis=0)
    wdt = jnp.concatenate([wdt_ref[...], jnp.zeros((LANES - SSD_HEADS, D_MODEL), BF16)], axis=0)
    dt = jax.nn.softplus(_dot_nt(hn, wdt) + dtb_ref[...])
    dec_ref[...] = jnp.exp(dt * (-jnp.exp(alog_ref[...])))
    head_of_col = lax.shift_right_logical(_iota((LANES, D_SSD), 1), 6)
    dt_rep = _dot(dt, (_iota((LANES, D_SSD), 0) == head_of_col).astype(F32), HI)
    cw_t = jnp.transpose(jnp.concatenate(
        [cw_ref[...], cb_ref[...], jnp.zeros((LANES - SSD_CONV - 1, D_CONV), F32)], axis=0))
    conv_t = cw_t[:, SSD_CONV:SSD_CONV + 1] + xbc_t * cw_t[:, SSD_CONV - 1:SSD_CONV]
    for k in range(SSD_CONV - 1):
        conv_t = conv_t + cbuf_ref[k] * cw_t[:, k:k + 1]
        if k > 0:
            nconv_ref[k - 1] = cbuf_ref[k]
    nconv_ref[SSD_CONV - 2] = xbc_t
    conv = jnp.transpose(_silu(conv_t))
    xs_ref[...] = conv[:, :D_SSD]
    xdt_ref[...] = conv[:, :D_SSD] * dt_rep
    b_ref[...] = conv[:, D_SSD:D_SSD + 2 * SSD_STATE]
    c_ref[...] = conv[:, D_SSD + 2 * SSD_STATE:]


def _sample_s5_kernel(u_ref, hre_ref, him_ref, a1c_ref, bbt_ref, cneg_ref, dcb_ref,
                      nre_ref, nim_ref, y_ref):
    for i in range(GS):
        u = u_ref[i]
        bu = _dot_tn(bbt_ref[i], u, HI)
        a_re, a_im = a1c_ref[i, 0], a1c_ref[i, 1]
        h_re, h_im = hre_ref[i], him_ref[i]
        n_re = a_re * h_re - a_im * h_im + bu[0:S5_STATE]
        n_im = a_re * h_im + a_im * h_re + bu[S5_STATE:]
        nre_ref[i] = n_re
        nim_ref[i] = n_im
        y_ref[i] = _dot(cneg_ref[i], jnp.concatenate([n_re, n_im], axis=0), HI) + dcb_ref[i] * u


def _sample_ssd_kernel(dec_ref, h0_ref, xdt_ref, xs_ref, b_ref, c_ref, drep_ref, hn_ref, y_ref):
    blk = pl.program_id(0)
    hpg = SSD_HEADS // 2
    half = D_SSD // 2
    xt = jnp.transpose(jnp.concatenate(
        [xdt_ref[...], jnp.zeros((LANES - SB, D_SSD), F32)], axis=0))
    c_b = c_ref[...].astype(BF16)
    rowi = _iota((SB, half), 0)
    ys = [jnp.zeros((SB, half), F32) for _ in range(2)]
    for jj in range(SB):
        seq = blk * SB + jj
        for g in range(2):
            brow = b_ref[jj:jj + 1, g * SSD_STATE:(g + 1) * SSD_STATE]
            parts = []
            for h in range(g * hpg, (g + 1) * hpg):
                xcol = xt[h * SSD_HEAD_DIM:(h + 1) * SSD_HEAD_DIM, jj:jj + 1]
                hn = dec_ref[seq, h] * h0_ref[jj, h] + xcol * brow
                hn_ref[jj, h] = hn
                parts.append(hn.astype(BF16))
            y_all = _dot_nt(c_b[:, g * SSD_STATE:(g + 1) * SSD_STATE], jnp.concatenate(parts, axis=0))
            ys[g] = jnp.where(rowi == jj, y_all, ys[g])
    y_ref[...] = jnp.concatenate(ys, axis=1) + drep_ref[...] * xs_ref[...]


def _sample_mix_kernel(x_ref, y5_ref, ys_ref, z_ref, wglut_ref, bglu_ref, nw5_ref, snw_ref,
                       wout_ref, x1_ref):
    g = jax.nn.gelu(jnp.transpose(y5_ref[...]))
    out = g * jax.nn.sigmoid(_dot_nt(g.astype(BF16), wglut_ref[...]) + bglu_ref[...])
    y5 = _rms(out, nw5_ref[...])
    y = ys_ref[...] * _silu(z_ref[...])
    half = D_SSD // 2
    yn = jnp.concatenate(
        [_rms(y[:, i * half:(i + 1) * half], snw_ref[:, i * half:(i + 1) * half]) for i in range(2)],
        axis=1)
    x1_ref[...] = (x_ref[...] + _dot(y5.astype(BF16), wout_ref[0:D_S5, :])
                   + _dot(yn.astype(BF16), wout_ref[D_S5:, :]))


def _sample_layer(x_sample, st_re, st_im, st_ssd, st_conv, norm_mix_w, w_int,
                  conv_w, conv_b, dtb, alog, drep, snw, prep, wglut, b_glu, s5_norm_w,
                  w_out_b):
    nb = DEC_BATCH
    a1c, bbt, cneg, dcb = prep[4:8]
    xs2 = x_sample.reshape(nb, D_MODEL)
    sds = lambda *s: jax.ShapeDtypeStruct(s, F32)
    whole = lambda shape: pl.BlockSpec(shape, lambda i: (0,) * len(shape))
    wrows = lambda n, blk: pl.BlockSpec((n, D_MODEL), lambda i: (blk, 0))
    xbc0 = (D_S5 + D_SSD) // XBC_BLK
    inproj_out = (sds(D_S5, nb), sds(nb, D_SSD), sds(nb, D_SSD), sds(nb, D_SSD),
                  sds(nb, 2 * SSD_STATE), sds(nb, 2 * SSD_STATE), sds(nb, LANES),
                  sds(SSD_CONV - 1, D_CONV, nb))
    u_t, z, xs, xdt, bm, cm, dec, nconv_t = pl.pallas_call(
        _sample_inproj_kernel, grid=(1,),
        in_specs=[whole((nb, D_MODEL)), whole((1, D_MODEL)),
                  wrows(D_S5, 0), wrows(D_SSD, D_S5 // D_SSD),
                  wrows(XBC_BLK, xbc0), wrows(XBC_BLK, xbc0 + 1), wrows(XBC_BLK, xbc0 + 2),
                  wrows(SSD_HEADS, (D_S5 + D_SSD + D_CONV) // SSD_HEADS),
                  whole((SSD_CONV, D_CONV)), whole((1, D_CONV)),
                  whole((SSD_CONV - 1, D_CONV, nb)), whole((1, LANES)), whole((1, LANES))],
        out_specs=[whole(s.shape) for s in inproj_out],
        out_shape=inproj_out,
        compiler_params=_cp("arbitrary"), name="sample_inproj",
    )(xs2, norm_mix_w.reshape(1, D_MODEL), w_int, w_int, w_int, w_int, w_int, w_int,
      conv_w, conv_b, jnp.transpose(st_conv, (1, 2, 0)), dtb, alog)

    def per_g(shape):
        nd = len(shape)
        return pl.BlockSpec((GS,) + shape, lambda g: (g,) + (0,) * nd)

    P = S5_STATE
    n_re_t, n_im_t, y5_t = pl.pallas_call(
        _sample_s5_kernel, grid=(S5_GROUPS // GS,),
        in_specs=[per_g((S5_CH, nb)), per_g((P, nb)), per_g((P, nb)), per_g((2, P, LANES)),
                  per_g((S5_CH, 2 * P)), per_g((S5_CH, 2 * P)), per_g((S5_CH, LANES))],
        out_specs=[per_g((P, nb)), per_g((P, nb)), per_g((S5_CH, nb))],
        out_shape=(sds(S5_GROUPS, P, nb), sds(S5_GROUPS, P, nb), sds(S5_GROUPS, S5_CH, nb)),
        compiler_params=_cp("arbitrary"), name="sample_s5",
    )(u_t.reshape(S5_GROUPS, S5_CH, nb), jnp.transpose(st_re, (1, 2, 0)),
      jnp.transpose(st_im, (1, 2, 0)), a1c, bbt, cneg, dcb)
    y5 = y5_t.reshape(D_S5, nb)

    smem = pl.BlockSpec(memory_space=pltpu.SMEM)
    blk2 = lambda n: pl.BlockSpec((SB, n), lambda i: (i, 0))
    st_spec = pl.BlockSpec((SB, SSD_HEADS, SSD_HEAD_DIM, SSD_STATE), lambda i: (i, 0, 0, 0))
    hn_ssd, ys = pl.pallas_call(
        _sample_ssd_kernel, grid=(nb // SB,),
        in_specs=[smem, st_spec, blk2(D_SSD), blk2(D_SSD), blk2(2 * SSD_STATE),
                  blk2(2 * SSD_STATE), pl.BlockSpec((1, D_SSD), lambda i: (0, 0))],
        out_specs=[st_spec, blk2(D_SSD)],
        out_shape=(sds(nb, SSD_HEADS, SSD_HEAD_DIM, SSD_STATE), sds(nb, D_SSD)),
        compiler_params=_cp("arbitrary"), name="sample_ssd",
    )(dec[:, :SSD_HEADS], st_ssd, xdt, xs, bm, cm, drep)

    x1 = pl.pallas_call(
        _sample_mix_kernel, out_shape=sds(nb, D_MODEL),
        compiler_params=_cp(), name="sample_mix",
    )(xs2, y5, ys, z, wglut, b_glu.reshape(1, D_S5), s5_norm_w.reshape(1, D_S5), snw, w_out_b)
    to_seq_major = lambda a: jnp.transpose(a, (2, 0, 1))
    return x1, to_seq_major(n_re_t), to_seq_major(n_im_t), hn_ssd, to_seq_major(nconv_t)


def kernel(x_prompt, x_sample, state_s5_re, state_s5_im, state_ssd, state_conv, norm_mix_w, w_in, s5_lam_re, s5_lam_im, s5_log_dt, s5_b_re, s5_b_im, s5_c_re, s5_c_im, s5_d, s5_w_glu, s5_b_glu, s5_norm_w, ssd_conv_w, ssd_conv_b, ssd_a_log, ssd_dt_bias, ssd_d, ssd_norm_w, w_out, norm_ffn_w, w_ff1, w_ff2, norm_final_w):
    P = S5_STATE
    w_int = jnp.transpose(w_in[0]).astype(BF16)
    wglut = jnp.transpose(s5_w_glu[0]).astype(BF16)
    w_out_b = w_out[0].astype(BF16)
    w1 = w_ff1[0].astype(BF16)
    w2 = w_ff2[0].astype(BF16)
    pad_h = lambda v: jnp.pad(v.reshape(1, SSD_HEADS), ((0, 0), (0, LANES - SSD_HEADS)))
    dtb, alog = pad_h(ssd_dt_bias[0]), pad_h(ssd_a_log[0])
    drep = jnp.repeat(ssd_d[0], SSD_HEAD_DIM).reshape(1, D_SSD)
    snw = ssd_norm_w[0].reshape(1, D_SSD)
    conv_w, conv_b = ssd_conv_w[0], ssd_conv_b[0].reshape(1, D_CONV)

    prep = _s5_prep(s5_lam_re[0], s5_lam_im[0], s5_log_dt[0], s5_b_re[0], s5_b_im[0],
                    s5_c_re[0], s5_c_im[0], s5_d[0])

    x1s, ns_re, ns_im, hn_ssd, nconv = _sample_layer(
        x_sample, state_s5_re[0], state_s5_im[0], state_ssd[0], state_conv[0], norm_mix_w[0],
        w_int, conv_w, conv_b, dtb, alog, drep, snw, prep, wglut,
        s5_b_glu[0], s5_norm_w[0], w_out_b)

    o5, hfin5 = _s5_prompt(x_prompt, norm_mix_w[0], w_int, prep, wglut, s5_b_glu[0], s5_norm_w[0],
                           w_out_b)
    y_prompt, h_ssd, ctail, y_s = _ssd_ffn_prompt(
        x_prompt, o5, norm_mix_w[0], w_int, conv_w, conv_b, dtb, alog, drep, snw, w_out_b,
        norm_ffn_w[0], w1, w2, norm_final_w, x1s)
    hfin5 = jnp.transpose(hfin5, (1, 0, 2))
    np_re, np_im = hfin5[None, :, :, :P], hfin5[None, :, :, P:]
    np_ssd = h_ssd[None]
    np_conv = ctail[None, :, SUBLANES - (SSD_CONV - 1):, :]

    y_sample = y_s.reshape(DEC_BATCH, 1, D_MODEL)
    ns_re, ns_im = ns_re[None], ns_im[None]
    ns_ssd = hn_ssd[None]
    ns_conv = nconv[None]

    return (y_prompt, y_sample, np_re, np_im, np_ssd, np_conv, ns_re, ns_im, ns_ssd, ns_conv)
```

```python
import functools

import jax
import jax.numpy as jnp
from jax import lax
from jax.experimental import pallas as pl
from jax.experimental.pallas import tpu as pltpu

F32 = jnp.float32
BF16 = jnp.bfloat16
HI = lax.Precision.HIGHEST
EPS = 1e-5
LOG2_E = 1.4426950408889634

D_MODEL = 1024
BATCH = 8
SEQ = 2048
DEC_BATCH = 128
D_S5 = 1024
S5_CH = 16
S5_GROUPS = 64
S5_STATE = 64
D_SSD = 1024
SSD_HEADS = 16
SSD_HEAD_DIM = 64
SSD_STATE = 128
SSD_CONV = 4
D_CONV = 1536
D_FF = 4096

LANES = 128
SUBLANES = 8
Q5 = 16
NC5 = SEQ // Q5
M5 = NC5 * BATCH
R5 = Q5 * S5_CH
QS = 128
TL = 512
XBC_BLK = 512
SB = 16
GS = 8
GP = 8
GC = 8
VMEM_LIMIT = 56 * 1024 * 1024
VMEM_LIMIT_FUSED = 62 * 1024 * 1024


def _cp(*sem):
    return pltpu.CompilerParams(dimension_semantics=sem, vmem_limit_bytes=VMEM_LIMIT)


def _rms(x, w):
    return x * lax.rsqrt(jnp.mean(x * x, axis=-1, keepdims=True) + EPS) * w


def _dot(a, b, precision=None):
    return jnp.dot(a, b, preferred_element_type=F32, precision=precision)


def _dot_nt(a, b, precision=None):
    return lax.dot_general(a, b, (((1,), (1,)), ((), ())), preferred_element_type=F32,
                           precision=precision)


def _dot_tn(a, b, precision=None):
    return lax.dot_general(a, b, (((0,), (0,)), ((), ())), preferred_element_type=F32,
                           precision=precision)


def _iota(shape, dim):
    return lax.broadcasted_iota(jnp.int32, shape, dim)


def _silu(x):
    return x * jax.nn.sigmoid(x)


def _dot_split(a, b):
    a_hi = a.astype(BF16)
    b_hi = b.astype(BF16)
    a_lo = (a - a_hi.astype(F32)).astype(BF16)
    b_lo = (b - b_hi.astype(F32)).astype(BF16)
    return _dot(a_hi, b_hi) + _dot(a_lo, b_hi) + _dot(a_hi, b_lo)


def _repeat_channels(x):
    return jnp.broadcast_to(x.reshape(Q5, 1, LANES), (Q5, S5_CH, LANES)).reshape(R5, LANES)


def _tile_steps(x):
    return jnp.broadcast_to(x[None], (Q5, S5_CH, LANES)).reshape(R5, LANES)


def _s5_prep_kernel(lre_ref, lim_ref, ldt_ref, cre_ref, cim_ref, d_ref, bre_ref, bim_ref,
                    ws_ref, t_ref, wo_ref, aq_ref, a1c_ref, bbt_ref, cneg_ref, dcb_ref):
    dup = lambda a: jnp.concatenate([a, a], axis=1)
    for i in range(GP):
        c_re, c_im = cre_ref[i], cim_ref[i]
        drow = jnp.concatenate([jnp.zeros((1, R5 - S5_CH), F32), d_ref[i:i + 1, :]], axis=1)
        _s5_prep_group(dup(lre_ref[i:i + 1, :]), dup(lim_ref[i:i + 1, :]), ldt_ref[i:i + 1, :],
                       jnp.concatenate([c_re, c_im], axis=1), dup(c_re), dup(c_im), drow,
                       dup(jnp.transpose(bre_ref[i])), dup(jnp.transpose(bim_ref[i])),
                       ws_ref.at[i], t_ref.at[i], wo_ref.at[i], aq_ref.at[i], a1c_ref.at[i],
                       bbt_ref.at[i], cneg_ref.at[i], dcb_ref.at[i])


def _col_bcast(row):
    return jnp.transpose(jnp.broadcast_to(row, (LANES, LANES)))


def _s5_prep_group(lr2, li2, ldt, ccat, c2re, c2im, drow, bt2re, bt2im,
                   ws_ref, t_ref, wo_ref, aq_ref, a1c_ref, bbt_ref, cneg_ref, dcb_ref):
    dt = jnp.exp(ldt)

    mag = jnp.exp(lr2 * dt)
    ab_re = mag * jnp.cos(li2 * dt)
    ab_im = mag * jnp.sin(li2 * dt)
    den = lr2 * lr2 + li2 * li2
    nr, ni = ab_re - 1.0, ab_im
    f_re, f_im = (nr * lr2 + ni * li2) / den, (ni * lr2 - nr * li2) / den

    def powers(expo):
        p_re, p_im = jnp.ones(expo.shape, F32), jnp.zeros(expo.shape, F32)
        s_re, s_im = ab_re, ab_im
        for k in range(Q5.bit_length()):
            take = (lax.shift_right_logical(expo, k) & 1) == 1
            p_re, p_im = (jnp.where(take, p_re * s_re - p_im * s_im, p_re),
                          jnp.where(take, p_re * s_im + p_im * s_re, p_im))
            s_re, s_im = s_re * s_re - s_im * s_im, 2.0 * s_re * s_im
        return p_re, p_im

    lane1 = _iota((1, LANES), 1) < S5_STATE
    lo16 = _iota((S5_CH, LANES), 1) < S5_STATE
    bb_a = f_re * bt2re - f_im * bt2im
    bb_b = f_re * bt2im + f_im * bt2re
    bbt = jnp.where(lo16, bb_a, bb_b)
    bbt_sw = jnp.where(lo16, bb_b, bb_a)

    rev = SUBLANES * ((Q5 + SUBLANES) // SUBLANES)
    r = _iota((rev + Q5, LANES), 0)
    tab_re, tab_im = powers(jnp.where(r < rev, r, (rev + Q5 - 1) - r))
    aq_re, aq_im = tab_re[Q5:Q5 + 1, :], tab_im[Q5:Q5 + 1, :]
    a1_re, a1_im = tab_re[1:2, :], tab_im[1:2, :]
    aq_ref[...] = jnp.concatenate(
        [aq_re, jnp.where(lane1, -aq_im, aq_im), jnp.where(lane1, aq_im, -aq_im),
         jnp.zeros((5, LANES), F32)], axis=0)

    a1c_ref[0] = _col_bcast(a1_re)[0:S5_STATE, :]
    a1c_ref[1] = _col_bcast(a1_im)[0:S5_STATE, :]
    bbt_ref[...] = bbt
    cneg_ref[...] = jnp.where(lo16, c2re, -c2im)
    dcb_ref[...] = _col_bcast(drow[:, R5 - LANES:])[LANES - S5_CH:, :]

    ain_re, ain_im = _repeat_channels(tab_re[rev:rev + Q5, :]), _repeat_channels(tab_im[rev:rev + Q5, :])
    sign = jnp.where(lane1, -1.0, 1.0)
    ws_t = ain_re * _tile_steps(bbt) + (sign * ain_im) * _tile_steps(bbt_sw)
    ws = jnp.transpose(ws_t)
    ws_ref[...] = ws.astype(BF16)

    upper = _iota((LANES, R5), 0) < S5_STATE
    vmat = _dot_split(ccat, jnp.where(upper, ws, -ws))
    vrow = _iota((S5_CH, R5), 0)
    vcol = _iota((S5_CH, R5), 1)
    vmat = vmat + jnp.where(vcol == (Q5 - 1) * S5_CH + vrow, drow, 0.0)
    for l in range(Q5):
        width = (l + 1) * S5_CH
        rolled = vmat if width == R5 else pltpu.roll(vmat, width, axis=1)
        t_ref[l * S5_CH:(l + 1) * S5_CH, :] = jnp.where(vcol < width, rolled, 0.0).astype(BF16)

    aout_re, aout_im = _repeat_channels(tab_re[1:Q5 + 1, :]), _repeat_channels(tab_im[1:Q5 + 1, :])
    c2re_t, c2im_t = _tile_steps(c2re), _tile_steps(c2im)
    wo_ref[...] = jnp.where(_iota((R5, LANES), 1) < S5_STATE,
                            c2re_t * aout_re - c2im_t * aout_im,
                            -(c2re_t * aout_im + c2im_t * aout_re)).astype(BF16)


def _s5_prep(lam_re, lam_im, log_dt, b_re, b_im, c_re, c_im, d):
    G, P, C = S5_GROUPS, S5_STATE, S5_CH
    args = (lam_re, lam_im, log_dt.reshape(G, 1), c_re, c_im, d.reshape(G, C), b_re, b_im)
    spec = lambda a: pl.BlockSpec((GP,) + a.shape[1:], lambda g, nd=a.ndim: (g,) + (0,) * (nd - 1))
    out_shapes = (
        jax.ShapeDtypeStruct((G, 2 * P, R5), BF16),
        jax.ShapeDtypeStruct((G, R5, R5), BF16),
        jax.ShapeDtypeStruct((G, R5, 2 * P), BF16),
        jax.ShapeDtypeStruct((G, 8, 2 * P), F32),
        jax.ShapeDtypeStruct((G, 2, P, LANES), F32),
        jax.ShapeDtypeStruct((G, C, 2 * P), F32),
        jax.ShapeDtypeStruct((G, C, 2 * P), F32),
        jax.ShapeDtypeStruct((G, C, LANES), F32),
    )
    return pl.pallas_call(
        _s5_prep_kernel, grid=(G // GP,),
        in_specs=[spec(a) for a in args],
        out_specs=[pl.BlockSpec((GP,) + s.shape[1:], lambda g, nd=len(s.shape): (g,) + (0,) * (nd - 1))
                   for s in out_shapes],
        out_shape=out_shapes, compiler_params=_cp("arbitrary"), name="s5_prep",
    )(*args)


def _s5_inproj_kernel(x_ref, nw_ref, wut_ref, ut_ref):
    x = x_ref[...].reshape(M5, D_MODEL)
    hn = _rms(x, nw_ref[...]).astype(BF16)
    ut = _dot_nt(wut_ref[...], hn)
    ut_ref[...] = ut.astype(BF16).reshape(S5_GROUPS, S5_CH, M5)


def _s5_core_kernel(ut_ref, t_ref, ws_ref, wo_ref, aq_ref, y_ref, hfin_ref,
                    hs_scr, sp_scr, sq_scr, yin_scr):
    for i in range(GC):
        ut = ut_ref[i]
        sp = jnp.transpose(_dot(ws_ref[i], ut))
        sp_scr[i] = sp
        sq_scr[i] = pltpu.roll(sp, S5_STATE, axis=1)
        yin_scr[i] = _dot(t_ref[i], ut)
    for i in range(GC):
        a1 = jnp.broadcast_to(aq_ref[i, 0:1, :], (BATCH, LANES))
        a2 = jnp.broadcast_to(aq_ref[i, 1:2, :], (BATCH, LANES))
        a3 = jnp.broadcast_to(aq_ref[i, 2:3, :], (BATCH, LANES))
        hp = jnp.zeros((BATCH, LANES), F32)
        hq = jnp.zeros((BATCH, LANES), F32)
        for k in range(NC5):
            rows = pl.ds(k * BATCH, BATCH)
            hs_scr[i, rows, :] = hp
            hp, hq = (a1 * hp + a2 * hq + sp_scr[i, rows, :],
                      a1 * hq + a3 * hp + sq_scr[i, rows, :])
        hfin_ref[i] = hp
    for i in range(GC):
        y = yin_scr[i] + _dot_nt(wo_ref[i], hs_scr[i].astype(BF16))
        y_ref[i] = y.astype(BF16)


def _s5_out_kernel(g_ref, wglut_ref, bglu_ref, nw_ref, wout_ref, o_ref):
    g = jax.nn.gelu(g_ref[...].astype(F32).reshape(D_S5, M5))
    z = _dot(wglut_ref[...], g.astype(BF16)) + bglu_ref[...]
    out = g * jax.nn.sigmoid(z)
    ms = jnp.mean(out * out, axis=0, keepdims=True)
    y5 = out * lax.rsqrt(ms + EPS) * nw_ref[...]
    o = _dot_tn(y5.astype(BF16), wout_ref[...])
    o_ref[...] = o.reshape(NC5, BATCH, D_MODEL)


def _s5_prompt(x_prompt, norm_mix_w, w_int, prep, wglut, b_glu, s5_norm_w, w_out_b):
    ws, tmat, wo, aq = prep[:4]
    x_v = jnp.transpose(x_prompt, (1, 0, 2)).reshape(NC5, Q5, BATCH, D_MODEL)
    tok_blk = pl.BlockSpec((NC5, None, BATCH, D_MODEL), lambda s: (0, s, 0, 0))
    chan_blk = pl.BlockSpec((S5_GROUPS, S5_CH, M5), lambda s: (0, s, 0))
    const2 = lambda s: (0, 0)
    once = pl.Buffered(1)
    ut = pl.pallas_call(
        _s5_inproj_kernel, grid=(Q5,),
        in_specs=[tok_blk,
                  pl.BlockSpec((1, D_MODEL), const2),
                  pl.BlockSpec((D_S5, D_MODEL), const2, pipeline_mode=once)],
        out_specs=chan_blk,
        out_shape=jax.ShapeDtypeStruct((S5_GROUPS, R5, M5), BF16),
        compiler_params=_cp("arbitrary"), name="s5_inproj",
    )(x_v, norm_mix_w.reshape(1, D_MODEL), w_int)

    per_g = lambda shape: pl.BlockSpec((GC,) + shape, lambda g: (g, 0, 0))
    y5_pre, hfin = pl.pallas_call(
        _s5_core_kernel, grid=(S5_GROUPS // GC,),
        in_specs=[per_g((R5, M5)), per_g((R5, R5)), per_g((LANES, R5)), per_g((R5, LANES)),
                  per_g((8, LANES))],
        out_specs=[per_g((R5, M5)), per_g((BATCH, LANES))],
        out_shape=(jax.ShapeDtypeStruct((S5_GROUPS, R5, M5), BF16),
                   jax.ShapeDtypeStruct((S5_GROUPS, BATCH, LANES), F32)),
        scratch_shapes=[pltpu.VMEM((GC, M5, LANES), F32) for _ in range(3)]
        + [pltpu.VMEM((GC, R5, M5), F32)],
        compiler_params=_cp("arbitrary"), name="s5_core",
    )(ut, tmat, ws, wo, aq)

    o5 = pl.pallas_call(
        _s5_out_kernel, grid=(Q5,),
        in_specs=[chan_blk,
                  pl.BlockSpec((D_S5, D_S5), const2, pipeline_mode=once),
                  pl.BlockSpec((D_S5, 1), const2),
                  pl.BlockSpec((D_S5, 1), const2),
                  pl.BlockSpec((D_S5, D_MODEL), const2, pipeline_mode=once)],
        out_specs=tok_blk,
        out_shape=jax.ShapeDtypeStruct((NC5, Q5, BATCH, D_MODEL), F32),
        compiler_params=_cp("arbitrary"), name="s5_out",
    )(y5_pre, wglut, b_glu.reshape(D_S5, 1), s5_norm_w.reshape(D_S5, 1), w_out_b)
    return o5.reshape(SEQ, BATCH * D_MODEL), hfin


assert D_SSD == 2 * XBC_BLK and D_CONV == 3 * XBC_BLK and 2 * SSD_STATE * 2 == XBC_BLK


def _ssd_ffn_kernel(x_ref, o5_ref, nw_ref, wz_ref, wxbc0_ref, wxbc1_ref, wxbc2_ref, wdt_ref,
                    cw_ref, cb_ref, dtb_ref, alog_ref, drep_ref, snw_ref, wout_ref,
                    fnw_ref, w1a_ref, w1b_ref, w1c_ref, w1d_ref, w2a_ref, w2b_ref, w2c_ref, w2d_ref,
                    fw_ref, x1s_ref,
                    y_ref, hout_ref, cout_ref, ys_ref,
                    xp_scr, xs_scr, b_scr, c_scr, z_scr, a_scr, dt_scr, y_scr, h_scr, x1_scr,
                    hf_scr, h1_scr, acc_scr):
    wxbc_refs = (wxbc0_ref, wxbc1_ref, wxbc2_ref)
    t = pl.program_id(0)
    n_tiles = pl.num_programs(0) - 1
    j = lax.rem(t, SEQ // TL)

    @pl.when(t == 0)
    def _():
        x1s = x1s_ref[...]
        x1_scr[0:DEC_BATCH, :] = x1s
        x1_scr[DEC_BATCH:, :] = jnp.zeros((TL - DEC_BATCH, D_MODEL), F32)
        hf_scr[0:DEC_BATCH, :] = _rms(x1s, fnw_ref[...]).astype(BF16)
        hf_scr[DEC_BATCH:, :] = jnp.zeros((TL - DEC_BATCH, D_MODEL), BF16)

    @pl.when(j == 0)
    def _():
        xp_scr[0:SUBLANES, :] = jnp.zeros((SUBLANES, D_CONV), F32)
        h_scr[...] = jnp.zeros(h_scr.shape, F32)

    w1_refs = (w1a_ref, w1b_ref, w1c_ref, w1d_ref)
    w2_refs = (w2a_ref, w2b_ref, w2c_ref, w2d_ref)
    sl = D_FF // 16

    def ffn_up(c, q):
        cols = slice(q * sl, (q + 1) * sl)
        h1 = jnp.square(jnp.maximum(_dot(hf_scr[...], w1_refs[c][:, cols]), 0.0))
        h1_scr[:, cols] = h1.astype(BF16)

    def ffn_down(c, q):
        cols = slice(q * sl, (q + 1) * sl)
        acc_scr[:, cols] += _dot(h1_scr[...], w2_refs[c][:, cols])

    ffn_slices = iter([functools.partial(f, c, q) for c in range(4)
                       for f in (ffn_up, ffn_down) for q in range(4)])

    def ffn_step(n=1):
        for _ in range(n):
            next(ffn_slices)()

    ffn_step(2)

    x = x_ref[...]
    hn = _rms(x, nw_ref[...]).astype(BF16)
    acc_scr[...] = x1_scr[...]

    def conv_block(i):
        cols = slice(i * XBC_BLK, (i + 1) * XBC_BLK)
        xp_scr[SUBLANES:, cols] = _dot_nt(hn, wxbc_refs[i][...])
        conv = cb_ref[:, cols] + sum(
            xp_scr[SUBLANES - (SSD_CONV - 1) + k:SUBLANES - (SSD_CONV - 1) + k + TL, cols]
            * cw_ref[k:k + 1, cols] for k in range(SSD_CONV))
        tail = xp_scr[TL:TL + SUBLANES, cols]
        xp_scr[0:SUBLANES, cols] = tail
        cout_ref[:, cols] = tail
        return _silu(conv)

    xs_scr[:, 0:XBC_BLK] = conv_block(0)
    xs_scr[:, XBC_BLK:2 * XBC_BLK] = conv_block(1)
    bc = conv_block(2)
    b_scr[...] = bc[:, :2 * SSD_STATE]
    c_scr[...] = bc[:, 2 * SSD_STATE:]
    z_scr[...] = _dot_nt(hn, wz_ref[...])
    wdt = jnp.concatenate([wdt_ref[...], jnp.zeros((LANES - SSD_HEADS, D_MODEL), BF16)], axis=0)
    dt = jax.nn.softplus(_dot_nt(hn, wdt) + dtb_ref[...])
    dt_scr[...] = dt
    a_scr[...] = dt * (-jnp.exp(alog_ref[...]) * LOG2_E)
    ffn_step(5)

    li = _iota((QS, QS), 0)
    si = _iota((QS, QS), 1)
    causal = li >= si
    lo = si < SSD_HEAD_DIM

    def chunk(ci, carry):
        rows = pl.ds(ci * QS, QS)
        acum = a_scr[rows, :]
        for sh in (1, 2, 4, 8, 16, 32, 64):
            acum = acum + jnp.where(li >= sh, pltpu.roll(acum, sh, axis=0), 0.0)
        acum_t = jnp.transpose(acum)
        asrc_t = acum_t - jnp.log2(jnp.transpose(dt_scr[rows, :]))
        alast = acum[QS - 1:QS, :]
        alast_t = acum_t[:, QS - 1:QS]
        for g in range(2):
            bg = b_scr[rows, g * SSD_STATE:(g + 1) * SSD_STATE]
            cg_b = c_scr[rows, g * SSD_STATE:(g + 1) * SSD_STATE].astype(BF16)
            bt = jnp.transpose(bg)
            cb = _dot(cg_b, bt.astype(BF16))
            h_grp = jnp.concatenate([h_scr[4 * g + i] for i in range(4)], axis=1)
            y_off = _dot(cg_b, h_grp.astype(BF16))
            for hp in range(4):
                pr = 4 * g + hp
                cols = slice(pr * LANES, (pr + 1) * LANES)
                xs_pair = xs_scr[rows, cols]
                xs_b = xs_pair.astype(BF16)
                zero = jnp.zeros_like(xs_b)
                xs_cat = jnp.concatenate([jnp.where(lo, xs_b, zero), jnp.where(lo, zero, xs_b)], axis=0)
                acols, ms, btss = [], [], []
                for h in (2 * pr, 2 * pr + 1):
                    acol = jnp.broadcast_to(acum[:, h:h + 1], (QS, QS))
                    asrc = asrc_t[h:h + 1, :]
                    lmat = jnp.exp2(jnp.where(causal, acol - asrc, -1e30))
                    ms.append((cb * lmat).astype(BF16))
                    btss.append((bt * jnp.exp2(alast_t[h:h + 1, :] - asrc)).astype(BF16))
                    acols.append(acol)
                y = _dot(jnp.concatenate(ms, axis=1), xs_cat)
                st = _dot(jnp.concatenate(btss, axis=1), xs_cat)
                h0, h1 = 2 * pr, 2 * pr + 1
                decay = jnp.exp2(jnp.where(lo[0:1, :], alast[:, h0:h0 + 1], alast[:, h1:h1 + 1]))
                h_scr[pr] = decay * h_scr[pr] + st
                y = (y + jnp.exp2(jnp.where(lo, acols[0], acols[1])) * y_off[:, hp * LANES:(hp + 1) * LANES]
                     + drep_ref[:, cols] * xs_pair)
                y_scr[rows, cols] = y * _silu(z_scr[rows, cols])
                if hp != 3 or (ci == 0 and g == 0):
                    ffn_step()
        half = D_SSD // 2
        for g in range(2):
            yg = y_scr[rows, g * half:(g + 1) * half]
            yg = yg * lax.rsqrt(jnp.mean(yg * yg, axis=-1, keepdims=True) + EPS)
            y_scr[rows, g * half:(g + 1) * half] = yg * snw_ref[:, g * half:(g + 1) * half]
        return carry

    for ci in range(TL // QS):
        chunk(ci, 0)
    assert next(ffn_slices, None) is None, "every FFN slice must have been issued"
    y_ref[...] = _rms(acc_scr[...], fw_ref[...])
    for hrows in (slice(0, TL // 2), slice(TL // 2, TL)):
        x1 = x[hrows] + o5_ref[hrows, :] + _dot(y_scr[hrows, :].astype(BF16), wout_ref[...])
        x1_scr[hrows, :] = x1
        hf_scr[hrows, :] = _rms(x1, fnw_ref[...]).astype(BF16)

    @pl.when(t == 0)
    def _():
        ys_ref[...] = y_ref[0:DEC_BATCH, :]

    @pl.when(jnp.logical_and(j == SEQ // TL - 1, t < n_tiles))
    def _():
        for pr in range(SSD_HEADS // 2):
            hout_ref[2 * pr:2 * pr + 2] = jnp.transpose(h_scr[pr]).reshape(2, SSD_HEAD_DIM, SSD_STATE)


def _ssd_ffn_prompt(x_prompt, o5, norm_mix_w, w_int, conv_w, conv_b, dtb, alog, drep, snw,
                    w_out_b, norm_ffn_w, w1, w2, norm_final_w, x1_sample):
    nj = SEQ // TL
    n_tiles = BATCH * nj
    c2 = lambda t: (0, 0)
    once = pl.Buffered(1)
    row = lambda n: pl.BlockSpec((1, n), c2, pipeline_mode=once)
    mat = lambda r, c: pl.BlockSpec((r, c), c2, pipeline_mode=once)
    wrows = lambda n, i: pl.BlockSpec((n, D_MODEL), lambda t: (i, 0), pipeline_mode=once)
    xbc0 = (D_S5 + D_SSD) // XBC_BLK

    def tile(t):
        tt = jnp.minimum(t, n_tiles - 1)
        return tt // nj, lax.rem(tt, nj)

    def prev(t):
        tp = jnp.maximum(t - 1, 0)
        return tp // nj, lax.rem(tp, nj)

    return pl.pallas_call(
        _ssd_ffn_kernel, grid=(n_tiles + 1,),
        in_specs=[pl.BlockSpec((None, TL, D_MODEL), lambda t: (*tile(t), 0)),
                  pl.BlockSpec((TL, D_MODEL), lambda t: tile(t)[::-1]),
                  row(D_MODEL),
                  wrows(D_SSD, D_S5 // D_SSD),
                  wrows(XBC_BLK, xbc0), wrows(XBC_BLK, xbc0 + 1), wrows(XBC_BLK, xbc0 + 2),
                  wrows(SSD_HEADS, (D_S5 + D_SSD + D_CONV) // SSD_HEADS),
                  mat(SSD_CONV, D_CONV), row(D_CONV),
                  row(LANES), row(LANES), row(D_SSD), row(D_SSD),
                  wrows(D_SSD, D_S5 // D_SSD),
                  row(D_MODEL),
                  *[pl.BlockSpec((D_MODEL, D_FF // 4), lambda t, c=c: (0, c), pipeline_mode=once)
                    for c in range(4)],
                  *[pl.BlockSpec((D_FF // 4, D_MODEL), lambda t, c=c: (c, 0), pipeline_mode=once)
                    for c in range(4)],
                  row(D_MODEL),
                  mat(DEC_BATCH, D_MODEL)],
        out_specs=[pl.BlockSpec((None, TL, D_MODEL), lambda t: (*prev(t), 0)),
                   pl.BlockSpec((None, SSD_HEADS, SSD_HEAD_DIM, SSD_STATE), lambda t: (tile(t)[0], 0, 0, 0)),
                   pl.BlockSpec((None, SUBLANES, D_CONV), lambda t: (tile(t)[0], 0, 0)),
                   pl.BlockSpec((DEC_BATCH, D_MODEL), c2)],
        out_shape=(jax.ShapeDtypeStruct((BATCH, SEQ, D_MODEL), F32),
                   jax.ShapeDtypeStruct((BATCH, SSD_HEADS, SSD_HEAD_DIM, SSD_STATE), F32),
                   jax.ShapeDtypeStruct((BATCH, SUBLANES, D_CONV), F32),
                   jax.ShapeDtypeStruct((DEC_BATCH, D_MODEL), F32)),
        scratch_shapes=[pltpu.VMEM((SUBLANES + TL, D_CONV), F32),
                        pltpu.VMEM((TL, D_SSD), F32),
                        pltpu.VMEM((TL, 2 * SSD_STATE), F32),
                        pltpu.VMEM((TL, 2 * SSD_STATE), F32),
                        pltpu.VMEM((TL, D_SSD), F32),
                        pltpu.VMEM((TL, LANES), F32),
                        pltpu.VMEM((TL, LANES), F32),
                        pltpu.VMEM((TL, D_SSD), F32),
                        pltpu.VMEM((SSD_HEADS // 2, SSD_STATE, LANES), F32),
                        pltpu.VMEM((TL, D_MODEL), F32),
                        pltpu.VMEM((TL, D_MODEL), BF16),
                        pltpu.VMEM((TL, D_FF // 4), BF16),
                        pltpu.VMEM((TL, D_MODEL), F32)],
        compiler_params=pltpu.CompilerParams(dimension_semantics=("arbitrary",),
                                             vmem_limit_bytes=VMEM_LIMIT_FUSED),
        name="ssd_ffn",
    )(x_prompt, o5, norm_mix_w.reshape(1, D_MODEL), w_int, w_int, w_int, w_int, w_int,
      conv_w, conv_b, dtb, alog, drep, snw, w_out_b,
      norm_ffn_w.reshape(1, D_MODEL), w1, w1, w1, w1, w2, w2, w2, w2,
      norm_final_w.reshape(1, D_MODEL), x1_sample)


def _sample_inproj_kernel(x_ref, nw_ref, wu_ref, wz_ref, wxbc0_ref, wxbc1_ref, wxbc2_ref, wdt_ref,
                          cw_ref, cb_ref, cbuf_ref, dtb_ref, alog_ref,
                          u_ref, z_ref, xs_ref, xdt_ref, b_ref, c_ref, dec_ref, nconv_ref):
    hn = _rms(x_ref[...], nw_ref[...]).astype(BF16)
    u_ref[...] = _dot_nt(wu_ref[...], hn)
    z_ref[...] = _dot_nt(hn, wz_ref[...])
    xbc_t = jnp.concatenate([_dot_nt(w_ref[...], hn) for w_ref in (wxbc0_ref, wxbc1_ref, wxbc2_ref)],
                            axis=0)
    wdt = jnp.concatenate([wdt_ref[...], jnp.zeros((LANES - SSD_HEADS, D_MODEL), BF16)], axis=0)
    dt = jax.nn.softplus(_dot_nt(hn, wdt) + dtb_ref[...])
    dec_ref[...] = jnp.exp(dt * (-jnp.exp(alog_ref[...])))
    head_of_col = lax.shift_right_logical(_iota((LANES, D_SSD), 1), 6)
    dt_rep = _dot(dt, (_iota((LANES, D_SSD), 0) == head_of_col).astype(F32), HI)
    cw_t = jnp.transpose(jnp.concatenate(
        [cw_ref[...], cb_ref[...], jnp.zeros((LANES - SSD_CONV - 1, D_CONV), F32)], axis=0))
    conv_t = cw_t[:, SSD_CONV:SSD_CONV + 1] + xbc_t * cw_t[:, SSD_CONV - 1:SSD_CONV]
    for k in range(SSD_CONV - 1):
        conv_t = conv_t + cbuf_ref[k] * cw_t[:, k:k + 1]
        if k > 0:
            nconv_ref[k - 1] = cbuf_ref[k]
    nconv_ref[SSD_CONV - 2] = xbc_t
    conv = jnp.transpose(_silu(conv_t))
    xs_ref[...] = conv[:, :D_SSD]
    xdt_ref[...] = conv[:, :D_SSD] * dt_rep
    b_ref[...] = conv[:, D_SSD:D_SSD + 2 * SSD_STATE]
    c_ref[...] = conv[:, D_SSD + 2 * SSD_STATE:]


def _sample_s5_kernel(u_ref, hre_ref, him_ref, a1c_ref, bbt_ref, cneg_ref, dcb_ref,
                      nre_ref, nim_ref, y_ref):
    for i in range(GS):
        u = u_ref[i]
        bu = _dot_tn(bbt_ref[i], u, HI)
        a_re, a_im = a1c_ref[i, 0], a1c_ref[i, 1]
        h_re, h_im = hre_ref[i], him_ref[i]
        n_re = a_re * h_re - a_im * h_im + bu[0:S5_STATE]
        n_im = a_re * h_im + a_im * h_re + bu[S5_STATE:]
        nre_ref[i] = n_re
        nim_ref[i] = n_im
        y_ref[i] = _dot(cneg_ref[i], jnp.concatenate([n_re, n_im], axis=0), HI) + dcb_ref[i] * u


def _sample_ssd_kernel(dec_ref, h0_ref, xdt_ref, xs_ref, b_ref, c_ref, drep_ref, hn_ref, y_ref):
    blk = pl.program_id(0)
    hpg = SSD_HEADS // 2
    half = D_SSD // 2
    xt = jnp.transpose(jnp.concatenate(
        [xdt_ref[...], jnp.zeros((LANES - SB, D_SSD), F32)], axis=0))
    c_b = c_ref[...].astype(BF16)
    rowi = _iota((SB, half), 0)
    ys = [jnp.zeros((SB, half), F32) for _ in range(2)]
    for jj in range(SB):
        seq = blk * SB + jj
        for g in range(2):
            brow = b_ref[jj:jj + 1, g * SSD_STATE:(g + 1) * SSD_STATE]
            parts = []
            for h in range(g * hpg, (g + 1) * hpg):
                xcol = xt[h * SSD_HEAD_DIM:(h + 1) * SSD_HEAD_DIM, jj:jj + 1]
                hn = dec_ref[seq, h] * h0_ref[jj, h] + xcol * brow
                hn_ref[jj, h] = hn
                parts.append(hn.astype(BF16))
            y_all = _dot_nt(c_b[:, g * SSD_STATE:(g + 1) * SSD_STATE], jnp.concatenate(parts, axis=0))
            ys[g] = jnp.where(rowi == jj, y_all, ys[g])
    y_ref[...] = jnp.concatenate(ys, axis=1) + drep_ref[...] * xs_ref[...]


def _sample_mix_kernel(x_ref, y5_ref, ys_ref, z_ref, wglut_ref, bglu_ref, nw5_ref, snw_ref,
                       wout_ref, x1_ref):
    g = jax.nn.gelu(jnp.transpose(y5_ref[...]))
    out = g * jax.nn.sigmoid(_dot_nt(g.astype(BF16), wglut_ref[...]) + bglu_ref[...])
    y5 = _rms(out, nw5_ref[...])
    y = ys_ref[...] * _silu(z_ref[...])
    half = D_SSD // 2
    yn = jnp.concatenate(
        [_rms(y[:, i * half:(i + 1) * half], snw_ref[:, i * half:(i + 1) * half]) for i in range(2)],
        axis=1)
    x1_ref[...] = (x_ref[...] + _dot(y5.astype(BF16), wout_ref[0:D_S5, :])
                   + _dot(yn.astype(BF16), wout_ref[D_S5:, :]))


def _sample_layer(x_sample, st_re, st_im, st_ssd, st_conv, norm_mix_w, w_int,
                  conv_w, conv_b, dtb, alog, drep, snw, prep, wglut, b_glu, s5_norm_w,
                  w_out_b):
    nb = DEC_BATCH
    a1c, bbt, cneg, dcb = prep[4:8]
    xs2 = x_sample.reshape(nb, D_MODEL)
    sds = lambda *s: jax.ShapeDtypeStruct(s, F32)
    whole = lambda shape: pl.BlockSpec(shape, lambda i: (0,) * len(shape))
    wrows = lambda n, blk: pl.BlockSpec((n, D_MODEL), lambda i: (blk, 0))
    xbc0 = (D_S5 + D_SSD) // XBC_BLK
    inproj_out = (sds(D_S5, nb), sds(nb, D_SSD), sds(nb, D_SSD), sds(nb, D_SSD),
                  sds(nb, 2 * SSD_STATE), sds(nb, 2 * SSD_STATE), sds(nb, LANES),
                  sds(SSD_CONV - 1, D_CONV, nb))
    u_t, z, xs, xdt, bm, cm, dec, nconv_t = pl.pallas_call(
        _sample_inproj_kernel, grid=(1,),
        in_specs=[whole((nb, D_MODEL)), whole((1, D_MODEL)),
                  wrows(D_S5, 0), wrows(D_SSD, D_S5 // D_SSD),
                  wrows(XBC_BLK, xbc0), wrows(XBC_BLK, xbc0 + 1), wrows(XBC_BLK, xbc0 + 2),
                  wrows(SSD_HEADS, (D_S5 + D_SSD + D_CONV) // SSD_HEADS),
                  whole((SSD_CONV, D_CONV)), whole((1, D_CONV)),
                  whole((SSD_CONV - 1, D_CONV, nb)), whole((1, LANES)), whole((1, LANES))],
        out_specs=[whole(s.shape) for s in inproj_out],
        out_shape=inproj_out,
        compiler_params=_cp("arbitrary"), name="sample_inproj",
    )(xs2, norm_mix_w.reshape(1, D_MODEL), w_int, w_int, w_int, w_int, w_int, w_int,
      conv_w, conv_b, jnp.transpose(st_conv, (1, 2, 0)), dtb, alog)

    def per_g(shape):
        nd = len(shape)
        return pl.BlockSpec((GS,) + shape, lambda g: (g,) + (0,) * nd)

    P = S5_STATE
    n_re_t, n_im_t, y5_t = pl.pallas_call(
        _sample_s5_kernel, grid=(S5_GROUPS // GS,),
        in_specs=[per_g((S5_CH, nb)), per_g((P, nb)), per_g((P, nb)), per_g((2, P, LANES)),
                  per_g((S5_CH, 2 * P)), per_g((S5_CH, 2 * P)), per_g((S5_CH, LANES))],
        out_specs=[per_g((P, nb)), per_g((P, nb)), per_g((S5_CH, nb))],
        out_shape=(sds(S5_GROUPS, P, nb), sds(S5_GROUPS, P, nb), sds(S5_GROUPS, S5_CH, nb)),
        compiler_params=_cp("arbitrary"), name="sample_s5",
    )(u_t.reshape(S5_GROUPS, S5_CH, nb), jnp.transpose(st_re, (1, 2, 0)),
      jnp.transpose(st_im, (1, 2, 0)), a1c, bbt, cneg, dcb)
    y5 = y5_t.reshape(D_S5, nb)

    smem = pl.BlockSpec(memory_space=pltpu.SMEM)
    blk2 = lambda n: pl.BlockSpec((SB, n), lambda i: (i, 0))
    st_spec = pl.BlockSpec((SB, SSD_HEADS, SSD_HEAD_DIM, SSD_STATE), lambda i: (i, 0, 0, 0))
    hn_ssd, ys = pl.pallas_call(
        _sample_ssd_kernel, grid=(nb // SB,),
        in_specs=[smem, st_spec, blk2(D_SSD), blk2(D_SSD), blk2(2 * SSD_STATE),
                  blk2(2 * SSD_STATE), pl.BlockSpec((1, D_SSD), lambda i: (0, 0))],
        out_specs=[st_spec, blk2(D_SSD)],
        out_shape=(sds(nb, SSD_HEADS, SSD_HEAD_DIM, SSD_STATE), sds(nb, D_SSD)),
        compiler_params=_cp("arbitrary"), name="sample_ssd",
    )(dec[:, :SSD_HEADS], st_ssd, xdt, xs, bm, cm, drep)

    x1 = pl.pallas_call(
        _sample_mix_kernel, out_shape=sds(nb, D_MODEL),
        compiler_params=_cp(), name="sample_mix",
    )(xs2, y5, ys, z, wglut, b_glu.reshape(1, D_S5), s5_norm_w.reshape(1, D_S5), snw, w_out_b)
    to_seq_major = lambda a: jnp.transpose(a, (2, 0, 1))
    return x1, to_seq_major(n_re_t), to_seq_major(n_im_t), hn_ssd, to_seq_major(nconv_t)


def kernel(x_prompt, x_sample, state_s5_re, state_s5_im, state_ssd, state_conv, norm_mix_w, w_in, s5_lam_re, s5_lam_im, s5_log_dt, s5_b_re, s5_b_im, s5_c_re, s5_c_im, s5_d, s5_w_glu, s5_b_glu, s5_norm_w, ssd_conv_w, ssd_conv_b, ssd_a_log, ssd_dt_bias, ssd_d, ssd_norm_w, w_out, norm_ffn_w, w_ff1, w_ff2, norm_final_w):
    P = S5_STATE
    w_int = jnp.transpose(w_in[0]).astype(BF16)
    wglut = jnp.transpose(s5_w_glu[0]).astype(BF16)
    w_out_b = w_out[0].astype(BF16)
    w1 = w_ff1[0].astype(BF16)
    w2 = w_ff2[0].astype(BF16)
    pad_h = lambda v: jnp.pad(v.reshape(1, SSD_HEADS), ((0, 0), (0, LANES - SSD_HEADS)))
    dtb, alog = pad_h(ssd_dt_bias[0]), pad_h(ssd_a_log[0])
    drep = jnp.repeat(ssd_d[0], SSD_HEAD_DIM).reshape(1, D_SSD)
    snw = ssd_norm_w[0].reshape(1, D_SSD)
    conv_w, conv_b = ssd_conv_w[0], ssd_conv_b[0].reshape(1, D_CONV)

    prep = _s5_prep(s5_lam_re[0], s5_lam_im[0], s5_log_dt[0], s5_b_re[0], s5_b_im[0],
                    s5_c_re[0], s5_c_im[0], s5_d[0])

    x1s, ns_re, ns_im, hn_ssd, nconv = _sample_layer(
        x_sample, state_s5_re[0], state_s5_im[0], state_ssd[0], state_conv[0], norm_mix_w[0],
        w_int, conv_w, conv_b, dtb, alog, drep, snw, prep, wglut,
        s5_b_glu[0], s5_norm_w[0], w_out_b)

    o5, hfin5 = _s5_prompt(x_prompt, norm_mix_w[0], w_int, prep, wglut, s5_b_glu[0], s5_norm_w[0],
                           w_out_b)
    y_prompt, h_ssd, ctail, y_s = _ssd_ffn_prompt(
        x_prompt, o5, norm_mix_w[0], w_int, conv_w, conv_b, dtb, alog, drep, snw, w_out_b,
        norm_ffn_w[0], w1, w2, norm_final_w, x1s)
    hfin5 = jnp.transpose(hfin5, (1, 0, 2))
    np_re, np_im = hfin5[None, :, :, :P], hfin5[None, :, :, P:]
    np_ssd = h_ssd[None]
    np_conv = ctail[None, :, SUBLANES - (SSD_CONV - 1):, :]

    y_sample = y_s.reshape(DEC_BATCH, 1, D_MODEL)
    ns_re, ns_im = ns_re[None], ns_im[None]
    ns_ssd = hn_ssd[None]
    ns_conv = nconv[None]

    return (y_prompt, y_sample, np_re, np_im, np_ssd, np_conv, ns_re, ns_im, ns_ssd, ns_conv)
```

```python
import functools

import jax
import jax.numpy as jnp
from jax import lax
from jax.experimental import pallas as pl
from jax.experimental.pallas import tpu as pltpu

F32 = jnp.float32
BF16 = jnp.bfloat16
HI = lax.Precision.HIGHEST
EPS = 1e-5
LOG2_E = 1.4426950408889634

D_MODEL = 1024
BATCH = 8
SEQ = 2048
DEC_BATCH = 128
D_S5 = 1024
S5_CH = 16
S5_GROUPS = 64
S5_STATE = 64
D_SSD = 1024
SSD_HEADS = 16
SSD_HEAD_DIM = 64
SSD_STATE = 128
SSD_CONV = 4
D_CONV = 1536
D_FF = 4096

LANES = 128
SUBLANES = 8
Q5 = 16
NC5 = SEQ // Q5
M5 = NC5 * BATCH
R5 = Q5 * S5_CH
QS = 128
TL = 512
XBC_BLK = 512
SB = 16
GS = 8
GP = 8
GC = 8
VMEM_LIMIT = 56 * 1024 * 1024
VMEM_LIMIT_FUSED = 62 * 1024 * 1024


def _cp(*sem):
    return pltpu.CompilerParams(dimension_semantics=sem, vmem_limit_bytes=VMEM_LIMIT)


def _rms(x, w):
    return x * lax.rsqrt(jnp.mean(x * x, axis=-1, keepdims=True) + EPS) * w


def _dot(a, b, precision=None):
    return jnp.dot(a, b, preferred_element_type=F32, precision=precision)


def _dot_nt(a, b, precision=None):
    return lax.dot_general(a, b, (((1,), (1,)), ((), ())), preferred_element_type=F32,
                           precision=precision)


def _dot_tn(a, b, precision=None):
    return lax.dot_general(a, b, (((0,), (0,)), ((), ())), preferred_element_type=F32,
                           precision=precision)


def _iota(shape, dim):
    return lax.broadcasted_iota(jnp.int32, shape, dim)


def _silu(x):
    return x * jax.nn.sigmoid(x)


def _dot_split(a, b):
    a_hi = a.astype(BF16)
    b_hi = b.astype(BF16)
    a_lo = (a - a_hi.astype(F32)).astype(BF16)
    b_lo = (b - b_hi.astype(F32)).astype(BF16)
    return _dot(a_hi, b_hi) + _dot(a_lo, b_hi) + _dot(a_hi, b_lo)


def _repeat_channels(x):
    return jnp.broadcast_to(x.reshape(Q5, 1, LANES), (Q5, S5_CH, LANES)).reshape(R5, LANES)


def _tile_steps(x):
    return jnp.broadcast_to(x[None], (Q5, S5_CH, LANES)).reshape(R5, LANES)


def _s5_prep_kernel(lre_ref, lim_ref, ldt_ref, cre_ref, cim_ref, d_ref, bre_ref, bim_ref,
                    ws_ref, t_ref, wo_ref, aq_ref, a1c_ref, bbt_ref, cneg_ref, dcb_ref):
    dup = lambda a: jnp.concatenate([a, a], axis=1)
    for i in range(GP):
        c_re, c_im = cre_ref[i], cim_ref[i]
        drow = jnp.concatenate([jnp.zeros((1, R5 - S5_CH), F32), d_ref[i:i + 1, :]], axis=1)
        _s5_prep_group(dup(lre_ref[i:i + 1, :]), dup(lim_ref[i:i + 1, :]), ldt_ref[i:i + 1, :],
                       jnp.concatenate([c_re, c_im], axis=1), dup(c_re), dup(c_im), drow,
                       dup(jnp.transpose(bre_ref[i])), dup(jnp.transpose(bim_ref[i])),
                       ws_ref.at[i], t_ref.at[i], wo_ref.at[i], aq_ref.at[i], a1c_ref.at[i],
                       bbt_ref.at[i], cneg_ref.at[i], dcb_ref.at[i])


def _col_bcast(row):
    return jnp.transpose(jnp.broadcast_to(row, (LANES, LANES)))


def _s5_prep_group(lr2, li2, ldt, ccat, c2re, c2im, drow, bt2re, bt2im,
                   ws_ref, t_ref, wo_ref, aq_ref, a1c_ref, bbt_ref, cneg_ref, dcb_ref):
    dt = jnp.exp(ldt)

    mag = jnp.exp(lr2 * dt)
    ab_re = mag * jnp.cos(li2 * dt)
    ab_im = mag * jnp.sin(li2 * dt)
    den = lr2 * lr2 + li2 * li2
    nr, ni = ab_re - 1.0, ab_im
    f_re, f_im = (nr * lr2 + ni * li2) / den, (ni * lr2 - nr * li2) / den

    def powers(expo):
        p_re, p_im = jnp.ones(expo.shape, F32), jnp.zeros(expo.shape, F32)
        s_re, s_im = ab_re, ab_im
        for k in range(Q5.bit_length()):
            take = (lax.shift_right_logical(expo, k) & 1) == 1
            p_re, p_im = (jnp.where(take, p_re * s_re - p_im * s_im, p_re),
                          jnp.where(take, p_re * s_im + p_im * s_re, p_im))
            s_re, s_im = s_re * s_re - s_im * s_im, 2.0 * s_re * s_im
        return p_re, p_im

    lane1 = _iota((1, LANES), 1) < S5_STATE
    lo16 = _iota((S5_CH, LANES), 1) < S5_STATE
    bb_a = f_re * bt2re - f_im * bt2im
    bb_b = f_re * bt2im + f_im * bt2re
    bbt = jnp.where(lo16, bb_a, bb_b)
    bbt_sw = jnp.where(lo16, bb_b, bb_a)

    rev = SUBLANES * ((Q5 + SUBLANES) // SUBLANES)
    r = _iota((rev + Q5, LANES), 0)
    tab_re, tab_im = powers(jnp.where(r < rev, r, (rev + Q5 - 1) - r))
    aq_re, aq_im = tab_re[Q5:Q5 + 1, :], tab_im[Q5:Q5 + 1, :]
    a1_re, a1_im = tab_re[1:2, :], tab_im[1:2, :]
    aq_ref[...] = jnp.concatenate(
        [aq_re, jnp.where(lane1, -aq_im, aq_im), jnp.where(lane1, aq_im, -aq_im),
         jnp.zeros((5, LANES), F32)], axis=0)

    a1c_ref[0] = _col_bcast(a1_re)[0:S5_STATE, :]
    a1c_ref[1] = _col_bcast(a1_im)[0:S5_STATE, :]
    bbt_ref[...] = bbt
    cneg_ref[...] = jnp.where(lo16, c2re, -c2im)
    dcb_ref[...] = _col_bcast(drow[:, R5 - LANES:])[LANES - S5_CH:, :]

    ain_re, ain_im = _repeat_channels(tab_re[rev:rev + Q5, :]), _repeat_channels(tab_im[rev:rev + Q5, :])
    sign = jnp.where(lane1, -1.0, 1.0)
    ws_t = ain_re * _tile_steps(bbt) + (sign * ain_im) * _tile_steps(bbt_sw)
    ws = jnp.transpose(ws_t)
    ws_ref[...] = ws.astype(BF16)

    upper = _iota((LANES, R5), 0) < S5_STATE
    vmat = _dot_split(ccat, jnp.where(upper, ws, -ws))
    vrow = _iota((S5_CH, R5), 0)
    vcol = _iota((S5_CH, R5), 1)
    vmat = vmat + jnp.where(vcol == (Q5 - 1) * S5_CH + vrow, drow, 0.0)
    for l in range(Q5):
        width = (l + 1) * S5_CH
        rolled = vmat if width == R5 else pltpu.roll(vmat, width, axis=1)
        t_ref[l * S5_CH:(l + 1) * S5_CH, :] = jnp.where(vcol < width, rolled, 0.0).astype(BF16)

    aout_re, aout_im = _repeat_channels(tab_re[1:Q5 + 1, :]), _repeat_channels(tab_im[1:Q5 + 1, :])
    c2re_t, c2im_t = _tile_steps(c2re), _tile_steps(c2im)
    wo_ref[...] = jnp.where(_iota((R5, LANES), 1) < S5_STATE,
                            c2re_t * aout_re - c2im_t * aout_im,
                            -(c2re_t * aout_im + c2im_t * aout_re)).astype(BF16)


def _s5_prep(lam_re, lam_im, log_dt, b_re, b_im, c_re, c_im, d):
    G, P, C = S5_GROUPS, S5_STATE, S5_CH
    args = (lam_re, lam_im, log_dt.reshape(G, 1), c_re, c_im, d.reshape(G, C), b_re, b_im)
    spec = lambda a: pl.BlockSpec((GP,) + a.shape[1:], lambda g, nd=a.ndim: (g,) + (0,) * (nd - 1))
    out_shapes = (
        jax.ShapeDtypeStruct((G, 2 * P, R5), BF16),
        jax.ShapeDtypeStruct((G, R5, R5), BF16),
        jax.ShapeDtypeStruct((G, R5, 2 * P), BF16),
        jax.ShapeDtypeStruct((G, 8, 2 * P), F32),
        jax.ShapeDtypeStruct((G, 2, P, LANES), F32),
        jax.ShapeDtypeStruct((G, C, 2 * P), F32),
        jax.ShapeDtypeStruct((G, C, 2 * P), F32),
        jax.ShapeDtypeStruct((G, C, LANES), F32),
    )
    return pl.pallas_call(
        _s5_prep_kernel, grid=(G // GP,),
        in_specs=[spec(a) for a in args],
        out_specs=[pl.BlockSpec((GP,) + s.shape[1:], lambda g, nd=len(s.shape): (g,) + (0,) * (nd - 1))
                   for s in out_shapes],
        out_shape=out_shapes, compiler_params=_cp("arbitrary"), name="s5_prep",
    )(*args)


def _s5_inproj_kernel(x_ref, nw_ref, wut_ref, ut_ref):
    x = x_ref[...].reshape(M5, D_MODEL)
    hn = _rms(x, nw_ref[...]).astype(BF16)
    ut = _dot_nt(wut_ref[...], hn)
    ut_ref[...] = ut.astype(BF16).reshape(S5_GROUPS, S5_CH, M5)


def _s5_core_kernel(ut_ref, t_ref, ws_ref, wo_ref, aq_ref, y_ref, hfin_ref,
                    hs_scr, sp_scr, sq_scr, yin_scr):
    for i in range(GC):
        ut = ut_ref[i]
        sp = jnp.transpose(_dot(ws_ref[i], ut))
        sp_scr[i] = sp
        sq_scr[i] = pltpu.roll(sp, S5_STATE, axis=1)
        yin_scr[i] = _dot(t_ref[i], ut)
    for i in range(GC):
        a1 = jnp.broadcast_to(aq_ref[i, 0:1, :], (BATCH, LANES))
        a2 = jnp.broadcast_to(aq_ref[i, 1:2, :], (BATCH, LANES))
        a3 = jnp.broadcast_to(aq_ref[i, 2:3, :], (BATCH, LANES))
        hp = jnp.zeros((BATCH, LANES), F32)
        hq = jnp.zeros((BATCH, LANES), F32)
        for k in range(NC5):
            rows = pl.ds(k * BATCH, BATCH)
            hs_scr[i, rows, :] = hp
            hp, hq = (a1 * hp + a2 * hq + sp_scr[i, rows, :],
                      a1 * hq + a3 * hp + sq_scr[i, rows, :])
        hfin_ref[i] = hp
    for i in range(GC):
        y = yin_scr[i] + _dot_nt(wo_ref[i], hs_scr[i].astype(BF16))
        y_ref[i] = y.astype(BF16)


def _s5_out_kernel(g_ref, wglut_ref, bglu_ref, nw_ref, wout_ref, o_ref):
    g = jax.nn.gelu(g_ref[...].astype(F32).reshape(D_S5, M5))
    z = _dot(wglut_ref[...], g.astype(BF16)) + bglu_ref[...]
    out = g * jax.nn.sigmoid(z)
    ms = jnp.mean(out * out, axis=0, keepdims=True)
    y5 = out * lax.rsqrt(ms + EPS) * nw_ref[...]
    o = _dot_tn(y5.astype(BF16), wout_ref[...])
    o_ref[...] = o.reshape(NC5, BATCH, D_MODEL)


def _s5_prompt(x_prompt, norm_mix_w, w_int, prep, wglut, b_glu, s5_norm_w, w_out_b):
    ws, tmat, wo, aq = prep[:4]
    x_v = jnp.transpose(x_prompt, (1, 0, 2)).reshape(NC5, Q5, BATCH, D_MODEL)
    tok_blk = pl.BlockSpec((NC5, None, BATCH, D_MODEL), lambda s: (0, s, 0, 0))
    chan_blk = pl.BlockSpec((S5_GROUPS, S5_CH, M5), lambda s: (0, s, 0))
    const2 = lambda s: (0, 0)
    once = pl.Buffered(1)
    ut = pl.pallas_call(
        _s5_inproj_kernel, grid=(Q5,),
        in_specs=[tok_blk,
                  pl.BlockSpec((1, D_MODEL), const2),
                  pl.BlockSpec((D_S5, D_MODEL), const2, pipeline_mode=once)],
        out_specs=chan_blk,
        out_shape=jax.ShapeDtypeStruct((S5_GROUPS, R5, M5), BF16),
        compiler_params=_cp("arbitrary"), name="s5_inproj",
    )(x_v, norm_mix_w.reshape(1, D_MODEL), w_int)

    per_g = lambda shape: pl.BlockSpec((GC,) + shape, lambda g: (g, 0, 0))
    y5_pre, hfin = pl.pallas_call(
        _s5_core_kernel, grid=(S5_GROUPS // GC,),
        in_specs=[per_g((R5, M5)), per_g((R5, R5)), per_g((LANES, R5)), per_g((R5, LANES)),
                  per_g((8, LANES))],
        out_specs=[per_g((R5, M5)), per_g((BATCH, LANES))],
        out_shape=(jax.ShapeDtypeStruct((S5_GROUPS, R5, M5), BF16),
                   jax.ShapeDtypeStruct((S5_GROUPS, BATCH, LANES), F32)),
        scratch_shapes=[pltpu.VMEM((GC, M5, LANES), F32) for _ in range(3)]
        + [pltpu.VMEM((GC, R5, M5), F32)],
        compiler_params=_cp("arbitrary"), name="s5_core",
    )(ut, tmat, ws, wo, aq)

    o5 = pl.pallas_call(
        _s5_out_kernel, grid=(Q5,),
        in_specs=[chan_blk,
                  pl.BlockSpec((D_S5, D_S5), const2, pipeline_mode=once),
                  pl.BlockSpec((D_S5, 1), const2),
                  pl.BlockSpec((D_S5, 1), const2),
                  pl.BlockSpec((D_S5, D_MODEL), const2, pipeline_mode=once)],
        out_specs=tok_blk,
        out_shape=jax.ShapeDtypeStruct((NC5, Q5, BATCH, D_MODEL), F32),
        compiler_params=_cp("arbitrary"), name="s5_out",
    )(y5_pre, wglut, b_glu.reshape(D_S5, 1), s5_norm_w.reshape(D_S5, 1), w_out_b)
    return o5.reshape(SEQ, BATCH * D_MODEL), hfin


assert D_SSD == 2 * XBC_BLK and D_CONV == 3 * XBC_BLK and 2 * SSD_STATE * 2 == XBC_BLK


def _ssd_ffn_kernel(x_ref, o5_ref, nw_ref, wz_ref, wxbc0_ref, wxbc1_ref, wxbc2_ref, wdt_ref,
                    cw_ref, cb_ref, dtb_ref, alog_ref, drep_ref, snw_ref, wout_ref,
                    fnw_ref, w1a_ref, w1b_ref, w1c_ref, w1d_ref, w2a_ref, w2b_ref, w2c_ref, w2d_ref,
                    fw_ref, x1s_ref,
                    y_ref, hout_ref, cout_ref, ys_ref,
                    xp_scr, xs_scr, b_scr, c_scr, z_scr, a_scr, dt_scr, y_scr, h_scr, x1_scr,
                    hf_scr, h1_scr, acc_scr):
    wxbc_refs = (wxbc0_ref, wxbc1_ref, wxbc2_ref)
    t = pl.program_id(0)
    n_tiles = pl.num_programs(0) - 1
    j = lax.rem(t, SEQ // TL)

    @pl.when(t == 0)
    def _():
        x1s = x1s_ref[...]
        x1_scr[0:DEC_BATCH, :] = x1s
        x1_scr[DEC_BATCH:, :] = jnp.zeros((TL - DEC_BATCH, D_MODEL), F32)
        hf_scr[0:DEC_BATCH, :] = _rms(x1s, fnw_ref[...]).astype(BF16)
        hf_scr[DEC_BATCH:, :] = jnp.zeros((TL - DEC_BATCH, D_MODEL), BF16)

    @pl.when(j == 0)
    def _():
        xp_scr[0:SUBLANES, :] = jnp.zeros((SUBLANES, D_CONV), F32)
        h_scr[...] = jnp.zeros(h_scr.shape, F32)

    w1_refs = (w1a_ref, w1b_ref, w1c_ref, w1d_ref)
    w2_refs = (w2a_ref, w2b_ref, w2c_ref, w2d_ref)
    sl = D_FF // 16

    def ffn_up(c, q):
        cols = slice(q * sl, (q + 1) * sl)
        h1 = jnp.square(jnp.maximum(_dot(hf_scr[...], w1_refs[c][:, cols]), 0.0))
        h1_scr[:, cols] = h1.astype(BF16)

    def ffn_down(c, q):
        cols = slice(q * sl, (q + 1) * sl)
        acc_scr[:, cols] += _dot(h1_scr[...], w2_refs[c][:, cols])

    ffn_slices = iter([functools.partial(f, c, q) for c in range(4)
                       for f in (ffn_up, ffn_down) for q in range(4)])

    def ffn_step(n=1):
        for _ in range(n):
            next(ffn_slices)()

    ffn_step(2)

    x = x_ref[...]
    hn = _rms(x, nw_ref[...]).astype(BF16)
    acc_scr[...] = x1_scr[...]

    def conv_block(i):
        cols = slice(i * XBC_BLK, (i + 1) * XBC_BLK)
        xp_scr[SUBLANES:, cols] = _dot_nt(hn, wxbc_refs[i][...])
        conv = cb_ref[:, cols] + sum(
            xp_scr[SUBLANES - (SSD_CONV - 1) + k:SUBLANES - (SSD_CONV - 1) + k + TL, cols]
            * cw_ref[k:k + 1, cols] for k in range(SSD_CONV))
        tail = xp_scr[TL:TL + SUBLANES, cols]
        xp_scr[0:SUBLANES, cols] = tail
        cout_ref[:, cols] = tail
        return _silu(conv)

    xs_scr[:, 0:XBC_BLK] = conv_block(0)
    xs_scr[:, XBC_BLK:2 * XBC_BLK] = conv_block(1)
    bc = conv_block(2)
    b_scr[...] = bc[:, :2 * SSD_STATE]
    c_scr[...] = bc[:, 2 * SSD_STATE:]
    z_scr[...] = _dot_nt(hn, wz_ref[...])
    wdt = jnp.concatenate([wdt_ref[...], jnp.zeros((LANES - SSD_HEADS, D_MODEL), BF16)], axis=0)
    dt = jax.nn.softplus(_dot_nt(hn, wdt) + dtb_ref[...])
    dt_scr[...] = dt
    a_scr[...] = dt * (-jnp.exp(alog_ref[...]) * LOG2_E)
    ffn_step(5)

    li = _iota((QS, QS), 0)
    si = _iota((QS, QS), 1)
    causal = li >= si
    lo = si < SSD_HEAD_DIM

    def chunk(ci, carry):
        rows = pl.ds(ci * QS, QS)
        acum = a_scr[rows, :]
        for sh in (1, 2, 4, 8, 16, 32, 64):
            acum = acum + jnp.where(li >= sh, pltpu.roll(acum, sh, axis=0), 0.0)
        acum_t = jnp.transpose(acum)
        asrc_t = acum_t - jnp.log2(jnp.transpose(dt_scr[rows, :]))
        alast = acum[QS - 1:QS, :]
        alast_t = acum_t[:, QS - 1:QS]
        for g in range(2):
            bg = b_scr[rows, g * SSD_STATE:(g + 1) * SSD_STATE]
            cg_b = c_scr[rows, g * SSD_STATE:(g + 1) * SSD_STATE].astype(BF16)
            bt = jnp.transpose(bg)
            cb = _dot(cg_b, bt.astype(BF16))
            h_grp = jnp.concatenate([h_scr[4 * g + i] for i in range(4)], axis=1)
            y_off = _dot(cg_b, h_grp.astype(BF16))
            for hp in range(4):
                pr = 4 * g + hp
                cols = slice(pr * LANES, (pr + 1) * LANES)
                xs_pair = xs_scr[rows, cols]
                xs_b = xs_pair.astype(BF16)
                zero = jnp.zeros_like(xs_b)
                xs_cat = jnp.concatenate([jnp.where(lo, xs_b, zero), jnp.where(lo, zero, xs_b)], axis=0)
                acols, ms, btss = [], [], []
                for h in (2 * pr, 2 * pr + 1):
                    acol = jnp.broadcast_to(acum[:, h:h + 1], (QS, QS))
                    asrc = asrc_t[h:h + 1, :]
                    lmat = jnp.exp2(jnp.where(causal, acol - asrc, -1e30))
                    ms.append((cb * lmat).astype(BF16))
                    btss.append((bt * jnp.exp2(alast_t[h:h + 1, :] - asrc)).astype(BF16))
                    acols.append(acol)
                y = _dot(jnp.concatenate(ms, axis=1), xs_cat)
                st = _dot(jnp.concatenate(btss, axis=1), xs_cat)
                h0, h1 = 2 * pr, 2 * pr + 1
                decay = jnp.exp2(jnp.where(lo[0:1, :], alast[:, h0:h0 + 1], alast[:, h1:h1 + 1]))
                h_scr[pr] = decay * h_scr[pr] + st
                y = (y + jnp.exp2(jnp.where(lo, acols[0], acols[1])) * y_off[:, hp * LANES:(hp + 1) * LANES]
                     + drep_ref[:, cols] * xs_pair)
                y_scr[rows, cols] = y * _silu(z_scr[rows, cols])
                if hp != 3 or (ci == 0 and g == 0):
                    ffn_step()
        half = D_SSD // 2
        for g in range(2):
            yg = y_scr[rows, g * half:(g + 1) * half]
            yg = yg * lax.rsqrt(jnp.mean(yg * yg, axis=-1, keepdims=True) + EPS)
            y_scr[rows, g * half:(g + 1) * half] = yg * snw_ref[:, g * half:(g + 1) * half]
        return carry

    for ci in range(TL // QS):
        chunk(ci, 0)
    assert next(ffn_slices, None) is None, "every FFN slice must have been issued"
    y_ref[...] = _rms(acc_scr[...], fw_ref[...])
    for hrows in (slice(0, TL // 2), slice(TL // 2, TL)):
        x1 = x[hrows] + o5_ref[hrows, :] + _dot(y_scr[hrows, :].astype(BF16), wout_ref[...])
        x1_scr[hrows, :] = x1
        hf_scr[hrows, :] = _rms(x1, fnw_ref[...]).astype(BF16)

    @pl.when(t == 0)
    def _():
        ys_ref[...] = y_ref[0:DEC_BATCH, :]

    @pl.when(jnp.logical_and(j == SEQ // TL - 1, t < n_tiles))
    def _():
        for pr in range(SSD_HEADS // 2):
            hout_ref[2 * pr:2 * pr + 2] = jnp.transpose(h_scr[pr]).reshape(2, SSD_HEAD_DIM, SSD_STATE)


def _ssd_ffn_prompt(x_prompt, o5, norm_mix_w, w_int, conv_w, conv_b, dtb, alog, drep, snw,
                    w_out_b, norm_ffn_w, w1, w2, norm_final_w, x1_sample):
    nj = SEQ // TL
    n_tiles = BATCH * nj
    c2 = lambda t: (0, 0)
    once = pl.Buffered(1)
    row = lambda n: pl.BlockSpec((1, n), c2, pipeline_mode=once)
    mat = lambda r, c: pl.BlockSpec((r, c), c2, pipeline_mode=once)
    wrows = lambda n, i: pl.BlockSpec((n, D_MODEL), lambda t: (i, 0), pipeline_mode=once)
    xbc0 = (D_S5 + D_SSD) // XBC_BLK

    def tile(t):
        tt = jnp.minimum(t, n_tiles - 1)
        return tt // nj, lax.rem(tt, nj)

    def prev(t):
        tp = jnp.maximum(t - 1, 0)
        return tp // nj, lax.rem(tp, nj)

    return pl.pallas_call(
        _ssd_ffn_kernel, grid=(n_tiles + 1,),
        in_specs=[pl.BlockSpec((None, TL, D_MODEL), lambda t: (*tile(t), 0)),
                  pl.BlockSpec((TL, D_MODEL), lambda t: tile(t)[::-1]),
                  row(D_MODEL),
                  wrows(D_SSD, D_S5 // D_SSD),
                  wrows(XBC_BLK, xbc0), wrows(XBC_BLK, xbc0 + 1), wrows(XBC_BLK, xbc0 + 2),
                  wrows(SSD_HEADS, (D_S5 + D_SSD + D_CONV) // SSD_HEADS),
                  mat(SSD_CONV, D_CONV), row(D_CONV),
                  row(LANES), row(LANES), row(D_SSD), row(D_SSD),
                  wrows(D_SSD, D_S5 // D_SSD),
                  row(D_MODEL),
                  *[pl.BlockSpec((D_MODEL, D_FF // 4), lambda t, c=c: (0, c), pipeline_mode=once)
                    for c in range(4)],
                  *[pl.BlockSpec((D_FF // 4, D_MODEL), lambda t, c=c: (c, 0), pipeline_mode=once)
                    for c in range(4)],
                  row(D_MODEL),
                  mat(DEC_BATCH, D_MODEL)],
        out_specs=[pl.BlockSpec((None, TL, D_MODEL), lambda t: (*prev(t), 0)),
                   pl.BlockSpec((None, SSD_HEADS, SSD_HEAD_DIM, SSD_STATE), lambda t: (tile(t)[0], 0, 0, 0)),
                   pl.BlockSpec((None, SUBLANES, D_CONV), lambda t: (tile(t)[0], 0, 0)),
                   pl.BlockSpec((DEC_BATCH, D_MODEL), c2)],
        out_shape=(jax.ShapeDtypeStruct((BATCH, SEQ, D_MODEL), F32),
                   jax.ShapeDtypeStruct((BATCH, SSD_HEADS, SSD_HEAD_DIM, SSD_STATE), F32),
                   jax.ShapeDtypeStruct((BATCH, SUBLANES, D_CONV), F32),
                   jax.ShapeDtypeStruct((DEC_BATCH, D_MODEL), F32)),
        scratch_shapes=[pltpu.VMEM((SUBLANES + TL, D_CONV), F32),
                        pltpu.VMEM((TL, D_SSD), F32),
                        pltpu.VMEM((TL, 2 * SSD_STATE), F32),
                        pltpu.VMEM((TL, 2 * SSD_STATE), F32),
                        pltpu.VMEM((TL, D_SSD), F32),
                        pltpu.VMEM((TL, LANES), F32),
                        pltpu.VMEM((TL, LANES), F32),
                        pltpu.VMEM((TL, D_SSD), F32),
                        pltpu.VMEM((SSD_HEADS // 2, SSD_STATE, LANES), F32),
                        pltpu.VMEM((TL, D_MODEL), F32),
                        pltpu.VMEM((TL, D_MODEL), BF16),
                        pltpu.VMEM((TL, D_FF // 4), BF16),
                        pltpu.VMEM((TL, D_MODEL), F32)],
        compiler_params=pltpu.CompilerParams(dimension_semantics=("arbitrary",),
                                             vmem_limit_bytes=VMEM_LIMIT_FUSED),
        name="ssd_ffn",
    )(x_prompt, o5, norm_mix_w.reshape(1, D_MODEL), w_int, w_int, w_int, w_int, w_int,
      conv_w, conv_b, dtb, alog, drep, snw, w_out_b,
      norm_ffn_w.reshape(1, D_MODEL), w1, w1, w1, w1, w2, w2, w2, w2,
      norm_final_w.reshape(1, D_MODEL), x1_sample)


def _sample_inproj_kernel(x_ref, nw_ref, wu_ref, wz_ref, wxbc0_ref, wxbc1_ref, wxbc2_ref, wdt_ref,
                          cw_ref, cb_ref, cbuf_ref, dtb_ref, alog_ref,
                          u_ref, z_ref, xs_ref, xdt_ref, b_ref, c_ref, dec_ref, nconv_ref):
    hn = _rms(x_ref[...], nw_ref[...]).astype(BF16)
    u_ref[...] = _dot_nt(wu_ref[...], hn)
    z_ref[...] = _dot_nt(hn, wz_ref[...])
    xbc_t = jnp.concatenate([_dot_nt(w_ref[...], hn) for w_ref in (wxbc0_ref, wxbc1_ref, wxbc2_ref)],
                            axis=0)
    wdt = jnp.concatenate([wdt_ref[...], jnp.zeros((LANES - SSD_HEADS, D_MODEL), BF16)], axis=0)
    dt = jax.nn.softplus(_dot_nt(hn, wdt) + dtb_ref[...])
    dec_ref[...] = jnp.exp(dt * (-jnp.exp(alog_ref[...])))
    head_of_col = lax.shift_right_logical(_iota((LANES, D_SSD), 1), 6)
    dt_rep = _dot(dt, (_iota((LANES, D_SSD), 0) == head_of_col).astype(F32), HI)
    cw_t = jnp.transpose(jnp.concatenate(
        [cw_ref[...], cb_ref[...], jnp.zeros((LANES - SSD_CONV - 1, D_CONV), F32)], axis=0))
    conv_t = cw_t[:, SSD_CONV:SSD_CONV + 1] + xbc_t * cw_t[:, SSD_CONV - 1:SSD_CONV]
    for k in range(SSD_CONV - 1):
        conv_t = conv_t + cbuf_ref[k] * cw_t[:, k:k + 1]
        if k > 0:
            nconv_ref[k - 1] = cbuf_ref[k]
    nconv_ref[SSD_CONV - 2] = xbc_t
    conv = jnp.transpose(_silu(conv_t))
    xs_ref[...] = conv[:, :D_SSD]
    xdt_ref[...] = conv[:, :D_SSD] * dt_rep
    b_ref[...] = conv[:, D_SSD:D_SSD + 2 * SSD_STATE]
    c_ref[...] = conv[:, D_SSD + 2 * SSD_STATE:]


def _sample_s5_kernel(u_ref, hre_ref, him_ref, a1c_ref, bbt_ref, cneg_ref, dcb_ref,
                      nre_ref, nim_ref, y_ref):
    for i in range(GS):
        u = u_ref[i]
        bu = _dot_tn(bbt_ref[i], u, HI)
        a_re, a_im = a1c_ref[i, 0], a1c_ref[i, 1]
        h_re, h_im = hre_ref[i], him_ref[i]
        n_re = a_re * h_re - a_im * h_im + bu[0:S5_STATE]
        n_im = a_re * h_im + a_im * h_re + bu[S5_STATE:]
        nre_ref[i] = n_re
        nim_ref[i] = n_im
        y_ref[i] = _dot(cneg_ref[i], jnp.concatenate([n_re, n_im], axis=0), HI) + dcb_ref[i] * u


def _sample_ssd_kernel(dec_ref, h0_ref, xdt_ref, xs_ref, b_ref, c_ref, drep_ref, after_ref,
                       hn_ref, y_ref):
    del after_ref
    blk = pl.program_id(0)
    hpg = SSD_HEADS // 2
    half = D_SSD // 2
    xt = jnp.transpose(jnp.concatenate(
        [xdt_ref[...], jnp.zeros((LANES - SB, D_SSD), F32)], axis=0))
    c_b = c_ref[...].astype(BF16)
    rowi = _iota((SB, half), 0)
    ys = [jnp.zeros((SB, half), F32) for _ in range(2)]
    for jj in range(SB):
        seq = blk * SB + jj
        for g in range(2):
            brow = b_ref[jj:jj + 1, g * SSD_STATE:(g + 1) * SSD_STATE]
            parts = []
            for h in range(g * hpg, (g + 1) * hpg):
                xcol = xt[h * SSD_HEAD_DIM:(h + 1) * SSD_HEAD_DIM, jj:jj + 1]
                hn = dec_ref[seq, h] * h0_ref[jj, h] + xcol * brow
                hn_ref[jj, h] = hn
                parts.append(hn.astype(BF16))
            y_all = _dot_nt(c_b[:, g * SSD_STATE:(g + 1) * SSD_STATE], jnp.concatenate(parts, axis=0))
            ys[g] = jnp.where(rowi == jj, y_all, ys[g])
    y_ref[...] = jnp.concatenate(ys, axis=1) + drep_ref[...] * xs_ref[...]


def _sample_mix_kernel(x_ref, y5_ref, ys_ref, z_ref, wglut_ref, bglu_ref, nw5_ref, snw_ref,
                       wout_ref, x1_ref):
    g = jax.nn.gelu(jnp.transpose(y5_ref[...]))
    out = g * jax.nn.sigmoid(_dot_nt(g.astype(BF16), wglut_ref[...]) + bglu_ref[...])
    y5 = _rms(out, nw5_ref[...])
    y = ys_ref[...] * _silu(z_ref[...])
    half = D_SSD // 2
    yn = jnp.concatenate(
        [_rms(y[:, i * half:(i + 1) * half], snw_ref[:, i * half:(i + 1) * half]) for i in range(2)],
        axis=1)
    x1_ref[...] = (x_ref[...] + _dot(y5.astype(BF16), wout_ref[0:D_S5, :])
                   + _dot(yn.astype(BF16), wout_ref[D_S5:, :]))


def _sample_layer(x_sample, st_re, st_im, st_ssd, st_conv, norm_mix_w, w_int,
                  conv_w, conv_b, dtb, alog, drep, snw, prep, wglut, b_glu, s5_norm_w,
                  w_out_b):
    nb = DEC_BATCH
    a1c, bbt, cneg, dcb = prep[4:8]
    xs2 = x_sample.reshape(nb, D_MODEL)
    sds = lambda *s: jax.ShapeDtypeStruct(s, F32)
    whole = lambda shape: pl.BlockSpec(shape, lambda i: (0,) * len(shape))
    wrows = lambda n, blk: pl.BlockSpec((n, D_MODEL), lambda i: (blk, 0))
    xbc0 = (D_S5 + D_SSD) // XBC_BLK
    inproj_out = (sds(D_S5, nb), sds(nb, D_SSD), sds(nb, D_SSD), sds(nb, D_SSD),
                  sds(nb, 2 * SSD_STATE), sds(nb, 2 * SSD_STATE), sds(nb, LANES),
                  sds(SSD_CONV - 1, D_CONV, nb))
    u_t, z, xs, xdt, bm, cm, dec, nconv_t = pl.pallas_call(
        _sample_inproj_kernel, grid=(1,),
        in_specs=[whole((nb, D_MODEL)), whole((1, D_MODEL)),
                  wrows(D_S5, 0), wrows(D_SSD, D_S5 // D_SSD),
                  wrows(XBC_BLK, xbc0), wrows(XBC_BLK, xbc0 + 1), wrows(XBC_BLK, xbc0 + 2),
                  wrows(SSD_HEADS, (D_S5 + D_SSD + D_CONV) // SSD_HEADS),
                  whole((SSD_CONV, D_CONV)), whole((1, D_CONV)),
                  whole((SSD_CONV - 1, D_CONV, nb)), whole((1, LANES)), whole((1, LANES))],
        out_specs=[whole(s.shape) for s in inproj_out],
        out_shape=inproj_out,
        compiler_params=_cp("arbitrary"), name="sample_inproj",
    )(xs2, norm_mix_w.reshape(1, D_MODEL), w_int, w_int, w_int, w_int, w_int, w_int,
      conv_w, conv_b, jnp.transpose(st_conv, (1, 2, 0)), dtb, alog)

    def per_g(shape):
        nd = len(shape)
        return pl.BlockSpec((GS,) + shape, lambda g: (g,) + (0,) * nd)

    P = S5_STATE
    n_re_t, n_im_t, y5_t = pl.pallas_call(
        _sample_s5_kernel, grid=(S5_GROUPS // GS,),
        in_specs=[per_g((S5_CH, nb)), per_g((P, nb)), per_g((P, nb)), per_g((2, P, LANES)),
                  per_g((S5_CH, 2 * P)), per_g((S5_CH, 2 * P)), per_g((S5_CH, LANES))],
        out_specs=[per_g((P, nb)), per_g((P, nb)), per_g((S5_CH, nb))],
        out_shape=(sds(S5_GROUPS, P, nb), sds(S5_GROUPS, P, nb), sds(S5_GROUPS, S5_CH, nb)),
        compiler_params=_cp("arbitrary"), name="sample_s5",
    )(u_t.reshape(S5_GROUPS, S5_CH, nb), jnp.transpose(st_re, (1, 2, 0)),
      jnp.transpose(st_im, (1, 2, 0)), a1c, bbt, cneg, dcb)
    y5 = y5_t.reshape(D_S5, nb)

    smem = pl.BlockSpec(memory_space=pltpu.SMEM)
    blk2 = lambda n: pl.BlockSpec((SB, n), lambda i: (i, 0))
    st_spec = pl.BlockSpec((SB, SSD_HEADS, SSD_HEAD_DIM, SSD_STATE), lambda i: (i, 0, 0, 0))
    hn_ssd, ys = pl.pallas_call(
        _sample_ssd_kernel, grid=(nb // SB,),
        in_specs=[smem, st_spec, blk2(D_SSD), blk2(D_SSD), blk2(2 * SSD_STATE),
                  blk2(2 * SSD_STATE), pl.BlockSpec((1, D_SSD), lambda i: (0, 0)),
                  pl.BlockSpec(memory_space=pl.ANY)],
        out_specs=[st_spec, blk2(D_SSD)],
        out_shape=(sds(nb, SSD_HEADS, SSD_HEAD_DIM, SSD_STATE), sds(nb, D_SSD)),
        compiler_params=_cp("arbitrary"), name="sample_ssd",
    )(dec[:, :SSD_HEADS], st_ssd, xdt, xs, bm, cm, drep, y5_t)

    x1 = pl.pallas_call(
        _sample_mix_kernel, out_shape=sds(nb, D_MODEL),
        compiler_params=_cp(), name="sample_mix",
    )(xs2, y5, ys, z, wglut, b_glu.reshape(1, D_S5), s5_norm_w.reshape(1, D_S5), snw, w_out_b)
    to_seq_major = lambda a: jnp.transpose(a, (2, 0, 1))
    return x1, to_seq_major(n_re_t), to_seq_major(n_im_t), hn_ssd, to_seq_major(nconv_t)


def kernel(x_prompt, x_sample, state_s5_re, state_s5_im, state_ssd, state_conv, norm_mix_w, w_in, s5_lam_re, s5_lam_im, s5_log_dt, s5_b_re, s5_b_im, s5_c_re, s5_c_im, s5_d, s5_w_glu, s5_b_glu, s5_norm_w, ssd_conv_w, ssd_conv_b, ssd_a_log, ssd_dt_bias, ssd_d, ssd_norm_w, w_out, norm_ffn_w, w_ff1, w_ff2, norm_final_w):
    P = S5_STATE
    w_int = jnp.transpose(w_in[0]).astype(BF16)
    wglut = jnp.transpose(s5_w_glu[0]).astype(BF16)
    w_out_b = w_out[0].astype(BF16)
    w1 = w_ff1[0].astype(BF16)
    w2 = w_ff2[0].astype(BF16)
    pad_h = lambda v: jnp.pad(v.reshape(1, SSD_HEADS), ((0, 0), (0, LANES - SSD_HEADS)))
    dtb, alog = pad_h(ssd_dt_bias[0]), pad_h(ssd_a_log[0])
    drep = jnp.repeat(ssd_d[0], SSD_HEAD_DIM).reshape(1, D_SSD)
    snw = ssd_norm_w[0].reshape(1, D_SSD)
    conv_w, conv_b = ssd_conv_w[0], ssd_conv_b[0].reshape(1, D_CONV)

    prep = _s5_prep(s5_lam_re[0], s5_lam_im[0], s5_log_dt[0], s5_b_re[0], s5_b_im[0],
                    s5_c_re[0], s5_c_im[0], s5_d[0])

    x1s, ns_re, ns_im, hn_ssd, nconv = _sample_layer(
        x_sample, state_s5_re[0], state_s5_im[0], state_ssd[0], state_conv[0], norm_mix_w[0],
        w_int, conv_w, conv_b, dtb, alog, drep, snw, prep, wglut,
        s5_b_glu[0], s5_norm_w[0], w_out_b)

    o5, hfin5 = _s5_prompt(x_prompt, norm_mix_w[0], w_int, prep, wglut, s5_b_glu[0], s5_norm_w[0],
                           w_out_b)
    y_prompt, h_ssd, ctail, y_s = _ssd_ffn_prompt(
        x_prompt, o5, norm_mix_w[0], w_int, conv_w, conv_b, dtb, alog, drep, snw, w_out_b,
        norm_ffn_w[0], w1, w2, norm_final_w, x1s)
    hfin5 = jnp.transpose(hfin5, (1, 0, 2))
    np_re, np_im = hfin5[None, :, :, :P], hfin5[None, :, :, P:]
    np_ssd = h_ssd[None]
    np_conv = ctail[None, :, SUBLANES - (SSD_CONV - 1):, :]

    y_sample = y_s.reshape(DEC_BATCH, 1, D_MODEL)
    ns_re, ns_im = ns_re[None], ns_im[None]
    ns_ssd = hn_ssd[None]
    ns_conv = nconv[None]

    return (y_prompt, y_sample, np_re, np_im, np_ssd, np_conv, ns_re, ns_im, ns_ssd, ns_conv)
```
